```python
import functools
import jax, jax.numpy as jnp
from jax import lax
import numpy as np

D_MODEL = 1024
BATCH = 4
SEQ = 8192
DEPTH = 4

GRID_W = 64
CTX_LEN = 256
N_HEADS = 8
QK_NOPE_DIM = 64
QK_ROPE_DIM = 32
QK_HEAD_DIM = QK_NOPE_DIM + QK_ROPE_DIM
ATTN_WIDTH = D_MODEL // 2
V_HEAD_DIM = ATTN_WIDTH // N_HEADS
Q_LORA_RANK = D_MODEL // 4
KV_LORA_RANK = D_MODEL // 4
ROPE_THETA = 10000.0
Q_BLOCK = 128
FOURIER_WIDTH = D_MODEL - ATTN_WIDTH
FOURIER_GROUPS = 4
FOURIER_GROUP_DIM = FOURIER_WIDTH // FOURIER_GROUPS
KV_OFFSET = Q_LORA_RANK
KPE_OFFSET = Q_LORA_RANK + KV_LORA_RANK
FOURIER_OFFSET = KPE_OFFSET + QK_ROPE_DIM
IN_WIDTH = FOURIER_OFFSET + FOURIER_WIDTH
MIX_WIDTH = ATTN_WIDTH + FOURIER_WIDTH
FFN_DIM = 2816
N_EXPERTS = 8
TOP_K = 2
MOE_FFN_DIM = 3584
MOE_BLOCK = 512
N_DENSE = (DEPTH + 1) // 2
N_MOE = DEPTH // 2
EPS = 1e-6

kernel_name = 'hymba_mla_fnet_moe_dit_trunk'


def rms_norm(x, g):
    xf = x.astype(jnp.float32)
    y = xf * lax.rsqrt(jnp.mean(xf * xf, axis=-1, keepdims=True) + EPS)
    return (y * g.astype(jnp.float32)).astype(x.dtype)


def modulate(h, shift, scale):
    return h * (1 + scale) + shift


def axial_angles(n_tokens):
    rows = n_tokens // GRID_W
    r = jnp.repeat(jnp.arange(rows, dtype=jnp.float32), GRID_W)
    col = jnp.tile(jnp.arange(GRID_W, dtype=jnp.float32), rows)
    half = QK_ROPE_DIM // 2
    inv_freq = ROPE_THETA ** (-jnp.arange(0, half, 2, dtype=jnp.float32) / half)
    return r[:, None] * inv_freq, col[:, None] * inv_freq


def rotate(x, ang):
    n = x.shape[-1] // 2
    cos = jnp.cos(ang)[None, :, None, :].astype(x.dtype)
    sin = jnp.sin(ang)[None, :, None, :].astype(x.dtype)
    x1, x2 = x[..., :n], x[..., n:]
    return jnp.concatenate([x1 * cos - x2 * sin, x1 * sin + x2 * cos], axis=-1)


def axial_rope(x, angles):
    ang_row, ang_col = angles
    half = QK_ROPE_DIM // 2
    return jnp.concatenate([rotate(x[..., :half], ang_row), rotate(x[..., half:], ang_col)], axis=-1)


def mla_queries(p, lora_norm, w_uq, qk_gain, angles):
    b, L, _ = p.shape
    q = jnp.matmul(rms_norm(p[..., :Q_LORA_RANK], lora_norm), w_uq).reshape(b, L, N_HEADS, QK_HEAD_DIM)
    q = rms_norm(q, qk_gain)
    if angles is not None:
        q = jnp.concatenate([q[..., :QK_NOPE_DIM], axial_rope(q[..., QK_NOPE_DIM:], angles)], axis=-1)
    return q


def mla_keys_values(p, lora_norm, w_ukv, qk_gain, angles):
    b, L, _ = p.shape
    c_kv = rms_norm(p[..., KV_OFFSET:KPE_OFFSET], lora_norm)
    kv = jnp.matmul(c_kv, w_ukv).reshape(b, L, N_HEADS, QK_NOPE_DIM + V_HEAD_DIM)
    k_pe = jnp.broadcast_to(p[:, :, None, KPE_OFFSET:FOURIER_OFFSET], (b, L, N_HEADS, QK_ROPE_DIM))
    k = rms_norm(jnp.concatenate([kv[..., :QK_NOPE_DIM], k_pe], axis=-1), qk_gain)
    if angles is not None:
        k = jnp.concatenate([k[..., :QK_NOPE_DIM], axial_rope(k[..., QK_NOPE_DIM:], angles)], axis=-1)
    return k, kv[..., QK_NOPE_DIM:]


def attend(q, k, v):
    s = jnp.einsum('bqhd,bkhd->bhqk', q, k, preferred_element_type=jnp.float32) * (QK_HEAD_DIM ** -0.5)
    p = jax.nn.softmax(s, axis=-1).astype(v.dtype)
    return jnp.einsum('bhqk,bkhd->bqhd', p, v)


def latent_attention(q, k, v):
    b, L, h, dk = q.shape
    nblk = L // Q_BLOCK
    qb = q.reshape(b, nblk, Q_BLOCK, h, dk).transpose(1, 0, 2, 3, 4)
    ob = lax.map(lambda qi: attend(qi, k, v), qb)
    return ob.transpose(1, 0, 2, 3, 4).reshape(b, L, h * v.shape[-1])


def fourier_mix(f):
    b, L, _ = f.shape
    fg = f.reshape(b, L, FOURIER_GROUPS, FOURIER_GROUP_DIM).astype(jnp.float32)
    out = jnp.fft.fft2(fg, axes=(1, 3), norm='ortho').real
    return out.reshape(b, L, FOURIER_WIDTH).astype(f.dtype)


def merge_heads(attn, four, g_attn, g_four, w_out):
    return jnp.matmul(jnp.concatenate([rms_norm(attn, g_attn), rms_norm(four, g_four)], axis=-1), w_out)


def swiglu(h, wg, wu, wd):
    return jnp.matmul(jax.nn.silu(jnp.matmul(h, wg)) * jnp.matmul(h, wu), wd)


def moe_swiglu(h, w_router, w_gate, w_up, w_down):
    b, L, d = h.shape
    xs = h.reshape(b * L, d)
    n_assign = b * L * TOP_K
    logits = jnp.matmul(xs, w_router).astype(jnp.float32)
    top_logit, top_idx = lax.top_k(logits, TOP_K)
    top_w = jax.nn.softmax(top_logit, axis=-1)
    exp_flat = top_idx.reshape(n_assign)
    tok_flat = jnp.arange(n_assign, dtype=jnp.int32) // TOP_K
    w_flat = top_w.reshape(n_assign)
    order = jnp.argsort(exp_flat)
    exp_sorted = exp_flat[order]
    counts = jnp.bincount(exp_flat, length=N_EXPERTS)
    starts = jnp.cumsum(counts) - counts
    padded = (counts + MOE_BLOCK - 1) // MOE_BLOCK * MOE_BLOCK
    pad_ends = jnp.cumsum(padded)
    pad_starts = pad_ends - padded
    dest = pad_starts[exp_sorted] + jnp.arange(n_assign, dtype=jnp.int32) - starts[exp_sorted]
    n_blocks = (n_assign + N_EXPERTS * (MOE_BLOCK - 1) + MOE_BLOCK - 1) // MOE_BLOCK
    n_rows = n_blocks * MOE_BLOCK
    row_tok = jnp.zeros((n_rows,), jnp.int32).at[dest].set(tok_flat[order])
    row_w = jnp.zeros((n_rows,), jnp.float32).at[dest].set(w_flat[order])
    block_exp = jnp.minimum(
        jnp.searchsorted(pad_ends, jnp.arange(n_blocks, dtype=jnp.int32) * MOE_BLOCK, side='right'),
        N_EXPERTS - 1)

    def expert_block(args):
        tok, e = args
        return swiglu(xs[tok], w_gate[e], w_up[e], w_down[e])

    yb = lax.map(expert_block, (row_tok.reshape(n_blocks, MOE_BLOCK), block_exp))
    y = jnp.zeros_like(xs).at[row_tok].add(yb.reshape(n_rows, d) * row_w[:, None].astype(xs.dtype))
    return y.reshape(b, L, d)


def setup_inputs(seed: int = 0) -> dict:
    key = jax.random.key(seed)
    ks = jax.random.split(key, 25)
    f32 = jnp.float32

    def dense(k, shape, fan_in, gain=1.0):
        return jax.random.normal(k, shape, f32) * (gain * fan_in ** -0.5)

    def gain_vec(k, shape):
        return 1.0 + 0.02 * jax.random.normal(k, shape, f32)

    return {
        'x': jax.random.normal(ks[0], (BATCH, SEQ, D_MODEL), f32),
        'c': jax.random.normal(ks[1], (BATCH, D_MODEL), f32),
        'ctx': jax.random.normal(ks[2], (BATCH, CTX_LEN, D_MODEL), f32),
        'c_ctx': jax.random.normal(ks[3], (D_MODEL,), f32),
        'w_ada': dense(ks[4], (DEPTH, D_MODEL, 6 * D_MODEL), D_MODEL, 0.5),
        'b_ada': 0.02 * jax.random.normal(ks[5], (DEPTH, 6 * D_MODEL), f32),
        'norm1': gain_vec(ks[6], (DEPTH, D_MODEL)),
        'w_in': dense(ks[7], (DEPTH, D_MODEL, IN_WIDTH), D_MODEL),
        'q_lora_norm': gain_vec(ks[8], (DEPTH, Q_LORA_RANK)),
        'kv_lora_norm': gain_vec(ks[9], (DEPTH, KV_LORA_RANK)),
        'w_uq': dense(ks[10], (DEPTH, Q_LORA_RANK, N_HEADS * QK_HEAD_DIM), Q_LORA_RANK),
        'w_ukv': dense(ks[11], (DEPTH, KV_LORA_RANK, N_HEADS * (QK_NOPE_DIM + V_HEAD_DIM)), KV_LORA_RANK),
        'q_norm': gain_vec(ks[12], (DEPTH, QK_HEAD_DIM)),
        'k_norm': gain_vec(ks[13], (DEPTH, QK_HEAD_DIM)),
        'out_norm_attn': gain_vec(ks[14], (DEPTH, ATTN_WIDTH)),
        'out_norm_fourier': gain_vec(ks[15], (DEPTH, FOURIER_WIDTH)),
        'w_out': dense(ks[16], (DEPTH, MIX_WIDTH, D_MODEL), MIX_WIDTH),
        'norm2': gain_vec(ks[17], (DEPTH, D_MODEL)),
        'w_ffn_gate': dense(ks[18], (N_DENSE, D_MODEL, FFN_DIM), D_MODEL),
        'w_ffn_up': dense(ks[19], (N_DENSE, D_MODEL, FFN_DIM), D_MODEL),
        'w_ffn_down': dense(ks[20], (N_DENSE, FFN_DIM, D_MODEL), FFN_DIM),
        'w_router': dense(ks[21], (N_MOE, D_MODEL, N_EXPERTS), D_MODEL),
        'w_moe_gate': dense(ks[22], (N_MOE, N_EXPERTS, D_MODEL, MOE_FFN_DIM), D_MODEL),
        'w_moe_up': dense(ks[23], (N_MOE, N_EXPERTS, D_MODEL, MOE_FFN_DIM), D_MODEL),
        'w_moe_down': dense(ks[24], (N_MOE, N_EXPERTS, MOE_FFN_DIM, D_MODEL), MOE_FFN_DIM),
    }


def reference(x, c, ctx, c_ctx, w_ada, b_ada, norm1, w_in, q_lora_norm, kv_lora_norm, w_uq, w_ukv,
              q_norm, k_norm, out_norm_attn, out_norm_fourier, w_out, norm2, w_ffn_gate, w_ffn_up,
              w_ffn_down, w_router, w_moe_gate, w_moe_up, w_moe_down):
    angles = axial_angles(x.shape[1])
    silu_c = jax.nn.silu(c)
    silu_cc = jax.nn.silu(c_ctx)
    for layer in range(DEPTH):
        update_ctx = layer < DEPTH - 1
        mod = jnp.matmul(silu_c, w_ada[layer]) + b_ada[layer]
        mod_c = jnp.matmul(silu_cc, w_ada[layer]) + b_ada[layer]
        shift1, scale1, gate1, shift2, scale2, gate2 = jnp.split(mod[:, None, :], 6, axis=-1)
        cshift1, cscale1, cgate1, cshift2, cscale2, cgate2 = jnp.split(mod_c, 6, axis=-1)

        h = modulate(rms_norm(x, norm1[layer]), shift1, scale1)
        hc = modulate(rms_norm(ctx, norm1[layer]), cshift1, cscale1)
        p = jnp.matmul(h, w_in[layer])
        pc = jnp.matmul(hc, w_in[layer])
        q = mla_queries(p, q_lora_norm[layer], w_uq[layer], q_norm[layer], angles)
        k, v = mla_keys_values(p, kv_lora_norm[layer], w_ukv[layer], k_norm[layer], angles)
        kc, vc = mla_keys_values(pc, kv_lora_norm[layer], w_ukv[layer], k_norm[layer], None)
        attn = latent_attention(q, jnp.concatenate([k, kc], axis=1), jnp.concatenate([v, vc], axis=1))
        four = fourier_mix(p[..., FOURIER_OFFSET:])
        x = x + gate1 * merge_heads(attn, four, out_norm_attn[layer], out_norm_fourier[layer], w_out[layer])
        if update_ctx:
            qc = mla_queries(pc, q_lora_norm[layer], w_uq[layer], q_norm[layer], None)
            attn_c = attend(qc, kc, vc).reshape(pc.shape[0], pc.shape[1], ATTN_WIDTH)
            four_c = fourier_mix(pc[..., FOURIER_OFFSET:])
            ctx = ctx + cgate1 * merge_heads(attn_c, four_c, out_norm_attn[layer], out_norm_fourier[layer],
                                             w_out[layer])

        if layer % 2 == 0:
            ffn = functools.partial(swiglu, wg=w_ffn_gate[layer // 2], wu=w_ffn_up[layer // 2],
                                    wd=w_ffn_down[layer // 2])
        else:
            ffn = functools.partial(moe_swiglu, w_router=w_router[layer // 2], w_gate=w_moe_gate[layer // 2],
                                    w_up=w_moe_up[layer // 2], w_down=w_moe_down[layer // 2])
        x = x + gate2 * ffn(modulate(rms_norm(x, norm2[layer]), shift2, scale2))
        if update_ctx:
            ctx = ctx + cgate2 * ffn(modulate(rms_norm(ctx, norm2[layer]), cshift2, cscale2))
    return x
```

```python
import functools
import math

import jax
import jax.numpy as jnp
from jax import lax
from jax.experimental import pallas as pl
from jax.experimental.pallas import tpu as pltpu

F32 = jnp.float32
BF16 = jnp.bfloat16

N_HEADS = 8
QK_NOPE_DIM = 64
QK_ROPE_DIM = 32
QK_HEAD_DIM = QK_NOPE_DIM + QK_ROPE_DIM
V_HEAD_DIM = 64
GRID_W = 64
ROPE_THETA = 10000.0
FOURIER_GROUPS = 4
FOURIER_GROUP_DIM = 128
N_EXPERTS = 8
TOP_K = 2
EPS = 1e-6

LANES = 128
MXU_DIM = 256
VMEM_LIMIT_BYTES = 60 * 1024 * 1024

ROW_TILE = 256
Q_TILE = 512
KV_CHUNKS = (768, 640, 512, 256)
FFT_N2 = 128
FFT_K1_GROUP = 8
EXPERT_ROWS = 512
FFN_CHUNK = 1408
MOE_CHUNK = 512


def _params(**kw):
    return pltpu.CompilerParams(vmem_limit_bytes=VMEM_LIMIT_BYTES, **kw)


def _dot(a, b):
    return jnp.dot(a, b, preferred_element_type=F32)


def _split(a):
    hi = a.astype(BF16)
    lo = (a - hi.astype(F32)).astype(BF16)
    return hi, lo


def _rms(x):
    return x * lax.rsqrt(jnp.mean(x * x, axis=-1, keepdims=True) + EPS)


def _silu(x):
    return x / (1.0 + jnp.exp(-x))


def _mod_kernel(c_ref, w_ref, b_ref, o_ref):
    chi, clo = _split(_silu(c_ref[...]))
    whi, wlo = _split(w_ref[0])
    o_ref[0] = _dot(chi, whi) + _dot(clo, whi) + _dot(chi, wlo) + b_ref[0]


def _modulation(cond, w_ada, b_ada):
    depth, d, n = w_ada.shape
    tn = n // 4
    return pl.pallas_call(
        _mod_kernel,
        out_shape=jax.ShapeDtypeStruct((depth, cond.shape[0], n), F32),
        grid=(depth, n // tn),
        in_specs=[
            pl.BlockSpec(cond.shape, lambda l, j: (0, 0)),
            pl.BlockSpec((1, d, tn), lambda l, j: (l, 0, j)),
            pl.BlockSpec((1, 1, tn), lambda l, j: (l, 0, j)),
        ],
        out_specs=pl.BlockSpec((1, cond.shape[0], tn), lambda l, j: (l, 0, j)),
        compiler_params=_params(),
        name="adaln_modulation",
    )(cond, w_ada, b_ada.reshape(depth, 1, n))


def _inproj_kernel(x_ref, mod_ref, g1_ref, win_ref, gq_ref, gkv_ref, wq_ref, wkv_ref, bd_ref,
                   qg_ref, qgs_ref, kg_ref, kgs_ref, cos_ref, sin_ref,
                   q_ref, k_ref, v_ref, f_ref):
    d = x_ref.shape[-1]
    m = mod_ref[0]
    shift, scale = m[:, 0:d], m[:, d:2 * d]
    h = _rms(x_ref[0]) * (g1_ref[...] * (1.0 + scale)) + shift
    p = _dot(h.astype(BF16), win_ref[...])
    nq = gq_ref.shape[-1]
    nkv = gkv_ref.shape[-1]
    nf = f_ref.shape[-1]
    o_f, o_pe = nq + nkv, nq + nkv + nf
    f_ref[0] = p[:, o_f:o_pe].astype(BF16)
    cq = (_rms(p[:, 0:nq]) * gq_ref[...]).astype(BF16)
    ckv = (_rms(p[:, nq:o_f]) * gkv_ref[...]).astype(BF16)
    qq = _dot(cq, wq_ref[...])
    kv = _dot(ckv, wkv_ref[...])
    hw = N_HEADS * LANES
    v_ref[0] = kv[:, hw:].astype(BF16)
    kpe = p[:, o_pe:o_pe + LANES]
    kpe_sw = p[:, o_pe + LANES:o_pe + 2 * LANES]
    kpe2 = jnp.concatenate([kpe, kpe], axis=1)
    kpe_sw2 = jnp.concatenate([kpe_sw, kpe_sw], axis=1)

    cos, sin = cos_ref[...], sin_ref[...]

    def pair(t):
        return jnp.concatenate([t, t], axis=1)

    q_c, q_s = pair(qg_ref[...] * cos), pair(qgs_ref[...] * sin)
    k_c, k_s = pair(kg_ref[...] * cos), pair(kgs_ref[...] * sin)
    bd = bd_ref[...]

    def head_rsqrt(raw):
        hi, lo = _split(raw * raw)
        ss = _dot(hi, bd) + _dot(lo, bd)
        return lax.rsqrt(ss * (1.0 / QK_HEAD_DIM) + EPS)

    q_scale = QK_HEAD_DIM ** -0.5
    for hp in range(N_HEADS // 2):
        lo_, hi_ = hp * 2 * LANES, (hp + 1) * 2 * LANES
        q_raw, q_sw = qq[:, lo_:hi_], qq[:, hw + lo_:hw + hi_]
        qo = (head_rsqrt(q_raw) * q_scale) * (q_raw * q_c + q_sw * q_s)
        q_ref[0, 2 * hp] = qo[:, :LANES].astype(BF16)
        q_ref[0, 2 * hp + 1] = qo[:, LANES:].astype(BF16)
        k_raw = kv[:, lo_:hi_] + kpe2
        ko = head_rsqrt(k_raw) * (k_raw * k_c + kpe_sw2 * k_s)
        k_ref[0, 2 * hp] = ko[:, :LANES].astype(BF16)
        k_ref[0, 2 * hp + 1] = ko[:, LANES:].astype(BF16)


def _mod_spec(n_lat_tiles, n_mod):
    def index(b, i):
        return (jnp.where(i < n_lat_tiles, b, pl.num_programs(0)), 0, 0)
    return pl.BlockSpec((1, 1, n_mod), index)


def _const_spec(shape):
    zeros = (0,) * len(shape)
    return pl.BlockSpec(shape, lambda b, i: zeros)


def _input_projection(xt, mod_l, lw, tabs, n_lat):
    bsz, stot, d = xt.shape
    tm = ROW_TILE
    hw = N_HEADS * LANES
    nf = FOURIER_GROUPS * FOURIER_GROUP_DIM
    tok = lambda n: pl.BlockSpec((1, tm, n), lambda b, i: (b, i, 0))
    rope = pl.BlockSpec((tm, LANES), lambda b, i: (i, 0))
    head_out = pl.BlockSpec((1, N_HEADS, tm, LANES), lambda b, i: (b, 0, i, 0))
    return pl.pallas_call(
        _inproj_kernel,
        out_shape=(
            jax.ShapeDtypeStruct((bsz, N_HEADS, stot, LANES), BF16),
            jax.ShapeDtypeStruct((bsz, N_HEADS, stot, LANES), BF16),
            jax.ShapeDtypeStruct((bsz, stot, N_HEADS * V_HEAD_DIM), BF16),
            jax.ShapeDtypeStruct((bsz, stot, nf), BF16),
        ),
        grid=(bsz, stot // tm),
        in_specs=[
            tok(d), _mod_spec(n_lat // tm, mod_l.shape[-1]), _const_spec((1, d)),
            _const_spec(lw["w_in"].shape), _const_spec(lw["gq"].shape), _const_spec(lw["gkv"].shape),
            _const_spec(lw["w_q"].shape), _const_spec(lw["w_kv"].shape), _const_spec(tabs["bd"].shape),
            _const_spec((1, LANES)), _const_spec((1, LANES)), _const_spec((1, LANES)),
            _const_spec((1, LANES)), rope, rope,
        ],
        out_specs=(head_out, head_out, tok(N_HEADS * V_HEAD_DIM), tok(nf)),
        compiler_params=_params(),
        name="input_projection",
    )(xt, mod_l, lw["g1"], lw["w_in"], lw["gq"], lw["gkv"], lw["w_q"], lw["w_kv"], tabs["bd"],
      lw["qg"], lw["qg_sw"], lw["kg"], lw["kg_sw"], tabs["cos"], tabs["sin"])


def _attn_kernel(q_ref, k_ref, v_ref, o_ref, *, n_chunks, tk):
    tq = q_ref.shape[2]
    outs = []
    for hh in range(2):
        q = q_ref[0, hh]

        def body(j, carry):
            m, l, acc = carry
            start = pl.multiple_of(j * tk, tk)
            k = k_ref[0, hh, pl.ds(start, tk), :]
            v = v_ref[0, pl.ds(start, tk), :]
            s = lax.dot_general(q, k, (((1,), (1,)), ((), ())), preferred_element_type=F32)
            m_new = jnp.maximum(m, jnp.max(s, axis=-1, keepdims=True))
            alpha = jnp.exp(m - m_new)
            p = jnp.exp(s - m_new)
            l = alpha * l + jnp.sum(p, axis=-1, keepdims=True)
            acc = alpha * acc + _dot(p.astype(BF16), v)
            return m_new, l, acc

        init = (jnp.full((tq, 1), -jnp.inf, F32), jnp.zeros((tq, 1), F32), jnp.zeros((tq, LANES), F32))
        _, l, acc = lax.fori_loop(0, n_chunks, body, init)
        outs.append(acc / l)
    lane = lax.broadcasted_iota(jnp.int32, (tq, LANES), 1)
    o_ref[0] = jnp.where(lane < V_HEAD_DIM, outs[0], outs[1]).astype(BF16)


def _attention(q, k, v, n_lat, with_ctx):
    bsz, _, stot, _ = q.shape
    n_ctx = stot - n_lat
    hp = N_HEADS // 2
    tk = next(c for c in KV_CHUNKS if stot % c == 0)
    tq = Q_TILE
    nv = N_HEADS * V_HEAD_DIM
    attn_lat = pl.pallas_call(
        functools.partial(_attn_kernel, n_chunks=stot // tk, tk=tk),
        out_shape=jax.ShapeDtypeStruct((bsz, n_lat, nv), BF16),
        grid=(bsz, hp, n_lat // tq),
        in_specs=[
            pl.BlockSpec((1, 2, tq, LANES), lambda b, h, i: (b, h, i, 0)),
            pl.BlockSpec((1, 2, stot, LANES), lambda b, h, i: (b, h, 0, 0)),
            pl.BlockSpec((1, stot, LANES), lambda b, h, i: (b, 0, h)),
        ],
        out_specs=pl.BlockSpec((1, tq, LANES), lambda b, h, i: (b, i, h)),
        compiler_params=_params(),
        name="latent_attention",
    )(q, k, v)
    if not with_ctx:
        return attn_lat, None
    cblk = n_lat // n_ctx
    attn_ctx = pl.pallas_call(
        functools.partial(_attn_kernel, n_chunks=1, tk=n_ctx),
        out_shape=jax.ShapeDtypeStruct((bsz, n_ctx, nv), BF16),
        grid=(bsz, hp),
        in_specs=[
            pl.BlockSpec((1, 2, n_ctx, LANES), lambda b, h: (b, h, cblk, 0)),
            pl.BlockSpec((1, 2, n_ctx, LANES), lambda b, h: (b, h, cblk, 0)),
            pl.BlockSpec((1, n_ctx, LANES), lambda b, h: (b, cblk, h)),
        ],
        out_specs=pl.BlockSpec((1, n_ctx, LANES), lambda b, h: (b, 0, h)),
        compiler_params=_params(),
        name="context_attention",
    )(q, k, v)
    return attn_lat, attn_ctx


def _fft1_kernel(cs_ref, x_ref, a_ref):
    a_ref[0] = _dot(cs_ref[...], x_ref[0]).astype(BF16)


def _fft2_kernel(ar_ref, ai_ref, m2_ref, wch_ref, o_ref):
    nf = wch_ref.shape[-1]
    n2 = ar_ref.shape[2]
    for j in range(ar_ref.shape[1]):
        slab = jnp.concatenate([ar_ref[0, j], ai_ref[0, j]], axis=0)
        z = _dot(m2_ref[j], slab)
        zc = jnp.concatenate([z[:n2], z[n2:]], axis=1).astype(BF16)
        o_ref[0, :, j * nf:(j + 1) * nf] = _dot(zc, wch_ref[...]).astype(BF16)


def _fft_ctx_kernel(cs_ref, x_ref, wch_ref, o_ref):
    n = x_ref.shape[1]
    z = _dot(cs_ref[...], x_ref[0])
    zc = jnp.concatenate([z[:n], z[n:]], axis=1).astype(BF16)
    o_ref[0] = _dot(zc, wch_ref[...]).astype(BF16)


def _fourier_mix(f, tabs, n_lat, with_ctx):
    bsz, stot, nf = f.shape
    n_ctx = stot - n_lat
    n2 = FFT_N2
    n1 = n_lat // n2
    kg = min(FFT_K1_GROUP, n1)
    tc = min(n2 * nf, 8192)
    a = pl.pallas_call(
        _fft1_kernel,
        out_shape=jax.ShapeDtypeStruct((bsz, 2 * n1, n2 * nf), BF16),
        grid=(bsz, n2 * nf // tc),
        in_specs=[
            pl.BlockSpec((2 * n1, n1), lambda b, j: (0, 0)),
            pl.BlockSpec((1, n1, tc), lambda b, j: (b, 0, j)),
        ],
        out_specs=pl.BlockSpec((1, 2 * n1, tc), lambda b, j: (b, 0, j)),
        compiler_params=_params(),
        name="fourier_stage1",
    )(tabs["cs1"], f.reshape(bsz, stot // n2, n2 * nf))
    a4 = a.reshape(bsz, 2 * n1, n2, nf)
    four_lat = pl.pallas_call(
        _fft2_kernel,
        out_shape=jax.ShapeDtypeStruct((bsz, n2, n1 * nf), BF16),
        grid=(bsz, n1 // kg),
        in_specs=[
            pl.BlockSpec((1, kg, n2, nf), lambda b, g: (b, g, 0, 0)),
            pl.BlockSpec((1, kg, n2, nf), lambda b, g: (b, n1 // kg + g, 0, 0)),
            pl.BlockSpec((kg, 2 * n2, 2 * n2), lambda b, g: (g, 0, 0)),
            pl.BlockSpec((2 * nf, nf), lambda b, g: (0, 0)),
        ],
        out_specs=pl.BlockSpec((1, n2, kg * nf), lambda b, g: (b, 0, g)),
        compiler_params=_params(),
        name="fourier_stage2",
    )(a4, a4, tabs["m2"], tabs["wch_lat"]).reshape(bsz, n_lat, nf)
    if not with_ctx:
        return four_lat, None
    cblk = n_lat // n_ctx
    four_ctx = pl.pallas_call(
        _fft_ctx_kernel,
        out_shape=jax.ShapeDtypeStruct((bsz, n_ctx, nf), BF16),
        grid=(bsz,),
        in_specs=[
            pl.BlockSpec((2 * n_ctx, n_ctx), lambda b: (0, 0)),
            pl.BlockSpec((1, n_ctx, nf), lambda b: (b, cblk, 0)),
            pl.BlockSpec((2 * nf, nf), lambda b: (0, 0)),
        ],
        out_specs=pl.BlockSpec((1, n_ctx, nf), lambda b: (b, 0, 0)),
        compiler_params=_params(),
        name="fourier_context",
    )(tabs["cs_ctx"], f, tabs["wch_ctx"])
    return four_lat, four_ctx


def _merge_kernel(x_ref, *rest, moe, n_lat_tiles, with_ctx):
    if with_ctx:
        al_ref, fl_ref, ac_ref, fc_ref = rest[:4]
        rest = rest[4:]
        is_ctx = pl.program_id(1) >= n_lat_tiles
        a = jnp.where(is_ctx, ac_ref[0], al_ref[0])
        f = jnp.where(is_ctx, fc_ref[0], fl_ref[0])
    else:
        a, f = rest[0][0], rest[1][0]
        rest = rest[2:]
    mod_ref, ga_ref, gf_ref, wout_ref, g2_ref = rest[:5]
    rest = rest[5:]
    d = x_ref.shape[-1]
    m = mod_ref[0]
    gate1, shift2, scale2 = m[:, 2 * d:3 * d], m[:, 3 * d:4 * d], m[:, 4 * d:5 * d]
    an = _rms(a.astype(F32)) * ga_ref[...]
    fn = _rms(f.astype(F32)) * gf_ref[...]
    y = _dot(jnp.concatenate([an, fn], axis=1).astype(BF16), wout_ref[...])
    xn = x_ref[0] + gate1 * y
    h2 = _rms(xn) * (g2_ref[...] * (1.0 + scale2)) + shift2
    if not moe:
        xo_ref, h_ref = rest
        xo_ref[0] = xn
        h_ref[0] = h2.astype(BF16)
        return
    wr_hi_ref, wr_lo_ref, xo_ref, h_ref, rt_ref = rest
    xo_ref[0] = xn
    h_ref[0] = h2
    hi, lo = _split(h2)
    logits = _dot(hi, wr_hi_ref[...]) + _dot(lo, wr_hi_ref[...]) + _dot(hi, wr_lo_ref[...])
    lane = lax.broadcasted_iota(jnp.int32, logits.shape, 1).astype(F32)
    lg = jnp.where(lane < N_EXPERTS, logits, -jnp.inf)
    m1 = jnp.max(lg, axis=-1, keepdims=True)
    i1 = jnp.min(jnp.where(lg == m1, lane, float(LANES)), axis=-1, keepdims=True)
    lg2 = jnp.where(lane == i1, -jnp.inf, lg)
    m2 = jnp.max(lg2, axis=-1, keepdims=True)
    i2 = jnp.min(jnp.where(lg2 == m2, lane, float(LANES)), axis=-1, keepdims=True)
    e = jnp.exp(m2 - m1)
    w1 = 1.0 / (1.0 + e)
    w2 = e * w1
    rt_ref[0] = jnp.where(lane == 0, i1, jnp.where(lane == 1, i2, jnp.where(lane == 2, w1,
                                                                          jnp.where(lane == 3, w2, 0.0))))


def _merge(xt, attn, four, mod_l, lw, n_lat, moe):
    bsz, _, d = xt.shape
    tm = ROW_TILE
    with_ctx = attn[1] is not None
    n_lat_tiles = n_lat // tm
    n_out = n_lat + (attn[1].shape[1] if with_ctx else 0)
    assert not with_ctx or attn[1].shape[1] == tm
    tok = lambda n: pl.BlockSpec((1, tm, n), lambda b, i: (b, i, 0))
    lat = lambda n: pl.BlockSpec((1, tm, n), lambda b, i: (b, jnp.minimum(i, n_lat_tiles - 1), 0))
    ctx = lambda n: pl.BlockSpec((1, tm, n), lambda b, i: (b, 0, 0))
    na, nf = attn[0].shape[-1], four[0].shape[-1]
    in_specs = [tok(d), lat(na), lat(nf)] + ([ctx(na), ctx(nf)] if with_ctx else [])
    args = [xt, attn[0], four[0]] + ([attn[1], four[1]] if with_ctx else [])
    in_specs += [_mod_spec(n_lat_tiles, mod_l.shape[-1]),
                 _const_spec((1, na)), _const_spec((1, nf)), _const_spec((na + nf, d)), _const_spec((1, d))]
    args += [mod_l, lw["ga"], lw["gf"], lw["w_out"], lw["g2"]]
    out_shape = [jax.ShapeDtypeStruct((bsz, n_out, d), F32),
                 jax.ShapeDtypeStruct((bsz, n_out, d), F32 if moe else BF16)]
    out_specs = [tok(d), tok(d)]
    if moe:
        in_specs += [_const_spec((d, LANES)), _const_spec((d, LANES))]
        args += [lw["wr_hi"], lw["wr_lo"]]
        out_shape.append(jax.ShapeDtypeStruct((bsz, n_out, LANES), F32))
        out_specs.append(tok(LANES))
    return pl.pallas_call(
        functools.partial(_merge_kernel, moe=moe, n_lat_tiles=n_lat_tiles, with_ctx=with_ctx),
        out_shape=tuple(out_shape),
        grid=(bsz, n_out // tm),
        in_specs=in_specs,
        out_specs=tuple(out_specs),
        compiler_params=_params(),
        name="merge_router" if moe else "merge",
    )(*args)


def _swiglu_chunks(h, wg_ref, wu_ref, wd_ref, chunk, lead=()):
    f = wg_ref.shape[-1]
    acc = None
    for c in range(f // chunk):
        sl = slice(c * chunk, (c + 1) * chunk)
        g = _dot(h, wg_ref[lead + (slice(None), sl)])
        u = _dot(h, wu_ref[lead + (slice(None), sl)])
        part = _dot((_silu(g) * u).astype(BF16), wd_ref[lead + (sl, slice(None))])
        acc = part if acc is None else acc + part
    return acc


def _ffn_kernel(x_ref, h_ref, mod_ref, wg_ref, wu_ref, wd_ref, o_ref):
    d = x_ref.shape[-1]
    gate2 = mod_ref[0][:, 5 * d:6 * d]
    y = _swiglu_chunks(h_ref[0], wg_ref, wu_ref, wd_ref, FFN_CHUNK)
    o_ref[0] = x_ref[0] + gate2 * y


def _dense_ffn(xt, h2, mod_l, lw, n_lat, n_out):
    bsz, _, d = xt.shape
    tm = ROW_TILE
    tok = lambda: pl.BlockSpec((1, tm, d), lambda b, i: (b, i, 0))
    return pl.pallas_call(
        _ffn_kernel,
        out_shape=jax.ShapeDtypeStruct((bsz, n_out, d), F32),
        grid=(bsz, n_out // tm),
        in_specs=[tok(), tok(), _mod_spec(n_lat // tm, mod_l.shape[-1]),
                  _const_spec(lw["w_gate"].shape), _const_spec(lw["w_up"].shape),
                  _const_spec(lw["w_down"].shape)],
        out_specs=tok(),
        compiler_params=_params(),
        name="dense_swiglu",
    )(xt, h2, mod_l, lw["w_gate"], lw["w_up"], lw["w_down"])


def _gather_rows_kernel(idx_ref, src_ref, o_ref, sems):
    rows = o_ref.shape[0]

    def copy(r):
        return pltpu.make_async_copy(src_ref.at[pl.ds(idx_ref[0, 0, r], 1), :],
                                     o_ref.at[pl.ds(r, 1), :], sems.at[r])

    def start(r, c):
        copy(r).start()
        return c

    def wait(r, c):
        copy(r).wait()
        return c

    lax.fori_loop(0, rows, start, 0)
    lax.fori_loop(0, rows, wait, 0)


def _gather_rows(src, idx, rows_per_block):
    n_blocks = idx.shape[0] // rows_per_block
    d = src.shape[-1]
    return pl.pallas_call(
        _gather_rows_kernel,
        out_shape=jax.ShapeDtypeStruct((idx.shape[0], d), src.dtype),
        grid=(n_blocks,),
        in_specs=[
            pl.BlockSpec((1, 1, rows_per_block), lambda n: (n, 0, 0), memory_space=pltpu.SMEM),
            pl.BlockSpec(memory_space=pl.ANY),
        ],
        out_specs=pl.BlockSpec((rows_per_block, d), lambda n: (n, 0)),
        scratch_shapes=[pltpu.SemaphoreType.DMA((rows_per_block,))],
        compiler_params=_params(),
        name="moe_gather_rows",
    )(idx.reshape(n_blocks, 1, rows_per_block), src)


def _expert_kernel(be_ref, nu_ref, x_ref, wg_ref, wu_ref, wd_ref, o_ref):
    used = pl.program_id(0) < nu_ref[0]

    @pl.when(used)
    def _():
        o_ref[...] = _swiglu_chunks(x_ref[...].astype(BF16), wg_ref, wu_ref, wd_ref, MOE_CHUNK, lead=(0,))

    @pl.when(jnp.logical_not(used))
    def _():
        o_ref[...] = jnp.zeros(o_ref.shape, o_ref.dtype)


def _expert_blocks(xs, block_exp, n_used, lw):
    n_rows, d = xs.shape
    tm = EXPERT_ROWS
    f = lw["w_moe_gate"].shape[-1]
    wspec = lambda shape: pl.BlockSpec((1,) + shape, lambda n, be, nu: (be[n], 0, 0),
                                       pipeline_mode=pl.Buffered(1))
    return pl.pallas_call(
        _expert_kernel,
        out_shape=jax.ShapeDtypeStruct((n_rows, d), F32),
        grid_spec=pltpu.PrefetchScalarGridSpec(
            num_scalar_prefetch=2,
            grid=(n_rows // tm,),
            in_specs=[pl.BlockSpec((tm, d), lambda n, be, nu: (n, 0)),
                      wspec((d, f)), wspec((d, f)), wspec((f, d))],
            out_specs=pl.BlockSpec((tm, d), lambda n, be, nu: (n, 0)),
        ),
        compiler_params=_params(),
        name="moe_expert_blocks",
    )(block_exp, n_used, xs, lw["w_moe_gate"], lw["w_moe_up"], lw["w_moe_down"])


def _combine_kernel(pos_ref, ys_ref, x_ref, rt_ref, mod_ref, o_ref, buf, sems):
    tm, d = buf.shape[1], buf.shape[2]

    def copy(r, j):
        return pltpu.make_async_copy(ys_ref.at[pl.ds(pos_ref[0, 0, TOP_K * r + j], 1), :],
                                     buf.at[j, pl.ds(r, 1), :], sems.at[j, r])

    def start(r, c):
        copy(r, 0).start()
        copy(r, 1).start()
        return c

    def wait(r, c):
        copy(r, 0).wait()
        copy(r, 1).wait()
        return c

    lax.fori_loop(0, tm, start, 0)
    lax.fori_loop(0, tm, wait, 0)
    gate2 = mod_ref[0][:, 5 * d:6 * d]
    rt = rt_ref[0]
    y = rt[:, 2:3] * buf[0] + rt[:, 3:4] * buf[1]
    o_ref[0] = x_ref[0] + gate2 * y


def _combine(ys, pos, xt, rt, mod_l, n_lat, n_out):
    bsz, stot, d = xt.shape
    tm = ROW_TILE
    tiles = stot // tm
    tok = lambda n: pl.BlockSpec((1, tm, n), lambda b, i: (b, i, 0))
    return pl.pallas_call(
        _combine_kernel,
        out_shape=jax.ShapeDtypeStruct((bsz, n_out, d), F32),
        grid=(bsz, n_out // tm),
        in_specs=[
            pl.BlockSpec((1, 1, TOP_K * tm), lambda b, i: (b * tiles + i, 0, 0), memory_space=pltpu.SMEM),
            pl.BlockSpec(memory_space=pl.ANY),
            tok(d), tok(LANES), _mod_spec(n_lat // tm, mod_l.shape[-1]),
        ],
        out_specs=tok(d),
        scratch_shapes=[pltpu.VMEM((TOP_K, tm, d), F32), pltpu.SemaphoreType.DMA((TOP_K, tm))],
        compiler_params=_params(),
        name="moe_combine",
    )(pos.reshape(bsz * tiles, 1, TOP_K * tm), ys, xt, rt, mod_l)


def _moe_ffn(xt, h2, rt, mod_l, lw, n_lat, n_out):
    bsz, stot, d = xt.shape
    n_tok = bsz * stot
    n_assign = n_tok * TOP_K
    tm = EXPERT_ROWS
    exp_flat = rt[..., :TOP_K].astype(jnp.int32).reshape(n_assign)
    onehot = (exp_flat[:, None] == jnp.arange(N_EXPERTS, dtype=jnp.int32)[None, :]).astype(jnp.int32)
    csum = jnp.cumsum(onehot, axis=0)
    rank = jnp.sum(jnp.where(onehot > 0, csum, 0), axis=1) - 1
    counts = csum[-1]
    padded = (counts + tm - 1) // tm * tm
    pad_ends = jnp.cumsum(padded)
    pad_starts = pad_ends - padded
    dest = (jnp.sum(onehot * pad_starts[None, :], axis=1) + rank).astype(jnp.int32)
    n_blocks = (n_assign + N_EXPERTS * (tm - 1) + tm - 1) // tm
    n_rows = n_blocks * tm
    tok_flat = jnp.arange(n_assign, dtype=jnp.int32) // TOP_K
    row_tok = jnp.zeros((n_rows,), jnp.int32).at[dest].set(tok_flat)
    block_exp = jnp.minimum(
        jnp.searchsorted(pad_ends, jnp.arange(n_blocks, dtype=jnp.int32) * tm, side="right"),
        N_EXPERTS - 1).astype(jnp.int32)
    n_used = (pad_ends[-1:] // tm).astype(jnp.int32)

    xs = _gather_rows(h2.reshape(n_tok, d), row_tok, tm)
    ys = _expert_blocks(xs, block_exp, n_used, lw)
    return _combine(ys, dest, xt, rt, mod_l, n_lat, n_out)


def _pair_swap_index():
    j = jnp.arange(QK_ROPE_DIM)
    return jnp.where((j % 16) < 8, j + 8, j - 8)


def _head_lanes(v):
    pad = [(0, 0)] * (v.ndim - 1) + [(0, LANES - QK_HEAD_DIM)]
    return jnp.pad(v, pad)


def _swap_rope(v):
    rope = v[..., QK_NOPE_DIM:][..., _pair_swap_index()]
    return _head_lanes(jnp.concatenate([jnp.zeros_like(v[..., :QK_NOPE_DIM]), rope], axis=-1))


def _dft_angles(n_rows, n_cols, period):
    idx = (jnp.arange(n_rows, dtype=jnp.int32)[:, None] * jnp.arange(n_cols, dtype=jnp.int32)[None, :]) % period
    return idx.astype(F32) * (2.0 * math.pi / period)


def _tables(n_lat, n_ctx):
    nf = FOURIER_GROUPS * FOURIER_GROUP_DIM
    rows = n_lat // GRID_W
    r = jnp.repeat(jnp.arange(rows, dtype=F32), GRID_W)
    col = jnp.tile(jnp.arange(GRID_W, dtype=F32), rows)
    half = QK_ROPE_DIM // 2
    inv_freq = ROPE_THETA ** (-jnp.arange(0, half, 2, dtype=F32) / half)
    ar, ac = r[:, None] * inv_freq, col[:, None] * inv_freq
    ones = jnp.ones((n_lat, QK_NOPE_DIM), F32)
    cos = jnp.concatenate([ones, jnp.cos(ar), jnp.cos(ar), jnp.cos(ac), jnp.cos(ac)], axis=1)
    sin = jnp.concatenate([0 * ones, -jnp.sin(ar), jnp.sin(ar), -jnp.sin(ac), jnp.sin(ac)], axis=1)
    cos = jnp.concatenate([cos, jnp.ones((n_ctx, QK_HEAD_DIM), F32)], axis=0)
    sin = jnp.concatenate([sin, jnp.zeros((n_ctx, QK_HEAD_DIM), F32)], axis=0)
    lane = jnp.arange(2 * LANES)
    bd = (lane[:, None] // LANES == lane[None, :] // LANES).astype(BF16)
    n2 = FFT_N2
    n1 = n_lat // n2
    a1 = _dft_angles(n1, n1, n1)
    cs1 = jnp.concatenate([jnp.cos(a1), -jnp.sin(a1)], axis=0).astype(BF16)
    k = (jnp.arange(n1, dtype=jnp.int32)[:, None, None] + n1 * jnp.arange(n2, dtype=jnp.int32)[None, :, None])
    ang = ((k * jnp.arange(n2, dtype=jnp.int32)[None, None, :]) % n_lat).astype(F32) * (2.0 * math.pi / n_lat)
    c2, s2 = jnp.cos(ang), jnp.sin(ang)
    m2 = jnp.concatenate([jnp.concatenate([c2, s2], axis=2), jnp.concatenate([-s2, c2], axis=2)], axis=1)
    ach = _dft_angles(FOURIER_GROUP_DIM, FOURIER_GROUP_DIM, FOURIER_GROUP_DIM)
    eye = jnp.eye(FOURIER_GROUPS, dtype=F32)
    wch = jnp.concatenate([jnp.kron(eye, jnp.cos(ach)), jnp.kron(eye, jnp.sin(ach))], axis=0)
    actx = _dft_angles(n_ctx, n_ctx, n_ctx)
    return {
        "cos": _head_lanes(cos), "sin": _head_lanes(sin), "bd": bd,
        "cs1": cs1, "m2": m2.astype(BF16),
        "wch_lat": (wch * (n_lat * FOURIER_GROUP_DIM) ** -0.5).astype(BF16),
        "wch_ctx": (wch * (n_ctx * FOURIER_GROUP_DIM) ** -0.5).astype(BF16),
        "cs_ctx": jnp.concatenate([jnp.cos(actx), -jnp.sin(actx)], axis=0).astype(BF16),
    }


def _layer_weights(layer, p):
    q_rank = p["q_lora_norm"].shape[-1]
    kv_rank = p["kv_lora_norm"].shape[-1]
    o_pe = q_rank + kv_rank
    o_f = o_pe + QK_ROPE_DIM
    w_in = p["w_in"][layer]
    d = w_in.shape[0]
    w_pe = w_in[:, o_pe:o_f]
    z64 = jnp.zeros((d, QK_NOPE_DIM), F32)
    z32 = jnp.zeros((d, LANES - QK_HEAD_DIM), F32)
    w_in_cat = jnp.concatenate(
        [w_in[:, :o_pe], w_in[:, o_f:], z64, w_pe, z32, z64, w_pe[:, _pair_swap_index()], z32], axis=1)
    w_uq = p["w_uq"][layer].reshape(q_rank, N_HEADS, QK_HEAD_DIM)
    w_q = jnp.concatenate([_head_lanes(w_uq).reshape(q_rank, -1), _swap_rope(w_uq).reshape(q_rank, -1)], axis=1)
    w_ukv = p["w_ukv"][layer].reshape(kv_rank, N_HEADS, QK_NOPE_DIM + V_HEAD_DIM)
    w_k = jnp.pad(w_ukv[..., :QK_NOPE_DIM], ((0, 0), (0, 0), (0, LANES - QK_NOPE_DIM))).reshape(kv_rank, -1)
    w_v = w_ukv[..., QK_NOPE_DIM:].reshape(kv_rank, -1)
    row = lambda v: v.reshape(1, -1)
    lw = {
        "g1": row(p["norm1"][layer]), "g2": row(p["norm2"][layer]),
        "w_in": w_in_cat.astype(BF16),
        "gq": row(p["q_lora_norm"][layer]), "gkv": row(p["kv_lora_norm"][layer]),
        "w_q": w_q.astype(BF16), "w_kv": jnp.concatenate([w_k, w_v], axis=1).astype(BF16),
        "qg": row(_head_lanes(p["q_norm"][layer])), "qg_sw": row(_swap_rope(p["q_norm"][layer])),
        "kg": row(_head_lanes(p["k_norm"][layer])), "kg_sw": row(_swap_rope(p["k_norm"][layer])),
        "ga": row(p["out_norm_attn"][layer]), "gf": row(p["out_norm_fourier"][layer]),
        "w_out": p["w_out"][layer].astype(BF16),
    }
    if layer % 2 == 0:
        lw.update(w_gate=p["w_ffn_gate"][layer // 2].astype(BF16), w_up=p["w_ffn_up"][layer // 2].astype(BF16),
                  w_down=p["w_ffn_down"][layer // 2].astype(BF16))
    else:
        wr = jnp.pad(p["w_router"][layer // 2], ((0, 0), (0, LANES - N_EXPERTS)))
        wr_hi, wr_lo = _split(wr)
        lw.update(wr_hi=wr_hi, wr_lo=wr_lo,
                  w_moe_gate=p["w_moe_gate"][layer // 2].astype(BF16),
                  w_moe_up=p["w_moe_up"][layer // 2].astype(BF16),
                  w_moe_down=p["w_moe_down"][layer // 2].astype(BF16))
    return lw


def kernel(x, c, ctx, c_ctx, w_ada, b_ada, norm1, w_in, q_lora_norm, kv_lora_norm, w_uq, w_ukv, q_norm, k_norm,
           out_norm_attn, out_norm_fourier, w_out, norm2, w_ffn_gate, w_ffn_up, w_ffn_down, w_router,
           w_moe_gate, w_moe_up, w_moe_down):
    params = dict(norm1=norm1, w_in=w_in, q_lora_norm=q_lora_norm, kv_lora_norm=kv_lora_norm, w_uq=w_uq,
                  w_ukv=w_ukv, q_norm=q_norm, k_norm=k_norm, out_norm_attn=out_norm_attn,
                  out_norm_fourier=out_norm_fourier, w_out=w_out, norm2=norm2, w_ffn_gate=w_ffn_gate,
                  w_ffn_up=w_ffn_up, w_ffn_down=w_ffn_down, w_router=w_router, w_moe_gate=w_moe_gate,
                  w_moe_up=w_moe_up, w_moe_down=w_moe_down)
    bsz, n_lat, d = x.shape
    n_ctx = ctx.shape[1]
    depth = w_ada.shape[0]
    assert n_lat % Q_TILE == 0 and n_lat % n_ctx == 0 and n_ctx % ROW_TILE == 0
    assert n_lat % FFT_N2 == 0 and n_lat % GRID_W == 0 and n_ctx % FFT_N2 == 0

    cond = jnp.concatenate([c, c_ctx[None, :], jnp.zeros((8 - bsz - 1, d), F32)], axis=0)
    mod = _modulation(cond, w_ada, b_ada)
    tabs = _tables(n_lat, n_ctx)
    xt = jnp.concatenate([x, ctx], axis=1)

    for layer in range(depth):
        last = layer == depth - 1
        lw = _layer_weights(layer, params)
        mod_l = mod[layer].reshape(mod.shape[1], 1, mod.shape[2])
        q, k, v, f = _input_projection(xt, mod_l, lw, tabs, n_lat)
        attn = _attention(q, k, v, n_lat, not last)
        four = _fourier_mix(f, tabs, n_lat, not last)
        moe = layer % 2 == 1
        n_out = n_lat if last else n_lat + n_ctx
        if moe:
            xt, h2, rt = _merge(xt, attn, four, mod_l, lw, n_lat, True)
            xt = _moe_ffn(xt, h2, rt, mod_l, lw, n_lat, n_out)
        else:
            xt, h2 = _merge(xt, attn, four, mod_l, lw, n_lat, False)
            xt = _dense_ffn(xt, h2, mod_l, lw, n_lat, n_out)
    return xt[:, :n_lat]
```

```python
import functools
import math

import jax
import jax.numpy as jnp
from jax import lax
from jax.experimental import pallas as pl
from jax.experimental.pallas import tpu as pltpu

F32 = jnp.float32
BF16 = jnp.bfloat16

N_HEADS = 8
QK_NOPE_DIM = 64
QK_ROPE_DIM = 32
QK_HEAD_DIM = QK_NOPE_DIM + QK_ROPE_DIM
V_HEAD_DIM = 64
GRID_W = 64
ROPE_THETA = 10000.0
FOURIER_GROUPS = 4
FOURIER_GROUP_DIM = 128
N_EXPERTS = 8
TOP_K = 2
EPS = 1e-6
LOG2E = 1.4426950408889634
MAX_DIRECT_BOUND = 50.0

LANES = 128
MXU_DIM = 256
VMEM_LIMIT_BYTES = 60 * 1024 * 1024

ROW_TILE = 256
Q_TILE = 512
KV_CHUNKS = (768, 640, 512, 256)
FFT_N2 = 128
FFT_K1_GROUP = 8
EXPERT_ROWS = 512
FFN_CHUNK = 1408
MOE_CHUNK = 512


def _params(**kw):
    return pltpu.CompilerParams(vmem_limit_bytes=VMEM_LIMIT_BYTES, **kw)


def _dot(a, b):
    return jnp.dot(a, b, preferred_element_type=F32)


def _split(a):
    hi = a.astype(BF16)
    lo = (a - hi.astype(F32)).astype(BF16)
    return hi, lo


def _rms(x):
    return x * lax.rsqrt(jnp.mean(x * x, axis=-1, keepdims=True) + EPS)


def _silu(x):
    return x / (1.0 + jnp.exp(-x))


def _mod_kernel(c_ref, w_ref, b_ref, o_ref):
    chi, clo = _split(_silu(c_ref[...]))
    whi, wlo = _split(w_ref[0])
    o_ref[0] = _dot(chi, whi) + _dot(clo, whi) + _dot(chi, wlo) + b_ref[0]


def _modulation(cond, w_ada, b_ada):
    depth, d, n = w_ada.shape
    tn = n // 4
    return pl.pallas_call(
        _mod_kernel,
        out_shape=jax.ShapeDtypeStruct((depth, cond.shape[0], n), F32),
        grid=(depth, n // tn),
        in_specs=[
            pl.BlockSpec(cond.shape, lambda l, j: (0, 0)),
            pl.BlockSpec((1, d, tn), lambda l, j: (l, 0, j)),
            pl.BlockSpec((1, 1, tn), lambda l, j: (l, 0, j)),
        ],
        out_specs=pl.BlockSpec((1, cond.shape[0], tn), lambda l, j: (l, 0, j)),
        compiler_params=_params(),
        name="adaln_modulation",
    )(cond, w_ada, b_ada.reshape(depth, 1, n))


def _inproj_kernel(x_ref, mod_ref, g1_ref, win_ref, gq_ref, gkv_ref, wq_ref, wkv_ref, bd_ref,
                   qg_ref, qgs_ref, kg_ref, kgs_ref, qoff_ref, koff_ref, voff_ref, cos_ref, sin_ref,
                   q_ref, k_ref, v_ref, f_ref):
    d = x_ref.shape[-1]
    m = mod_ref[0]
    shift, scale = m[:, 0:d], m[:, d:2 * d]
    h = _rms(x_ref[0]) * (g1_ref[...] * (1.0 + scale)) + shift
    p = _dot(h.astype(BF16), win_ref[...])
    nq = gq_ref.shape[-1]
    nkv = gkv_ref.shape[-1]
    nf = f_ref.shape[-1]
    o_f, o_pe = nq + nkv, nq + nkv + nf
    f_ref[0] = p[:, o_f:o_pe].astype(BF16)
    cq = (_rms(p[:, 0:nq]) * gq_ref[...]).astype(BF16)
    ckv = (_rms(p[:, nq:o_f]) * gkv_ref[...]).astype(BF16)
    qq = _dot(cq, wq_ref[...])
    kv = _dot(ckv, wkv_ref[...])
    hw = N_HEADS * LANES
    voff = voff_ref[...]
    for hd in range(N_HEADS):
        v_ref[0, hd] = (kv[:, hw + hd * LANES:hw + (hd + 1) * LANES] + voff).astype(BF16)
    kpe = p[:, o_pe:o_pe + LANES]
    kpe_sw = p[:, o_pe + LANES:o_pe + 2 * LANES]
    kpe2 = jnp.concatenate([kpe, kpe], axis=1)
    kpe_sw2 = jnp.concatenate([kpe_sw, kpe_sw], axis=1)

    cos, sin = cos_ref[...], sin_ref[...]

    def pair(t):
        return jnp.concatenate([t, t], axis=1)

    q_c, q_s = pair(qg_ref[...] * cos), pair(qgs_ref[...] * sin)
    k_c, k_s = pair(kg_ref[...] * cos), pair(kgs_ref[...] * sin)
    bd = bd_ref[...]

    def head_rsqrt(raw):
        hi, lo = _split(raw * raw)
        ss = _dot(hi, bd) + _dot(lo, bd)
        return lax.rsqrt(ss * (1.0 / QK_HEAD_DIM) + EPS)

    q_scale = QK_HEAD_DIM ** -0.5 * LOG2E
    qoff, koff = pair(qoff_ref[...]), pair(koff_ref[...])
    for hp in range(N_HEADS // 2):
        lo_, hi_ = hp * 2 * LANES, (hp + 1) * 2 * LANES
        q_raw, q_sw = qq[:, lo_:hi_], qq[:, hw + lo_:hw + hi_]
        qo = (head_rsqrt(q_raw) * q_scale) * (q_raw * q_c + q_sw * q_s) + qoff
        q_ref[0, 2 * hp] = qo[:, :LANES].astype(BF16)
        q_ref[0, 2 * hp + 1] = qo[:, LANES:].astype(BF16)
        k_raw = kv[:, lo_:hi_] + kpe2
        ko = head_rsqrt(k_raw) * (k_raw * k_c + kpe_sw2 * k_s) + koff
        k_ref[0, 2 * hp] = ko[:, :LANES].astype(BF16)
        k_ref[0, 2 * hp + 1] = ko[:, LANES:].astype(BF16)


def _mod_spec(n_lat_tiles, n_mod):
    def index(b, i):
        return (jnp.where(i < n_lat_tiles, b, pl.num_programs(0)), 0, 0)
    return pl.BlockSpec((1, 1, n_mod), index)


def _const_spec(shape):
    zeros = (0,) * len(shape)
    return pl.BlockSpec(shape, lambda b, i: zeros)


def _input_projection(xt, mod_l, lw, tabs, n_lat):
    bsz, stot, d = xt.shape
    tm = ROW_TILE
    hw = N_HEADS * LANES
    nf = FOURIER_GROUPS * FOURIER_GROUP_DIM
    tok = lambda n: pl.BlockSpec((1, tm, n), lambda b, i: (b, i, 0))
    rope = pl.BlockSpec((tm, LANES), lambda b, i: (i, 0))
    head_out = pl.BlockSpec((1, N_HEADS, tm, LANES), lambda b, i: (b, 0, i, 0))
    return pl.pallas_call(
        _inproj_kernel,
        out_shape=(
            jax.ShapeDtypeStruct((bsz, N_HEADS, stot, LANES), BF16),
            jax.ShapeDtypeStruct((bsz, N_HEADS, stot, LANES), BF16),
            jax.ShapeDtypeStruct((bsz, N_HEADS, stot, LANES), BF16),
            jax.ShapeDtypeStruct((bsz, stot, nf), BF16),
        ),
        grid=(bsz, stot // tm),
        in_specs=[
            tok(d), _mod_spec(n_lat // tm, mod_l.shape[-1]), _const_spec((1, d)),
            _const_spec(lw["w_in"].shape), _const_spec(lw["gq"].shape), _const_spec(lw["gkv"].shape),
            _const_spec(lw["w_q"].shape), _const_spec(lw["w_kv"].shape), _const_spec(tabs["bd"].shape),
        ] + [_const_spec((1, LANES))] * 7 + [rope, rope],
        out_specs=(head_out, head_out, head_out, tok(nf)),
        compiler_params=_params(),
        name="input_projection",
    )(xt, mod_l, lw["g1"], lw["w_in"], lw["gq"], lw["gkv"], lw["w_q"], lw["w_kv"], tabs["bd"],
      lw["qg"], lw["qg_sw"], lw["kg"], lw["kg_sw"], tabs["qoff"], lw["koff"], tabs["voff"],
      tabs["cos"], tabs["sin"])


def _scores(q, k_ref, hh, start, tk):
    k = k_ref[0, hh, pl.ds(start, tk), :]
    return lax.dot_general(q, k, (((1,), (1,)), ((), ())), preferred_element_type=F32)


def _attn_finish(accs, o_ref):
    outs = [a[:, :V_HEAD_DIM] / a[:, V_HEAD_DIM:V_HEAD_DIM + 1] for a in accs]
    o_ref[0] = jnp.concatenate(outs, axis=1).astype(BF16)


def _attn_bounded_kernel(q_ref, k_ref, v_ref, o_ref, *, n_chunks, tk):
    tq = q_ref.shape[2]
    qs = [q_ref[0, hh] for hh in range(2)]

    def body(j, accs):
        start = pl.multiple_of(j * tk, tk)
        new = []
        for hh in range(2):
            p = jnp.exp2(_scores(qs[hh], k_ref, hh, start, tk)).astype(BF16)
            new.append(accs[hh] + _dot(p, v_ref[0, hh, pl.ds(start, tk), :]))
        return tuple(new)

    zero = jnp.zeros((tq, LANES), F32)
    _attn_finish(lax.fori_loop(0, n_chunks, body, (zero, zero)), o_ref)


def _attn_online_kernel(q_ref, k_ref, v_ref, o_ref, *, n_chunks, tk):
    tq = q_ref.shape[2]
    qs = [q_ref[0, hh] for hh in range(2)]

    def body(j, carry):
        start = pl.multiple_of(j * tk, tk)
        new = []
        for hh in range(2):
            m, acc = carry[hh]
            s = _scores(qs[hh], k_ref, hh, start, tk)
            m_new = jnp.maximum(m, jnp.max(s, axis=-1, keepdims=True))
            p = jnp.exp2(s - m_new).astype(BF16)
            acc = jnp.exp2(m - m_new) * acc + _dot(p, v_ref[0, hh, pl.ds(start, tk), :])
            new.append((m_new, acc))
        return tuple(new)

    init = (jnp.full((tq, 1), -jnp.inf, F32), jnp.zeros((tq, LANES), F32))
    out = lax.fori_loop(0, n_chunks, body, (init, init))
    _attn_finish([out[0][1], out[1][1]], o_ref)


def _attention_calls(body, tag, q, k, v, n_lat, with_ctx):
    bsz, _, stot, _ = q.shape
    n_ctx = stot - n_lat
    hp = N_HEADS // 2
    tk = next(c for c in KV_CHUNKS if stot % c == 0)
    tq = Q_TILE
    nv = N_HEADS * V_HEAD_DIM
    kv_all = pl.BlockSpec((1, 2, stot, LANES), lambda b, h, i: (b, h, 0, 0))
    attn_lat = pl.pallas_call(
        functools.partial(body, n_chunks=stot // tk, tk=tk),
        out_shape=jax.ShapeDtypeStruct((bsz, n_lat, nv), BF16),
        grid=(bsz, hp, n_lat // tq),
        in_specs=[pl.BlockSpec((1, 2, tq, LANES), lambda b, h, i: (b, h, i, 0)), kv_all, kv_all],
        out_specs=pl.BlockSpec((1, tq, LANES), lambda b, h, i: (b, i, h)),
        compiler_params=_params(),
        name="latent_attention" + tag,
    )(q, k, v)
    if not with_ctx:
        return attn_lat
    cblk = n_lat // n_ctx
    ctx_rows = pl.BlockSpec((1, 2, n_ctx, LANES), lambda b, h: (b, h, cblk, 0))
    attn_ctx = pl.pallas_call(
        functools.partial(body, n_chunks=1, tk=n_ctx),
        out_shape=jax.ShapeDtypeStruct((bsz, n_ctx, nv), BF16),
        grid=(bsz, hp),
        in_specs=[ctx_rows, ctx_rows, ctx_rows],
        out_specs=pl.BlockSpec((1, n_ctx, LANES), lambda b, h: (b, 0, h)),
        compiler_params=_params(),
        name="context_attention" + tag,
    )(q, k, v)
    return attn_lat, attn_ctx


def _attention(q, k, v, bound, n_lat, with_ctx):
    out = lax.cond(
        bound <= MAX_DIRECT_BOUND,
        lambda: _attention_calls(_attn_bounded_kernel, "", q, k, v, n_lat, with_ctx),
        lambda: _attention_calls(_attn_online_kernel, "_online", q, k, v, n_lat, with_ctx))
    return out if with_ctx else (out, None)


def _fft1_kernel(cs_ref, x_ref, a_ref):
    a_ref[0] = _dot(cs_ref[...], x_ref[0]).astype(BF16)


def _fft2_kernel(ar_ref, ai_ref, m2_ref, wch_ref, o_ref):
    nf = wch_ref.shape[-1]
    n2 = ar_ref.shape[2]
    for j in range(ar_ref.shape[1]):
        slab = jnp.concatenate([ar_ref[0, j], ai_ref[0, j]], axis=0)
        z = _dot(m2_ref[j], slab)
        zc = jnp.concatenate([z[:n2], z[n2:]], axis=1).astype(BF16)
        o_ref[0, :, j * nf:(j + 1) * nf] = _dot(zc, wch_ref[...]).astype(BF16)


def _fft_ctx_kernel(cs_ref, x_ref, wch_ref, o_ref):
    n = x_ref.shape[1]
    z = _dot(cs_ref[...], x_ref[0])
    zc = jnp.concatenate([z[:n], z[n:]], axis=1).astype(BF16)
    o_ref[0] = _dot(zc, wch_ref[...]).astype(BF16)


def _fourier_mix(f, tabs, n_lat, with_ctx):
    bsz, stot, nf = f.shape
    n_ctx = stot - n_lat
    n2 = FFT_N2
    n1 = n_lat // n2
    kg = min(FFT_K1_GROUP, n1)
    tc = min(n2 * nf, 8192)
    a = pl.pallas_call(
        _fft1_kernel,
        out_shape=jax.ShapeDtypeStruct((bsz, 2 * n1, n2 * nf), BF16),
        grid=(bsz, n2 * nf // tc),
        in_specs=[
            pl.BlockSpec((2 * n1, n1), lambda b, j: (0, 0)),
            pl.BlockSpec((1, n1, tc), lambda b, j: (b, 0, j)),
        ],
        out_specs=pl.BlockSpec((1, 2 * n1, tc), lambda b, j: (b, 0, j)),
        compiler_params=_params(),
        name="fourier_stage1",
    )(tabs["cs1"], f.reshape(bsz, stot // n2, n2 * nf))
    a4 = a.reshape(bsz, 2 * n1, n2, nf)
    four_lat = pl.pallas_call(
        _fft2_kernel,
        out_shape=jax.ShapeDtypeStruct((bsz, n2, n1 * nf), BF16),
        grid=(bsz, n1 // kg),
        in_specs=[
            pl.BlockSpec((1, kg, n2, nf), lambda b, g: (b, g, 0, 0)),
            pl.BlockSpec((1, kg, n2, nf), lambda b, g: (b, n1 // kg + g, 0, 0)),
            pl.BlockSpec((kg, 2 * n2, 2 * n2), lambda b, g: (g, 0, 0)),
            pl.BlockSpec((2 * nf, nf), lambda b, g: (0, 0)),
        ],
        out_specs=pl.BlockSpec((1, n2, kg * nf), lambda b, g: (b, 0, g)),
        compiler_params=_params(),
        name="fourier_stage2",
    )(a4, a4, tabs["m2"], tabs["wch_lat"]).reshape(bsz, n_lat, nf)
    if not with_ctx:
        return four_lat, None
    cblk = n_lat // n_ctx
    four_ctx = pl.pallas_call(
        _fft_ctx_kernel,
        out_shape=jax.ShapeDtypeStruct((bsz, n_ctx, nf), BF16),
        grid=(bsz,),
        in_specs=[
            pl.BlockSpec((2 * n_ctx, n_ctx), lambda b: (0, 0)),
            pl.BlockSpec((1, n_ctx, nf), lambda b: (b, cblk, 0)),
            pl.BlockSpec((2 * nf, nf), lambda b: (0, 0)),
        ],
        out_specs=pl.BlockSpec((1, n_ctx, nf), lambda b: (b, 0, 0)),
        compiler_params=_params(),
        name="fourier_context",
    )(tabs["cs_ctx"], f, tabs["wch_ctx"])
    return four_lat, four_ctx


def _merge_kernel(x_ref, *rest, moe, n_lat_tiles, with_ctx):
    if with_ctx:
        al_ref, fl_ref, ac_ref, fc_ref = rest[:4]
        rest = rest[4:]
        is_ctx = pl.program_id(1) >= n_lat_tiles
        a = jnp.where(is_ctx, ac_ref[0], al_ref[0])
        f = jnp.where(is_ctx, fc_ref[0], fl_ref[0])
    else:
        a, f = rest[0][0], rest[1][0]
        rest = rest[2:]
    mod_ref, ga_ref, gf_ref, wout_ref, g2_ref = rest[:5]
    rest = rest[5:]
    d = x_ref.shape[-1]
    m = mod_ref[0]
    gate1, shift2, scale2 = m[:, 2 * d:3 * d], m[:, 3 * d:4 * d], m[:, 4 * d:5 * d]
    an = _rms(a.astype(F32)) * ga_ref[...]
    fn = _rms(f.astype(F32)) * gf_ref[...]
    y = _dot(jnp.concatenate([an, fn], axis=1).astype(BF16), wout_ref[...])
    xn = x_ref[0] + gate1 * y
    h2 = _rms(xn) * (g2_ref[...] * (1.0 + scale2)) + shift2
    if not moe:
        xo_ref, h_ref = rest
        xo_ref[0] = xn
        h_ref[0] = h2.astype(BF16)
        return
    wr_hi_ref, wr_lo_ref, xo_ref, h_ref, rt_ref = rest
    xo_ref[0] = xn
    h_ref[0] = h2
    hi, lo = _split(h2)
    logits = _dot(hi, wr_hi_ref[...]) + _dot(lo, wr_hi_ref[...]) + _dot(hi, wr_lo_ref[...])
    lane = lax.broadcasted_iota(jnp.int32, logits.shape, 1).astype(F32)
    lg = jnp.where(lane < N_EXPERTS, logits, -jnp.inf)
    m1 = jnp.max(lg, axis=-1, keepdims=True)
    i1 = jnp.min(jnp.where(lg == m1, lane, float(LANES)), axis=-1, keepdims=True)
    lg2 = jnp.where(lane == i1, -jnp.inf, lg)
    m2 = jnp.max(lg2, axis=-1, keepdims=True)
    i2 = jnp.min(jnp.where(lg2 == m2, lane, float(LANES)), axis=-1, keepdims=True)
    e = jnp.exp(m2 - m1)
    w1 = 1.0 / (1.0 + e)
    w2 = e * w1
    rt_ref[0] = jnp.where(lane == 0, i1, jnp.where(lane == 1, i2, jnp.where(lane == 2, w1,
                                                                          jnp.where(lane == 3, w2, 0.0))))


def _merge(xt, attn, four, mod_l, lw, n_lat, moe):
    bsz, _, d = xt.shape
    tm = ROW_TILE
    with_ctx = attn[1] is not None
    n_lat_tiles = n_lat // tm
    n_out = n_lat + (attn[1].shape[1] if with_ctx else 0)
    assert not with_ctx or attn[1].shape[1] == tm
    tok = lambda n: pl.BlockSpec((1, tm, n), lambda b, i: (b, i, 0))
    lat = lambda n: pl.BlockSpec((1, tm, n), lambda b, i: (b, jnp.minimum(i, n_lat_tiles - 1), 0))
    ctx = lambda n: pl.BlockSpec((1, tm, n), lambda b, i: (b, 0, 0))
    na, nf = attn[0].shape[-1], four[0].shape[-1]
    in_specs = [tok(d), lat(na), lat(nf)] + ([ctx(na), ctx(nf)] if with_ctx else [])
    args = [xt, attn[0], four[0]] + ([attn[1], four[1]] if with_ctx else [])
    in_specs += [_mod_spec(n_lat_tiles, mod_l.shape[-1]),
                 _const_spec((1, na)), _const_spec((1, nf)), _const_spec((na + nf, d)), _const_spec((1, d))]
    args += [mod_l, lw["ga"], lw["gf"], lw["w_out"], lw["g2"]]
    out_shape = [jax.ShapeDtypeStruct((bsz, n_out, d), F32),
                 jax.ShapeDtypeStruct((bsz, n_out, d), F32 if moe else BF16)]
    out_specs = [tok(d), tok(d)]
    if moe:
        in_specs += [_const_spec((d, LANES)), _const_spec((d, LANES))]
        args += [lw["wr_hi"], lw["wr_lo"]]
        out_shape.append(jax.ShapeDtypeStruct((bsz, n_out, LANES), F32))
        out_specs.append(tok(LANES))
    return pl.pallas_call(
        functools.partial(_merge_kernel, moe=moe, n_lat_tiles=n_lat_tiles, with_ctx=with_ctx),
        out_shape=tuple(out_shape),
        grid=(bsz, n_out // tm),
        in_specs=in_specs,
        out_specs=tuple(out_specs),
        compiler_params=_params(),
        name="merge_router" if moe else "merge",
    )(*args)


def _swiglu_chunks(h, wg_ref, wu_ref, wd_ref, chunk, lead=()):
    f = wg_ref.shape[-1]
    acc = None
    for c in range(f // chunk):
        sl = slice(c * chunk, (c + 1) * chunk)
        g = _dot(h, wg_ref[lead + (slice(None), sl)])
        u = _dot(h, wu_ref[lead + (slice(None), sl)])
        part = _dot((_silu(g) * u).astype(BF16), wd_ref[lead + (sl, slice(None))])
        acc = part if acc is None else acc + part
    return acc


def _ffn_kernel(x_ref, h_ref, mod_ref, wg_ref, wu_ref, wd_ref, o_ref):
    d = x_ref.shape[-1]
    gate2 = mod_ref[0][:, 5 * d:6 * d]
    y = _swiglu_chunks(h_ref[0], wg_ref, wu_ref, wd_ref, FFN_CHUNK)
    o_ref[0] = x_ref[0] + gate2 * y


def _dense_ffn(xt, h2, mod_l, lw, n_lat, n_out):
    bsz, _, d = xt.shape
    tm = ROW_TILE
    tok = lambda: pl.BlockSpec((1, tm, d), lambda b, i: (b, i, 0))
    return pl.pallas_call(
        _ffn_kernel,
        out_shape=jax.ShapeDtypeStruct((bsz, n_out, d), F32),
        grid=(bsz, n_out // tm),
        in_specs=[tok(), tok(), _mod_spec(n_lat // tm, mod_l.shape[-1]),
                  _const_spec(lw["w_gate"].shape), _const_spec(lw["w_up"].shape),
                  _const_spec(lw["w_down"].shape)],
        out_specs=tok(),
        compiler_params=_params(),
        name="dense_swiglu",
    )(xt, h2, mod_l, lw["w_gate"], lw["w_up"], lw["w_down"])


def _gather_rows_kernel(idx_ref, src_ref, o_ref, sems):
    rows = o_ref.shape[0]

    def copy(r):
        return pltpu.make_async_copy(src_ref.at[pl.ds(idx_ref[0, 0, r], 1), :],
                                     o_ref.at[pl.ds(r, 1), :], sems.at[r])

    def start(r, c):
        copy(r).start()
        return c

    def wait(r, c):
        copy(r).wait()
        return c

    lax.fori_loop(0, rows, start, 0)
    lax.fori_loop(0, rows, wait, 0)


def _gather_rows(src, idx, rows_per_block):
    n_blocks = idx.shape[0] // rows_per_block
    d = src.shape[-1]
    return pl.pallas_call(
        _gather_rows_kernel,
        out_shape=jax.ShapeDtypeStruct((idx.shape[0], d), src.dtype),
        grid=(n_blocks,),
        in_specs=[
            pl.BlockSpec((1, 1, rows_per_block), lambda n: (n, 0, 0), memory_space=pltpu.SMEM),
            pl.BlockSpec(memory_space=pl.ANY),
        ],
        out_specs=pl.BlockSpec((rows_per_block, d), lambda n: (n, 0)),
        scratch_shapes=[pltpu.SemaphoreType.DMA((rows_per_block,))],
        compiler_params=_params(),
        name="moe_gather_rows",
    )(idx.reshape(n_blocks, 1, rows_per_block), src)


def _expert_kernel(be_ref, nu_ref, x_ref, wg_ref, wu_ref, wd_ref, o_ref):
    used = pl.program_id(0) < nu_ref[0]

    @pl.when(used)
    def _():
        o_ref[...] = _swiglu_chunks(x_ref[...].astype(BF16), wg_ref, wu_ref, wd_ref, MOE_CHUNK, lead=(0,))

    @pl.when(jnp.logical_not(used))
    def _():
        o_ref[...] = jnp.zeros(o_ref.shape, o_ref.dtype)


def _expert_blocks(xs, block_exp, n_used, lw):
    n_rows, d = xs.shape
    tm = EXPERT_ROWS
    f = lw["w_moe_gate"].shape[-1]
    wspec = lambda shape: pl.BlockSpec((1,) + shape, lambda n, be, nu: (be[n], 0, 0),
                                       pipeline_mode=pl.Buffered(1))
    return pl.pallas_call(
        _expert_kernel,
        out_shape=jax.ShapeDtypeStruct((n_rows, d), F32),
        grid_spec=pltpu.PrefetchScalarGridSpec(
            num_scalar_prefetch=2,
            grid=(n_rows // tm,),
            in_specs=[pl.BlockSpec((tm, d), lambda n, be, nu: (n, 0)),
                      wspec((d, f)), wspec((d, f)), wspec((f, d))],
            out_specs=pl.BlockSpec((tm, d), lambda n, be, nu: (n, 0)),
        ),
        compiler_params=_params(),
        name="moe_expert_blocks",
    )(block_exp, n_used, xs, lw["w_moe_gate"], lw["w_moe_up"], lw["w_moe_down"])


def _combine_kernel(pos_ref, ys_ref, x_ref, rt_ref, mod_ref, o_ref, buf, sems):
    tm, d = buf.shape[1], buf.shape[2]

    def copy(r, j):
        return pltpu.make_async_copy(ys_ref.at[pl.ds(pos_ref[0, 0, TOP_K * r + j], 1), :],
                                     buf.at[j, pl.ds(r, 1), :], sems.at[j, r])

    def start(r, c):
        copy(r, 0).start()
        copy(r, 1).start()
        return c

    def wait(r, c):
        copy(r, 0).wait()
        copy(r, 1).wait()
        return c

    lax.fori_loop(0, tm, start, 0)
    lax.fori_loop(0, tm, wait, 0)
    gate2 = mod_ref[0][:, 5 * d:6 * d]
    rt = rt_ref[0]
    y = rt[:, 2:3] * buf[0] + rt[:, 3:4] * buf[1]
    o_ref[0] = x_ref[0] + gate2 * y


def _combine(ys, pos, xt, rt, mod_l, n_lat, n_out):
    bsz, stot, d = xt.shape
    tm = ROW_TILE
    tiles = stot // tm
    tok = lambda n: pl.BlockSpec((1, tm, n), lambda b, i: (b, i, 0))
    return pl.pallas_call(
        _combine_kernel,
        out_shape=jax.ShapeDtypeStruct((bsz, n_out, d), F32),
        grid=(bsz, n_out // tm),
        in_specs=[
            pl.BlockSpec((1, 1, TOP_K * tm), lambda b, i: (b * tiles + i, 0, 0), memory_space=pltpu.SMEM),
            pl.BlockSpec(memory_space=pl.ANY),
            tok(d), tok(LANES), _mod_spec(n_lat // tm, mod_l.shape[-1]),
        ],
        out_specs=tok(d),
        scratch_shapes=[pltpu.VMEM((TOP_K, tm, d), F32), pltpu.SemaphoreType.DMA((TOP_K, tm))],
        compiler_params=_params(),
        name="moe_combine",
    )(pos.reshape(bsz * tiles, 1, TOP_K * tm), ys, xt, rt, mod_l)


def _moe_ffn(xt, h2, rt, mod_l, lw, n_lat, n_out):
    bsz, stot, d = xt.shape
    n_tok = bsz * stot
    n_assign = n_tok * TOP_K
    tm = EXPERT_ROWS
    exp_flat = rt[..., :TOP_K].astype(jnp.int32).reshape(n_assign)
    onehot = (exp_flat[:, None] == jnp.arange(N_EXPERTS, dtype=jnp.int32)[None, :]).astype(jnp.int32)
    csum = jnp.cumsum(onehot, axis=0)
    rank = jnp.sum(jnp.where(onehot > 0, csum, 0), axis=1) - 1
    counts = csum[-1]
    padded = (counts + tm - 1) // tm * tm
    pad_ends = jnp.cumsum(padded)
    pad_starts = pad_ends - padded
    dest = (jnp.sum(onehot * pad_starts[None, :], axis=1) + rank).astype(jnp.int32)
    n_blocks = (n_assign + N_EXPERTS * (tm - 1) + tm - 1) // tm
    n_rows = n_blocks * tm
    tok_flat = jnp.arange(n_assign, dtype=jnp.int32) // TOP_K
    row_tok = jnp.zeros((n_rows,), jnp.int32).at[dest].set(tok_flat)
    block_start = jnp.arange(n_blocks, dtype=jnp.int32) * tm
    block_exp = jnp.minimum(jnp.sum((pad_ends[None, :] <= block_start[:, None]).astype(jnp.int32), axis=1),
                            N_EXPERTS - 1).astype(jnp.int32)
    n_used = (pad_ends[-1:] // tm).astype(jnp.int32)

    xs = _gather_rows(h2.reshape(n_tok, d), row_tok, tm)
    ys = _expert_blocks(xs, block_exp, n_used, lw)
    return _combine(ys, dest, xt, rt, mod_l, n_lat, n_out)


def _pair_swap_index():
    j = jnp.arange(QK_ROPE_DIM)
    return jnp.where((j % 16) < 8, j + 8, j - 8)


def _head_lanes(v):
    pad = [(0, 0)] * (v.ndim - 1) + [(0, LANES - QK_HEAD_DIM)]
    return jnp.pad(v, pad)


def _swap_rope(v):
    rope = v[..., QK_NOPE_DIM:][..., _pair_swap_index()]
    return _head_lanes(jnp.concatenate([jnp.zeros_like(v[..., :QK_NOPE_DIM]), rope], axis=-1))


def _dft_angles(n_rows, n_cols, period):
    idx = (jnp.arange(n_rows, dtype=jnp.int32)[:, None] * jnp.arange(n_cols, dtype=jnp.int32)[None, :]) % period
    return idx.astype(F32) * (2.0 * math.pi / period)


def _tables(n_lat, n_ctx):
    nf = FOURIER_GROUPS * FOURIER_GROUP_DIM
    rows = n_lat // GRID_W
    r = jnp.repeat(jnp.arange(rows, dtype=F32), GRID_W)
    col = jnp.tile(jnp.arange(GRID_W, dtype=F32), rows)
    half = QK_ROPE_DIM // 2
    inv_freq = ROPE_THETA ** (-jnp.arange(0, half, 2, dtype=F32) / half)
    ar, ac = r[:, None] * inv_freq, col[:, None] * inv_freq
    ones = jnp.ones((n_lat, QK_NOPE_DIM), F32)
    cos = jnp.concatenate([ones, jnp.cos(ar), jnp.cos(ar), jnp.cos(ac), jnp.cos(ac)], axis=1)
    sin = jnp.concatenate([0 * ones, -jnp.sin(ar), jnp.sin(ar), -jnp.sin(ac), jnp.sin(ac)], axis=1)
    cos = jnp.concatenate([cos, jnp.ones((n_ctx, QK_HEAD_DIM), F32)], axis=0)
    sin = jnp.concatenate([sin, jnp.zeros((n_ctx, QK_HEAD_DIM), F32)], axis=0)
    lane = jnp.arange(2 * LANES)
    bd = (lane[:, None] // LANES == lane[None, :] // LANES).astype(BF16)
    n2 = FFT_N2
    n1 = n_lat // n2
    a1 = _dft_angles(n1, n1, n1)
    cs1 = jnp.concatenate([jnp.cos(a1), -jnp.sin(a1)], axis=0).astype(BF16)
    k = (jnp.arange(n1, dtype=jnp.int32)[:, None, None] + n1 * jnp.arange(n2, dtype=jnp.int32)[None, :, None])
    ang = ((k * jnp.arange(n2, dtype=jnp.int32)[None, None, :]) % n_lat).astype(F32) * (2.0 * math.pi / n_lat)
    c2, s2 = jnp.cos(ang), jnp.sin(ang)
    m2 = jnp.concatenate([jnp.concatenate([c2, s2], axis=2), jnp.concatenate([-s2, c2], axis=2)], axis=1)
    ach = _dft_angles(FOURIER_GROUP_DIM, FOURIER_GROUP_DIM, FOURIER_GROUP_DIM)
    eye = jnp.eye(FOURIER_GROUPS, dtype=F32)
    wch = jnp.concatenate([jnp.kron(eye, jnp.cos(ach)), jnp.kron(eye, jnp.sin(ach))], axis=0)
    actx = _dft_angles(n_ctx, n_ctx, n_ctx)
    return {
        "cos": _head_lanes(cos), "sin": _head_lanes(sin), "bd": bd,
        "qoff": jnp.zeros((1, LANES), F32).at[0, QK_HEAD_DIM].set(1.0),
        "voff": jnp.zeros((1, LANES), F32).at[0, V_HEAD_DIM].set(1.0),
        "cs1": cs1, "m2": m2.astype(BF16),
        "wch_lat": (wch * (n_lat * FOURIER_GROUP_DIM) ** -0.5).astype(BF16),
        "wch_ctx": (wch * (n_ctx * FOURIER_GROUP_DIM) ** -0.5).astype(BF16),
        "cs_ctx": jnp.concatenate([jnp.cos(actx), -jnp.sin(actx)], axis=0).astype(BF16),
    }


def _layer_weights(layer, p):
    q_rank = p["q_lora_norm"].shape[-1]
    kv_rank = p["kv_lora_norm"].shape[-1]
    o_pe = q_rank + kv_rank
    o_f = o_pe + QK_ROPE_DIM
    w_in = p["w_in"][layer]
    d = w_in.shape[0]
    w_pe = w_in[:, o_pe:o_f]
    z64 = jnp.zeros((d, QK_NOPE_DIM), F32)
    z32 = jnp.zeros((d, LANES - QK_HEAD_DIM), F32)
    w_in_cat = jnp.concatenate(
        [w_in[:, :o_pe], w_in[:, o_f:], z64, w_pe, z32, z64, w_pe[:, _pair_swap_index()], z32], axis=1)
    w_uq = p["w_uq"][layer].reshape(q_rank, N_HEADS, QK_HEAD_DIM)
    w_q = jnp.concatenate([_head_lanes(w_uq).reshape(q_rank, -1), _swap_rope(w_uq).reshape(q_rank, -1)], axis=1)
    w_ukv = p["w_ukv"][layer].reshape(kv_rank, N_HEADS, QK_NOPE_DIM + V_HEAD_DIM)
    w_k = jnp.pad(w_ukv[..., :QK_NOPE_DIM], ((0, 0), (0, 0), (0, LANES - QK_NOPE_DIM))).reshape(kv_rank, -1)
    w_v = jnp.pad(w_ukv[..., QK_NOPE_DIM:], ((0, 0), (0, 0), (0, LANES - V_HEAD_DIM))).reshape(kv_rank, -1)
    row = lambda v: v.reshape(1, -1)
    bound = LOG2E * (1.01 * QK_HEAD_DIM ** 0.5 * jnp.max(jnp.abs(p["q_norm"][layer]))
                     * jnp.max(jnp.abs(p["k_norm"][layer])) + 0.1)
    lw = {
        "g1": row(p["norm1"][layer]), "g2": row(p["norm2"][layer]),
        "w_in": w_in_cat.astype(BF16),
        "gq": row(p["q_lora_norm"][layer]), "gkv": row(p["kv_lora_norm"][layer]),
        "w_q": w_q.astype(BF16), "w_kv": jnp.concatenate([w_k, w_v], axis=1).astype(BF16),
        "bound": bound, "koff": jnp.zeros((1, LANES), F32).at[0, QK_HEAD_DIM].set(-bound),
        "qg": row(_head_lanes(p["q_norm"][layer])), "qg_sw": row(_swap_rope(p["q_norm"][layer])),
        "kg": row(_head_lanes(p["k_norm"][layer])), "kg_sw": row(_swap_rope(p["k_norm"][layer])),
        "ga": row(p["out_norm_attn"][layer]), "gf": row(p["out_norm_fourier"][layer]),
        "w_out": p["w_out"][layer].astype(BF16),
    }
    if layer % 2 == 0:
        lw.update(w_gate=p["w_ffn_gate"][layer // 2].astype(BF16), w_up=p["w_ffn_up"][layer // 2].astype(BF16),
                  w_down=p["w_ffn_down"][layer // 2].astype(BF16))
    else:
        wr = jnp.pad(p["w_router"][layer // 2], ((0, 0), (0, LANES - N_EXPERTS)))
        wr_hi, wr_lo = _split(wr)
        lw.update(wr_hi=wr_hi, wr_lo=wr_lo,
                  w_moe_gate=p["w_moe_gate"][layer // 2].astype(BF16),
                  w_moe_up=p["w_moe_up"][layer // 2].astype(BF16),
                  w_moe_down=p["w_moe_down"][layer // 2].astype(BF16))
    return lw


def kernel(x, c, ctx, c_ctx, w_ada, b_ada, norm1, w_in, q_lora_norm, kv_lora_norm, w_uq, w_ukv, q_norm, k_norm,
           out_norm_attn, out_norm_fourier, w_out, norm2, w_ffn_gate, w_ffn_up, w_ffn_down, w_router,
           w_moe_gate, w_moe_up, w_moe_down):
    params = dict(norm1=norm1, w_in=w_in, q_lora_norm=q_lora_norm, kv_lora_norm=kv_lora_norm, w_uq=w_uq,
                  w_ukv=w_ukv, q_norm=q_norm, k_norm=k_norm, out_norm_attn=out_norm_attn,
                  out_norm_fourier=out_norm_fourier, w_out=w_out, norm2=norm2, w_ffn_gate=w_ffn_gate,
                  w_ffn_up=w_ffn_up, w_ffn_down=w_ffn_down, w_router=w_router, w_moe_gate=w_moe_gate,
                  w_moe_up=w_moe_up, w_moe_down=w_moe_down)
    bsz, n_lat, d = x.shape
    n_ctx = ctx.shape[1]
    depth = w_ada.shape[0]
    assert n_lat % Q_TILE == 0 and n_lat % n_ctx == 0 and n_ctx % ROW_TILE == 0
    assert n_lat % FFT_N2 == 0 and n_lat % GRID_W == 0 and n_ctx % FFT_N2 == 0

    cond = jnp.concatenate([c, c_ctx[None, :], jnp.zeros((8 - bsz - 1, d), F32)], axis=0)
    mod = _modulation(cond, w_ada, b_ada)
    tabs = _tables(n_lat, n_ctx)
    xt = jnp.concatenate([x, ctx], axis=1)

    for layer in range(depth):
        last = layer == depth - 1
        lw = _layer_weights(layer, params)
        mod_l = mod[layer].reshape(mod.shape[1], 1, mod.shape[2])
        q, k, v, f = _input_projection(xt, mod_l, lw, tabs, n_lat)
        attn = _attention(q, k, v, lw["bound"], n_lat, not last)
        four = _fourier_mix(f, tabs, n_lat, not last)
        moe = layer % 2 == 1
        n_out = n_lat if last else n_lat + n_ctx
        if moe:
            xt, h2, rt = _merge(xt, attn, four, mod_l, lw, n_lat, True)
            xt = _moe_ffn(xt, h2, rt, mod_l, lw, n_lat, n_out)
        else:
            xt, h2 = _merge(xt, attn, four, mod_l, lw, n_lat, False)
            xt = _dense_ffn(xt, h2, mod_l, lw, n_lat, n_out)
    return xt[:, :n_lat]
```

```python
import functools
import math

import jax
import jax.numpy as jnp
from jax import lax
from jax.experimental import pallas as pl
from jax.experimental.pallas import tpu as pltpu

F32 = jnp.float32
BF16 = jnp.bfloat16

N_HEADS = 8
QK_NOPE_DIM = 64
QK_ROPE_DIM = 32
QK_HEAD_DIM = QK_NOPE_DIM + QK_ROPE_DIM
V_HEAD_DIM = 64
GRID_W = 64
ROPE_THETA = 10000.0
FOURIER_GROUPS = 4
FOURIER_GROUP_DIM = 128
N_EXPERTS = 8
TOP_K = 2
EPS = 1e-6
LOG2E = 1.4426950408889634
MAX_DIRECT_BOUND = 50.0

LANES = 128
MXU_DIM = 256
VMEM_LIMIT_BYTES = 60 * 1024 * 1024

ROW_TILE = 256
Q_TILE = 1024
KV_CHUNKS = (1408, 640, 512, 256)
FFT_N2 = 128
FFT_K1_GROUP = 8
EXPERT_ROWS = 512
FFN_CHUNK = 1408
MOE_CHUNK = 512
DMA_UNROLL = 8


def _params(**kw):
    return pltpu.CompilerParams(vmem_limit_bytes=VMEM_LIMIT_BYTES, **kw)


def _dot(a, b):
    return jnp.dot(a, b, preferred_element_type=F32)


def _split(a):
    hi = a.astype(BF16)
    lo = (a - hi.astype(F32)).astype(BF16)
    return hi, lo


def _rms(x):
    return x * lax.rsqrt(jnp.mean(x * x, axis=-1, keepdims=True) + EPS)


def _silu(x):
    return x / (1.0 + jnp.exp(-x))


def _mod_kernel(c_ref, w_ref, b_ref, o_ref):
    chi, clo = _split(_silu(c_ref[...]))
    whi, wlo = _split(w_ref[0])
    o_ref[0] = _dot(chi, whi) + _dot(clo, whi) + _dot(chi, wlo) + b_ref[0]


def _modulation(cond, w_ada, b_ada):
    depth, d, n = w_ada.shape
    tn = n // 4
    return pl.pallas_call(
        _mod_kernel,
        out_shape=jax.ShapeDtypeStruct((depth, cond.shape[0], n), F32),
        grid=(depth, n // tn),
        in_specs=[
            pl.BlockSpec(cond.shape, lambda l, j: (0, 0)),
            pl.BlockSpec((1, d, tn), lambda l, j: (l, 0, j)),
            pl.BlockSpec((1, 1, tn), lambda l, j: (l, 0, j)),
        ],
        out_specs=pl.BlockSpec((1, cond.shape[0], tn), lambda l, j: (l, 0, j)),
        compiler_params=_params(),
        name="adaln_modulation",
    )(cond, w_ada, b_ada.reshape(depth, 1, n))


def _inproj_kernel(x_ref, mod_ref, g1_ref, win_ref, gq_ref, gkv_ref, wq_ref, wkv_ref, bd_ref,
                   qg_ref, qgs_ref, kg_ref, kgs_ref, qoff_ref, koff_ref, voff_ref, cos_ref, sin_ref,
                   q_ref, k_ref, v_ref, f_ref):
    d = x_ref.shape[-1]
    m = mod_ref[0]
    shift, scale = m[:, 0:d], m[:, d:2 * d]
    h = _rms(x_ref[0]) * (g1_ref[...] * (1.0 + scale)) + shift
    p = _dot(h.astype(BF16), win_ref[...])
    nq = gq_ref.shape[-1]
    nkv = gkv_ref.shape[-1]
    nf = f_ref.shape[-1]
    o_f, o_pe = nq + nkv, nq + nkv + nf
    f_ref[0] = p[:, o_f:o_pe].astype(BF16)
    cq = (_rms(p[:, 0:nq]) * gq_ref[...]).astype(BF16)
    ckv = (_rms(p[:, nq:o_f]) * gkv_ref[...]).astype(BF16)
    qq = _dot(cq, wq_ref[...])
    kv = _dot(ckv, wkv_ref[...])
    hw = N_HEADS * LANES
    voff = voff_ref[...]
    for hd in range(N_HEADS):
        v_ref[0, hd] = (kv[:, hw + hd * LANES:hw + (hd + 1) * LANES] + voff).astype(BF16)
    kpe = p[:, o_pe:o_pe + LANES]
    kpe_sw = p[:, o_pe + LANES:o_pe + 2 * LANES]
    kpe2 = jnp.concatenate([kpe, kpe], axis=1)
    kpe_sw2 = jnp.concatenate([kpe_sw, kpe_sw], axis=1)

    cos, sin = cos_ref[...], sin_ref[...]

    def pair(t):
        return jnp.concatenate([t, t], axis=1)

    q_c, q_s = pair(qg_ref[...] * cos), pair(qgs_ref[...] * sin)
    k_c, k_s = pair(kg_ref[...] * cos), pair(kgs_ref[...] * sin)
    bd = bd_ref[...]

    def head_rsqrt(raw):
        hi, lo = _split(raw * raw)
        ss = _dot(hi, bd) + _dot(lo, bd)
        return lax.rsqrt(ss * (1.0 / QK_HEAD_DIM) + EPS)

    q_scale = QK_HEAD_DIM ** -0.5 * LOG2E
    qoff, koff = pair(qoff_ref[...]), pair(koff_ref[...])
    for hp in range(N_HEADS // 2):
        lo_, hi_ = hp * 2 * LANES, (hp + 1) * 2 * LANES
        q_raw, q_sw = qq[:, lo_:hi_], qq[:, hw + lo_:hw + hi_]
        qo = (head_rsqrt(q_raw) * q_scale) * (q_raw * q_c + q_sw * q_s) + qoff
        q_ref[0, 2 * hp] = qo[:, :LANES].astype(BF16)
        q_ref[0, 2 * hp + 1] = qo[:, LANES:].astype(BF16)
        k_raw = kv[:, lo_:hi_] + kpe2
        ko = head_rsqrt(k_raw) * (k_raw * k_c + kpe_sw2 * k_s) + koff
        k_ref[0, 2 * hp] = ko[:, :LANES].astype(BF16)
        k_ref[0, 2 * hp + 1] = ko[:, LANES:].astype(BF16)


def _mod_spec(n_lat_tiles, n_mod):
    def index(b, i):
        return (jnp.where(i < n_lat_tiles, b, pl.num_programs(0)), 0, 0)
    return pl.BlockSpec((1, 1, n_mod), index)


def _const_spec(shape):
    zeros = (0,) * len(shape)
    return pl.BlockSpec(shape, lambda b, i: zeros)


def _input_projection(xt, mod_l, lw, tabs, n_lat):
    bsz, stot, d = xt.shape
    tm = ROW_TILE
    hw = N_HEADS * LANES
    nf = FOURIER_GROUPS * FOURIER_GROUP_DIM
    tok = lambda n: pl.BlockSpec((1, tm, n), lambda b, i: (b, i, 0))
    rope = pl.BlockSpec((tm, LANES), lambda b, i: (i, 0))
    head_out = pl.BlockSpec((1, N_HEADS, tm, LANES), lambda b, i: (b, 0, i, 0))
    return pl.pallas_call(
        _inproj_kernel,
        out_shape=(
            jax.ShapeDtypeStruct((bsz, N_HEADS, stot, LANES), BF16),
            jax.ShapeDtypeStruct((bsz, N_HEADS, stot, LANES), BF16),
            jax.ShapeDtypeStruct((bsz, N_HEADS, stot, LANES), BF16),
            jax.ShapeDtypeStruct((bsz, stot, nf), BF16),
        ),
        grid=(bsz, stot // tm),
        in_specs=[
            tok(d), _mod_spec(n_lat // tm, mod_l.shape[-1]), _const_spec((1, d)),
            _const_spec(lw["w_in"].shape), _const_spec(lw["gq"].shape), _const_spec(lw["gkv"].shape),
            _const_spec(lw["w_q"].shape), _const_spec(lw["w_kv"].shape), _const_spec(tabs["bd"].shape),
        ] + [_const_spec((1, LANES))] * 7 + [rope, rope],
        out_specs=(head_out, head_out, head_out, tok(nf)),
        compiler_params=_params(),
        name="input_projection",
    )(xt, mod_l, lw["g1"], lw["w_in"], lw["gq"], lw["gkv"], lw["w_q"], lw["w_kv"], tabs["bd"],
      lw["qg"], lw["qg_sw"], lw["kg"], lw["kg_sw"], tabs["qoff"], lw["koff"], tabs["voff"],
      tabs["cos"], tabs["sin"])


def _scores(q, k_ref, hh, start, tk):
    k = k_ref[0, hh, pl.ds(start, tk), :]
    return lax.dot_general(q, k, (((1,), (1,)), ((), ())), preferred_element_type=F32)


def _attn_finish(accs, o_ref):
    outs = [a[:, :V_HEAD_DIM] / a[:, V_HEAD_DIM:V_HEAD_DIM + 1] for a in accs]
    o_ref[0] = jnp.concatenate(outs, axis=1).astype(BF16)


def _attn_bounded_kernel(q_ref, k_ref, v_ref, o_ref, *, n_chunks, tk):
    tq = q_ref.shape[2]
    qs = [q_ref[0, hh] for hh in range(2)]

    def body(j, accs):
        start = pl.multiple_of(j * tk, tk)
        new = []
        for hh in range(2):
            p = jnp.exp2(_scores(qs[hh], k_ref, hh, start, tk)).astype(BF16)
            new.append(accs[hh] + _dot(p, v_ref[0, hh, pl.ds(start, tk), :]))
        return tuple(new)

    zero = jnp.zeros((tq, LANES), F32)
    _attn_finish(lax.fori_loop(0, n_chunks, body, (zero, zero)), o_ref)


def _attn_online_kernel(q_ref, k_ref, v_ref, o_ref, *, n_chunks, tk):
    tq = q_ref.shape[2]
    qs = [q_ref[0, hh] for hh in range(2)]

    def body(j, carry):
        start = pl.multiple_of(j * tk, tk)
        new = []
        for hh in range(2):
            m, acc = carry[hh]
            s = _scores(qs[hh], k_ref, hh, start, tk)
            m_new = jnp.maximum(m, jnp.max(s, axis=-1, keepdims=True))
            p = jnp.exp2(s - m_new).astype(BF16)
            acc = jnp.exp2(m - m_new) * acc + _dot(p, v_ref[0, hh, pl.ds(start, tk), :])
            new.append((m_new, acc))
        return tuple(new)

    init = (jnp.full((tq, 1), -jnp.inf, F32), jnp.zeros((tq, LANES), F32))
    out = lax.fori_loop(0, n_chunks, body, (init, init))
    _attn_finish([out[0][1], out[1][1]], o_ref)


def _attention_calls(body, tag, q, k, v, n_lat, with_ctx):
    bsz, _, stot, _ = q.shape
    n_ctx = stot - n_lat
    hp = N_HEADS // 2
    tk = next(c for c in KV_CHUNKS if stot % c == 0)
    tq = Q_TILE
    nv = N_HEADS * V_HEAD_DIM
    kv_all = pl.BlockSpec((1, 2, stot, LANES), lambda b, h, i: (b, h, 0, 0))
    attn_lat = pl.pallas_call(
        functools.partial(body, n_chunks=stot // tk, tk=tk),
        out_shape=jax.ShapeDtypeStruct((bsz, n_lat, nv), BF16),
        grid=(bsz, hp, n_lat // tq),
        in_specs=[pl.BlockSpec((1, 2, tq, LANES), lambda b, h, i: (b, h, i, 0)), kv_all, kv_all],
        out_specs=pl.BlockSpec((1, tq, LANES), lambda b, h, i: (b, i, h)),
        compiler_params=_params(),
        name="latent_attention" + tag,
    )(q, k, v)
    if not with_ctx:
        return attn_lat
    cblk = n_lat // n_ctx
    ctx_rows = pl.BlockSpec((1, 2, n_ctx, LANES), lambda b, h: (b, h, cblk, 0))
    attn_ctx = pl.pallas_call(
        functools.partial(body, n_chunks=1, tk=n_ctx),
        out_shape=jax.ShapeDtypeStruct((bsz, n_ctx, nv), BF16),
        grid=(bsz, hp),
        in_specs=[ctx_rows, ctx_rows, ctx_rows],
        out_specs=pl.BlockSpec((1, n_ctx, LANES), lambda b, h: (b, 0, h)),
        compiler_params=_params(),
        name="context_attention" + tag,
    )(q, k, v)
    return attn_lat, attn_ctx


def _attention(q, k, v, bound, n_lat, with_ctx):
    out = lax.cond(
        bound <= MAX_DIRECT_BOUND,
        lambda: _attention_calls(_attn_bounded_kernel, "", q, k, v, n_lat, with_ctx),
        lambda: _attention_calls(_attn_online_kernel, "_online", q, k, v, n_lat, with_ctx))
    return out if with_ctx else (out, None)


def _fft1_kernel(cs_ref, x_ref, a_ref):
    a_ref[0] = _dot(cs_ref[...], x_ref[0]).astype(BF16)


def _fft2_kernel(ar_ref, ai_ref, m2_ref, wch_ref, o_ref):
    nf = wch_ref.shape[-1]
    n2 = ar_ref.shape[2]
    for j in range(ar_ref.shape[1]):
        slab = jnp.concatenate([ar_ref[0, j], ai_ref[0, j]], axis=0)
        z = _dot(m2_ref[j], slab)
        zc = jnp.concatenate([z[:n2], z[n2:]], axis=1).astype(BF16)
        o_ref[0, :, j * nf:(j + 1) * nf] = _dot(zc, wch_ref[...]).astype(BF16)


def _fft_ctx_kernel(cs_ref, x_ref, wch_ref, o_ref):
    n = x_ref.shape[1]
    z = _dot(cs_ref[...], x_ref[0])
    zc = jnp.concatenate([z[:n], z[n:]], axis=1).astype(BF16)
    o_ref[0] = _dot(zc, wch_ref[...]).astype(BF16)


def _fourier_mix(f, tabs, n_lat, with_ctx):
    bsz, stot, nf = f.shape
    n_ctx = stot - n_lat
    n2 = FFT_N2
    n1 = n_lat // n2
    kg = min(FFT_K1_GROUP, n1)
    tc = min(n2 * nf, 8192)
    a = pl.pallas_call(
        _fft1_kernel,
        out_shape=jax.ShapeDtypeStruct((bsz, 2 * n1, n2 * nf), BF16),
        grid=(bsz, n2 * nf // tc),
        in_specs=[
            pl.BlockSpec((2 * n1, n1), lambda b, j: (0, 0)),
            pl.BlockSpec((1, n1, tc), lambda b, j: (b, 0, j)),
        ],
        out_specs=pl.BlockSpec((1, 2 * n1, tc), lambda b, j: (b, 0, j)),
        compiler_params=_params(),
        name="fourier_stage1",
    )(tabs["cs1"], f.reshape(bsz, stot // n2, n2 * nf))
    a4 = a.reshape(bsz, 2 * n1, n2, nf)
    four_lat = pl.pallas_call(
        _fft2_kernel,
        out_shape=jax.ShapeDtypeStruct((bsz, n2, n1 * nf), BF16),
        grid=(bsz, n1 // kg),
        in_specs=[
            pl.BlockSpec((1, kg, n2, nf), lambda b, g: (b, g, 0, 0)),
            pl.BlockSpec((1, kg, n2, nf), lambda b, g: (b, n1 // kg + g, 0, 0)),
            pl.BlockSpec((kg, 2 * n2, 2 * n2), lambda b, g: (g, 0, 0)),
            pl.BlockSpec((2 * nf, nf), lambda b, g: (0, 0)),
        ],
        out_specs=pl.BlockSpec((1, n2, kg * nf), lambda b, g: (b, 0, g)),
        compiler_params=_params(),
        name="fourier_stage2",
    )(a4, a4, tabs["m2"], tabs["wch_lat"]).reshape(bsz, n_lat, nf)
    if not with_ctx:
        return four_lat, None
    cblk = n_lat // n_ctx
    four_ctx = pl.pallas_call(
        _fft_ctx_kernel,
        out_shape=jax.ShapeDtypeStruct((bsz, n_ctx, nf), BF16),
        grid=(bsz,),
        in_specs=[
            pl.BlockSpec((2 * n_ctx, n_ctx), lambda b: (0, 0)),
            pl.BlockSpec((1, n_ctx, nf), lambda b: (b, cblk, 0)),
            pl.BlockSpec((2 * nf, nf), lambda b: (0, 0)),
        ],
        out_specs=pl.BlockSpec((1, n_ctx, nf), lambda b: (b, 0, 0)),
        compiler_params=_params(),
        name="fourier_context",
    )(tabs["cs_ctx"], f, tabs["wch_ctx"])
    return four_lat, four_ctx


def _merge_kernel(x_ref, *rest, moe, n_lat_tiles, with_ctx):
    if with_ctx:
        al_ref, fl_ref, ac_ref, fc_ref = rest[:4]
        rest = rest[4:]
        is_ctx = pl.program_id(1) >= n_lat_tiles
        a = jnp.where(is_ctx, ac_ref[0], al_ref[0])
        f = jnp.where(is_ctx, fc_ref[0], fl_ref[0])
    else:
        a, f = rest[0][0], rest[1][0]
        rest = rest[2:]
    mod_ref, ga_ref, gf_ref, wout_ref, g2_ref = rest[:5]
    rest = rest[5:]
    d = x_ref.shape[-1]
    m = mod_ref[0]
    gate1, shift2, scale2 = m[:, 2 * d:3 * d], m[:, 3 * d:4 * d], m[:, 4 * d:5 * d]
    an = _rms(a.astype(F32)) * ga_ref[...]
    fn = _rms(f.astype(F32)) * gf_ref[...]
    y = _dot(jnp.concatenate([an, fn], axis=1).astype(BF16), wout_ref[...])
    xn = x_ref[0] + gate1 * y
    h2 = _rms(xn) * (g2_ref[...] * (1.0 + scale2)) + shift2
    if not moe:
        xo_ref, h_ref = rest
        xo_ref[0] = xn
        h_ref[0] = h2.astype(BF16)
        return
    wr_hi_ref, wr_lo_ref, xo_ref, h_ref, rt_ref = rest
    xo_ref[0] = xn
    _to_slabs(h_ref, h2)
    hi, lo = _split(h2)
    logits = _dot(hi, wr_hi_ref[...]) + _dot(lo, wr_hi_ref[...]) + _dot(hi, wr_lo_ref[...])
    lane = lax.broadcasted_iota(jnp.int32, logits.shape, 1).astype(F32)
    lg = jnp.where(lane < N_EXPERTS, logits, -jnp.inf)
    m1 = jnp.max(lg, axis=-1, keepdims=True)
    i1 = jnp.min(jnp.where(lg == m1, lane, float(LANES)), axis=-1, keepdims=True)
    lg2 = jnp.where(lane == i1, -jnp.inf, lg)
    m2 = jnp.max(lg2, axis=-1, keepdims=True)
    i2 = jnp.min(jnp.where(lg2 == m2, lane, float(LANES)), axis=-1, keepdims=True)
    e = jnp.exp(m2 - m1)
    w1 = 1.0 / (1.0 + e)
    w2 = e * w1
    rt_ref[0] = jnp.where(lane == 0, i1, jnp.where(lane == 1, i2, jnp.where(lane == 2, w1,
                                                                          jnp.where(lane == 3, w2, 0.0))))


def _merge(xt, attn, four, mod_l, lw, n_lat, moe):
    bsz, _, d = xt.shape
    tm = ROW_TILE
    with_ctx = attn[1] is not None
    n_lat_tiles = n_lat // tm
    n_out = n_lat + (attn[1].shape[1] if with_ctx else 0)
    assert not with_ctx or attn[1].shape[1] == tm
    tok = lambda n: pl.BlockSpec((1, tm, n), lambda b, i: (b, i, 0))
    lat = lambda n: pl.BlockSpec((1, tm, n), lambda b, i: (b, jnp.minimum(i, n_lat_tiles - 1), 0))
    ctx = lambda n: pl.BlockSpec((1, tm, n), lambda b, i: (b, 0, 0))
    na, nf = attn[0].shape[-1], four[0].shape[-1]
    in_specs = [tok(d), lat(na), lat(nf)] + ([ctx(na), ctx(nf)] if with_ctx else [])
    args = [xt, attn[0], four[0]] + ([attn[1], four[1]] if with_ctx else [])
    in_specs += [_mod_spec(n_lat_tiles, mod_l.shape[-1]),
                 _const_spec((1, na)), _const_spec((1, nf)), _const_spec((na + nf, d)), _const_spec((1, d))]
    args += [mod_l, lw["ga"], lw["gf"], lw["w_out"], lw["g2"]]
    out_shape = [jax.ShapeDtypeStruct((bsz, n_out, d), F32), jax.ShapeDtypeStruct((bsz, n_out, d), BF16)]
    out_specs = [tok(d), tok(d)]
    if moe:
        slab, tiles = d // LANES, n_out // tm
        out_shape[1] = jax.ShapeDtypeStruct((bsz * n_out * slab, LANES), F32)
        out_specs[1] = pl.BlockSpec((tm * slab, LANES), lambda b, i: (b * tiles + i, 0))
        in_specs += [_const_spec((d, LANES)), _const_spec((d, LANES))]
        args += [lw["wr_hi"], lw["wr_lo"]]
        out_shape.append(jax.ShapeDtypeStruct((bsz, n_out, LANES), F32))
        out_specs.append(tok(LANES))
    return pl.pallas_call(
        functools.partial(_merge_kernel, moe=moe, n_lat_tiles=n_lat_tiles, with_ctx=with_ctx),
        out_shape=tuple(out_shape),
        grid=(bsz, n_out // tm),
        in_specs=in_specs,
        out_specs=tuple(out_specs),
        compiler_params=_params(),
        name="merge_router" if moe else "merge",
    )(*args)


def _swiglu_chunks(h, wg_ref, wu_ref, wd_ref, chunk, lead=()):
    f = wg_ref.shape[-1]
    acc = None
    for c in range(f // chunk):
        sl = slice(c * chunk, (c + 1) * chunk)
        g = _dot(h, wg_ref[lead + (slice(None), sl)])
        u = _dot(h, wu_ref[lead + (slice(None), sl)])
        part = _dot((_silu(g) * u).astype(BF16), wd_ref[lead + (sl, slice(None))])
        acc = part if acc is None else acc + part
    return acc


def _ffn_kernel(x_ref, h_ref, mod_ref, wg_ref, wu_ref, wd_ref, o_ref):
    d = x_ref.shape[-1]
    gate2 = mod_ref[0][:, 5 * d:6 * d]
    y = _swiglu_chunks(h_ref[0], wg_ref, wu_ref, wd_ref, FFN_CHUNK)
    o_ref[0] = x_ref[0] + gate2 * y


def _dense_ffn(xt, h2, mod_l, lw, n_lat, n_out):
    bsz, _, d = xt.shape
    tm = ROW_TILE
    tok = lambda: pl.BlockSpec((1, tm, d), lambda b, i: (b, i, 0))
    return pl.pallas_call(
        _ffn_kernel,
        out_shape=jax.ShapeDtypeStruct((bsz, n_out, d), F32),
        grid=(bsz, n_out // tm),
        in_specs=[tok(), tok(), _mod_spec(n_lat // tm, mod_l.shape[-1]),
                  _const_spec(lw["w_gate"].shape), _const_spec(lw["w_up"].shape),
                  _const_spec(lw["w_down"].shape)],
        out_specs=tok(),
        compiler_params=_params(),
        name="dense_swiglu",
    )(xt, h2, mod_l, lw["w_gate"], lw["w_up"], lw["w_down"])


def _to_slabs(ref, val):
    rows, n = val.shape[0], val.shape[1] // LANES
    for s in range(n):
        ref[pl.ds(s, rows, stride=n), :] = val[:, s * LANES:(s + 1) * LANES]


def _from_slabs(ref, n):
    rows = ref.shape[0] // n
    return jnp.concatenate([ref[pl.ds(s, rows, stride=n), :] for s in range(n)], axis=1)


def _row_copies(n_rows, make_copy):
    def start(r, c):
        for cp in make_copy(r):
            cp.start()
        return c

    def wait(r, c):
        for cp in make_copy(r):
            cp.wait()
        return c

    lax.fori_loop(0, n_rows, start, 0, unroll=DMA_UNROLL)
    lax.fori_loop(0, n_rows, wait, 0, unroll=DMA_UNROLL)


def _dispatch_kernel(dest_ref, h_ref, _, xs_ref, sems, *, slab):
    tm = h_ref.shape[0] // slab

    def copies(r):
        src = h_ref.at[pl.ds(pl.multiple_of(r * slab, slab), slab), :]
        return [pltpu.make_async_copy(
            src, xs_ref.at[pl.ds(pl.multiple_of(dest_ref[0, 0, TOP_K * r + j] * slab, slab), slab), :],
            sems.at[j, r]) for j in range(TOP_K)]

    _row_copies(tm, copies)


def _dispatch(h2, dest, n_rows, slab):
    tm = ROW_TILE
    n_tiles = h2.shape[0] // (tm * slab)
    return pl.pallas_call(
        functools.partial(_dispatch_kernel, slab=slab),
        out_shape=jax.ShapeDtypeStruct((n_rows * slab, LANES), h2.dtype),
        grid=(n_tiles,),
        in_specs=[
            pl.BlockSpec((1, 1, TOP_K * tm), lambda n: (n, 0, 0), memory_space=pltpu.SMEM),
            pl.BlockSpec((tm * slab, LANES), lambda n: (n, 0)),
            pl.BlockSpec(memory_space=pl.ANY),
        ],
        out_specs=pl.BlockSpec(memory_space=pl.ANY),
        scratch_shapes=[pltpu.SemaphoreType.DMA((TOP_K, tm))],
        input_output_aliases={2: 0},
        compiler_params=_params(),
        name="moe_dispatch",
    )(dest.reshape(n_tiles, 1, TOP_K * tm), h2, jnp.zeros((n_rows * slab, LANES), h2.dtype))


def _expert_kernel(be_ref, nu_ref, x_ref, wg_ref, wu_ref, wd_ref, o_ref, *, slab):
    used = pl.program_id(0) < nu_ref[0]

    @pl.when(used)
    def _():
        x = _from_slabs(x_ref, slab).astype(BF16)
        _to_slabs(o_ref, _swiglu_chunks(x, wg_ref, wu_ref, wd_ref, MOE_CHUNK, lead=(0,)))

    @pl.when(jnp.logical_not(used))
    def _():
        o_ref[...] = jnp.zeros(o_ref.shape, o_ref.dtype)


def _expert_blocks(xs, block_exp, n_used, lw, slab):
    tm = EXPERT_ROWS
    d, f = lw["w_moe_gate"].shape[-2:]
    wspec = lambda shape: pl.BlockSpec((1,) + shape, lambda n, be, nu: (be[n], 0, 0),
                                       pipeline_mode=pl.Buffered(1))
    rows = pl.BlockSpec((tm * slab, LANES), lambda n, be, nu: (n, 0))
    return pl.pallas_call(
        functools.partial(_expert_kernel, slab=slab),
        out_shape=jax.ShapeDtypeStruct(xs.shape, F32),
        grid_spec=pltpu.PrefetchScalarGridSpec(
            num_scalar_prefetch=2,
            grid=(xs.shape[0] // (tm * slab),),
            in_specs=[rows, wspec((d, f)), wspec((d, f)), wspec((f, d))],
            out_specs=rows,
        ),
        compiler_params=_params(),
        name="moe_expert_blocks",
    )(block_exp, n_used, xs, lw["w_moe_gate"], lw["w_moe_up"], lw["w_moe_down"])


def _combine_kernel(pos_ref, ys_ref, x_ref, rt_ref, mod_ref, o_ref, buf, sems, *, slab):
    tm, d = x_ref.shape[1], x_ref.shape[2]

    def copies(r):
        return [pltpu.make_async_copy(
            ys_ref.at[pl.ds(pl.multiple_of(pos_ref[0, 0, TOP_K * r + j] * slab, slab), slab), :],
            buf.at[j, pl.ds(pl.multiple_of(r * slab, slab), slab), :], sems.at[j, r]) for j in range(TOP_K)]

    _row_copies(tm, copies)
    gate2 = mod_ref[0][:, 5 * d:6 * d]
    rt = rt_ref[0]
    y = rt[:, 2:3] * _from_slabs(buf.at[0], slab) + rt[:, 3:4] * _from_slabs(buf.at[1], slab)
    o_ref[0] = x_ref[0] + gate2 * y


def _combine(ys, pos, xt, rt, mod_l, n_lat, n_out, slab):
    bsz, stot, d = xt.shape
    tm = ROW_TILE
    tiles = stot // tm
    tok = lambda n: pl.BlockSpec((1, tm, n), lambda b, i: (b, i, 0))
    return pl.pallas_call(
        functools.partial(_combine_kernel, slab=slab),
        out_shape=jax.ShapeDtypeStruct((bsz, n_out, d), F32),
        grid=(bsz, n_out // tm),
        in_specs=[
            pl.BlockSpec((1, 1, TOP_K * tm), lambda b, i: (b * tiles + i, 0, 0), memory_space=pltpu.SMEM),
            pl.BlockSpec(memory_space=pl.ANY),
            tok(d), tok(LANES), _mod_spec(n_lat // tm, mod_l.shape[-1]),
        ],
        out_specs=tok(d),
        scratch_shapes=[pltpu.VMEM((TOP_K, tm * slab, LANES), F32), pltpu.SemaphoreType.DMA((TOP_K, tm))],
        compiler_params=_params(),
        name="moe_combine",
    )(pos.reshape(bsz * tiles, 1, TOP_K * tm), ys, xt, rt, mod_l)


def _moe_ffn(xt, h2, rt, mod_l, lw, n_lat, n_out):
    bsz, stot, d = xt.shape
    slab = d // LANES
    n_assign = bsz * stot * TOP_K
    tm = EXPERT_ROWS
    exp_flat = rt[..., :TOP_K].astype(jnp.int32).reshape(n_assign)
    onehot = (exp_flat[:, None] == jnp.arange(N_EXPERTS, dtype=jnp.int32)[None, :]).astype(jnp.int32)
    csum = jnp.cumsum(onehot, axis=0)
    rank = jnp.sum(jnp.where(onehot > 0, csum, 0), axis=1) - 1
    counts = csum[-1]
    padded = (counts + tm - 1) // tm * tm
    pad_ends = jnp.cumsum(padded)
    pad_starts = pad_ends - padded
    dest = (jnp.sum(onehot * pad_starts[None, :], axis=1) + rank).astype(jnp.int32)
    n_blocks = (n_assign + N_EXPERTS * (tm - 1) + tm - 1) // tm
    block_start = jnp.arange(n_blocks, dtype=jnp.int32) * tm
    block_exp = jnp.minimum(jnp.sum((pad_ends[None, :] <= block_start[:, None]).astype(jnp.int32), axis=1),
                            N_EXPERTS - 1).astype(jnp.int32)
    n_used = (pad_ends[-1:] // tm).astype(jnp.int32)

    xs = _dispatch(h2, dest, n_blocks * tm, slab)
    ys = _expert_blocks(xs, block_exp, n_used, lw, slab)
    return _combine(ys, dest, xt, rt, mod_l, n_lat, n_out, slab)


def _pair_swap_index():
    j = jnp.arange(QK_ROPE_DIM)
    return jnp.where((j % 16) < 8, j + 8, j - 8)


def _head_lanes(v):
    pad = [(0, 0)] * (v.ndim - 1) + [(0, LANES - QK_HEAD_DIM)]
    return jnp.pad(v, pad)


def _swap_rope(v):
    rope = v[..., QK_NOPE_DIM:][..., _pair_swap_index()]
    return _head_lanes(jnp.concatenate([jnp.zeros_like(v[..., :QK_NOPE_DIM]), rope], axis=-1))


def _dft_angles(n_rows, n_cols, period):
    idx = (jnp.arange(n_rows, dtype=jnp.int32)[:, None] * jnp.arange(n_cols, dtype=jnp.int32)[None, :]) % period
    return idx.astype(F32) * (2.0 * math.pi / period)


def _tables(n_lat, n_ctx):
    nf = FOURIER_GROUPS * FOURIER_GROUP_DIM
    rows = n_lat // GRID_W
    r = jnp.repeat(jnp.arange(rows, dtype=F32), GRID_W)
    col = jnp.tile(jnp.arange(GRID_W, dtype=F32), rows)
    half = QK_ROPE_DIM // 2
    inv_freq = ROPE_THETA ** (-jnp.arange(0, half, 2, dtype=F32) / half)
    ar, ac = r[:, None] * inv_freq, col[:, None] * inv_freq
    ones = jnp.ones((n_lat, QK_NOPE_DIM), F32)
    cos = jnp.concatenate([ones, jnp.cos(ar), jnp.cos(ar), jnp.cos(ac), jnp.cos(ac)], axis=1)
    sin = jnp.concatenate([0 * ones, -jnp.sin(ar), jnp.sin(ar), -jnp.sin(ac), jnp.sin(ac)], axis=1)
    cos = jnp.concatenate([cos, jnp.ones((n_ctx, QK_HEAD_DIM), F32)], axis=0)
    sin = jnp.concatenate([sin, jnp.zeros((n_ctx, QK_HEAD_DIM), F32)], axis=0)
    lane = jnp.arange(2 * LANES)
    bd = (lane[:, None] // LANES == lane[None, :] // LANES).astype(BF16)
    n2 = FFT_N2
    n1 = n_lat // n2
    a1 = _dft_angles(n1, n1, n1)
    cs1 = jnp.concatenate([jnp.cos(a1), -jnp.sin(a1)], axis=0).astype(BF16)
    k = (jnp.arange(n1, dtype=jnp.int32)[:, None, None] + n1 * jnp.arange(n2, dtype=jnp.int32)[None, :, None])
    ang = ((k * jnp.arange(n2, dtype=jnp.int32)[None, None, :]) % n_lat).astype(F32) * (2.0 * math.pi / n_lat)
    c2, s2 = jnp.cos(ang), jnp.sin(ang)
    m2 = jnp.concatenate([jnp.concatenate([c2, s2], axis=2), jnp.concatenate([-s2, c2], axis=2)], axis=1)
    ach = _dft_angles(FOURIER_GROUP_DIM, FOURIER_GROUP_DIM, FOURIER_GROUP_DIM)
    eye = jnp.eye(FOURIER_GROUPS, dtype=F32)
    wch = jnp.concatenate([jnp.kron(eye, jnp.cos(ach)), jnp.kron(eye, jnp.sin(ach))], axis=0)
    actx = _dft_angles(n_ctx, n_ctx, n_ctx)
    return {
        "cos": _head_lanes(cos), "sin": _head_lanes(sin), "bd": bd,
        "qoff": jnp.zeros((1, LANES), F32).at[0, QK_HEAD_DIM].set(1.0),
        "voff": jnp.zeros((1, LANES), F32).at[0, V_HEAD_DIM].set(1.0),
        "cs1": cs1, "m2": m2.astype(BF16),
        "wch_lat": (wch * (n_lat * FOURIER_GROUP_DIM) ** -0.5).astype(BF16),
        "wch_ctx": (wch * (n_ctx * FOURIER_GROUP_DIM) ** -0.5).astype(BF16),
        "cs_ctx": jnp.concatenate([jnp.cos(actx), -jnp.sin(actx)], axis=0).astype(BF16),
    }


def _layer_weights(layer, p):
    q_rank = p["q_lora_norm"].shape[-1]
    kv_rank = p["kv_lora_norm"].shape[-1]
    o_pe = q_rank + kv_rank
    o_f = o_pe + QK_ROPE_DIM
    w_in = p["w_in"][layer]
    d = w_in.shape[0]
    w_pe = w_in[:, o_pe:o_f]
    z64 = jnp.zeros((d, QK_NOPE_DIM), F32)
    z32 = jnp.zeros((d, LANES - QK_HEAD_DIM), F32)
    w_in_cat = jnp.concatenate(
        [w_in[:, :o_pe], w_in[:, o_f:], z64, w_pe, z32, z64, w_pe[:, _pair_swap_index()], z32], axis=1)
    w_uq = p["w_uq"][layer].reshape(q_rank, N_HEADS, QK_HEAD_DIM)
    w_q = jnp.concatenate([_head_lanes(w_uq).reshape(q_rank, -1), _swap_rope(w_uq).reshape(q_rank, -1)], axis=1)
    w_ukv = p["w_ukv"][layer].reshape(kv_rank, N_HEADS, QK_NOPE_DIM + V_HEAD_DIM)
    w_k = jnp.pad(w_ukv[..., :QK_NOPE_DIM], ((0, 0), (0, 0), (0, LANES - QK_NOPE_DIM))).reshape(kv_rank, -1)
    w_v = jnp.pad(w_ukv[..., QK_NOPE_DIM:], ((0, 0), (0, 0), (0, LANES - V_HEAD_DIM))).reshape(kv_rank, -1)
    row = lambda v: v.reshape(1, -1)
    bound = LOG2E * (1.01 * QK_HEAD_DIM ** 0.5 * jnp.max(jnp.abs(p["q_norm"][layer]))
                     * jnp.max(jnp.abs(p["k_norm"][layer])) + 0.1)
    lw = {
        "g1": row(p["norm1"][layer]), "g2": row(p["norm2"][layer]),
        "w_in": w_in_cat.astype(BF16),
        "gq": row(p["q_lora_norm"][layer]), "gkv": row(p["kv_lora_norm"][layer]),
        "w_q": w_q.astype(BF16), "w_kv": jnp.concatenate([w_k, w_v], axis=1).astype(BF16),
        "bound": bound, "koff": jnp.zeros((1, LANES), F32).at[0, QK_HEAD_DIM].set(-bound),
        "qg": row(_head_lanes(p["q_norm"][layer])), "qg_sw": row(_swap_rope(p["q_norm"][layer])),
        "kg": row(_head_lanes(p["k_norm"][layer])), "kg_sw": row(_swap_rope(p["k_norm"][layer])),
        "ga": row(p["out_norm_attn"][layer]), "gf": row(p["out_norm_fourier"][layer]),
        "w_out": p["w_out"][layer].astype(BF16),
    }
    if layer % 2 == 0:
        lw.update(w_gate=p["w_ffn_gate"][layer // 2].astype(BF16), w_up=p["w_ffn_up"][layer // 2].astype(BF16),
                  w_down=p["w_ffn_down"][layer // 2].astype(BF16))
    else:
        wr = jnp.pad(p["w_router"][layer // 2], ((0, 0), (0, LANES - N_EXPERTS)))
        wr_hi, wr_lo = _split(wr)
        lw.update(wr_hi=wr_hi, wr_lo=wr_lo,
                  w_moe_gate=p["w_moe_gate"][layer // 2].astype(BF16),
                  w_moe_up=p["w_moe_up"][layer // 2].astype(BF16),
                  w_moe_down=p["w_moe_down"][layer // 2].astype(BF16))
    return lw


def kernel(x, c, ctx, c_ctx, w_ada, b_ada, norm1, w_in, q_lora_norm, kv_lora_norm, w_uq, w_ukv, q_norm, k_norm,
           out_norm_attn, out_norm_fourier, w_out, norm2, w_ffn_gate, w_ffn_up, w_ffn_down, w_router,
           w_moe_gate, w_moe_up, w_moe_down):
    params = dict(norm1=norm1, w_in=w_in, q_lora_norm=q_lora_norm, kv_lora_norm=kv_lora_norm, w_uq=w_uq,
                  w_ukv=w_ukv, q_norm=q_norm, k_norm=k_norm, out_norm_attn=out_norm_attn,
                  out_norm_fourier=out_norm_fourier, w_out=w_out, norm2=norm2, w_ffn_gate=w_ffn_gate,
                  w_ffn_up=w_ffn_up, w_ffn_down=w_ffn_down, w_router=w_router, w_moe_gate=w_moe_gate,
                  w_moe_up=w_moe_up, w_moe_down=w_moe_down)
    bsz, n_lat, d = x.shape
    n_ctx = ctx.shape[1]
    depth = w_ada.shape[0]
    assert n_lat % Q_TILE == 0 and n_lat % n_ctx == 0 and n_ctx % ROW_TILE == 0
    assert n_lat % FFT_N2 == 0 and n_lat % GRID_W == 0 and n_ctx % FFT_N2 == 0

    cond = jnp.concatenate([c, c_ctx[None, :], jnp.zeros((8 - bsz - 1, d), F32)], axis=0)
    mod = _modulation(cond, w_ada, b_ada)
    tabs = _tables(n_lat, n_ctx)
    xt = jnp.concatenate([x, ctx], axis=1)

    for layer in range(depth):
        last = layer == depth - 1
        lw = _layer_weights(layer, params)
        mod_l = mod[layer].reshape(mod.shape[1], 1, mod.shape[2])
        q, k, v, f = _input_projection(xt, mod_l, lw, tabs, n_lat)
        attn = _attention(q, k, v, lw["bound"], n_lat, not last)
        four = _fourier_mix(f, tabs, n_lat, not last)
        moe = layer % 2 == 1
        n_out = n_lat if last else n_lat + n_ctx
        if moe:
            xt, h2, rt = _merge(xt, attn, four, mod_l, lw, n_lat, True)
            xt = _moe_ffn(xt, h2, rt, mod_l, lw, n_lat, n_out)
        else:
            xt, h2 = _merge(xt, attn, four, mod_l, lw, n_lat, False)
            xt = _dense_ffn(xt, h2, mod_l, lw, n_lat, n_out)
    return xt[:, :n_lat]
```

```python
import functools
import math

import jax
import jax.numpy as jnp
from jax import lax
from jax.experimental import pallas as pl
from jax.experimental.pallas import tpu as pltpu

F32 = jnp.float32
BF16 = jnp.bfloat16

N_HEADS = 8
QK_NOPE_DIM = 64
QK_ROPE_DIM = 32
QK_HEAD_DIM = QK_NOPE_DIM + QK_ROPE_DIM
V_HEAD_DIM = 64
GRID_W = 64
ROPE_THETA = 10000.0
FOURIER_GROUPS = 4
FOURIER_GROUP_DIM = 128
N_EXPERTS = 8
TOP_K = 2
EPS = 1e-6
LOG2E = 1.4426950408889634
MAX_DIRECT_BOUND = 50.0

LANES = 128
MXU_DIM = 256
VMEM_LIMIT_BYTES = 60 * 1024 * 1024

ROW_TILE = 256
Q_TILE = 1024
KV_CHUNKS = (1408, 640, 512, 256)
FFT_N2 = 128
FFT_K1_GROUP = 8
EXPERT_ROWS = 512
FFN_CHUNK = 1408
MOE_CHUNK = 512
DMA_UNROLL = 8


def _params(**kw):
    return pltpu.CompilerParams(vmem_limit_bytes=VMEM_LIMIT_BYTES, **kw)


def _dot(a, b):
    return jnp.dot(a, b, preferred_element_type=F32)


def _split(a):
    hi = a.astype(BF16)
    lo = (a - hi.astype(F32)).astype(BF16)
    return hi, lo


def _rms(x):
    return x * lax.rsqrt(jnp.mean(x * x, axis=-1, keepdims=True) + EPS)


def _silu(x):
    return x / (1.0 + jnp.exp(-x))


def _mod_kernel(c_ref, w_ref, b_ref, o_ref):
    chi, clo = _split(_silu(c_ref[...]))
    whi, wlo = _split(w_ref[0])
    o_ref[0] = _dot(chi, whi) + _dot(clo, whi) + _dot(chi, wlo) + b_ref[0]


def _modulation(cond, w_ada, b_ada):
    depth, d, n = w_ada.shape
    tn = n // 4
    return pl.pallas_call(
        _mod_kernel,
        out_shape=jax.ShapeDtypeStruct((depth, cond.shape[0], n), F32),
        grid=(depth, n // tn),
        in_specs=[
            pl.BlockSpec(cond.shape, lambda l, j: (0, 0)),
            pl.BlockSpec((1, d, tn), lambda l, j: (l, 0, j)),
            pl.BlockSpec((1, 1, tn), lambda l, j: (l, 0, j)),
        ],
        out_specs=pl.BlockSpec((1, cond.shape[0], tn), lambda l, j: (l, 0, j)),
        compiler_params=_params(),
        name="adaln_modulation",
    )(cond, w_ada, b_ada.reshape(depth, 1, n))


def _token_specs(src, tm, n_lat_tiles):
    if not isinstance(src, tuple):
        return [pl.BlockSpec((1, tm, src.shape[-1]), lambda b, i: (b, i, 0))], [src]
    n = src[0].shape[-1]
    lat = pl.BlockSpec((1, tm, n), lambda b, i: (b, jnp.minimum(i, n_lat_tiles - 1), 0))
    ctx = pl.BlockSpec((1, tm, n), lambda b, i: (b, jnp.maximum(i - n_lat_tiles, 0), 0))
    return [lat, ctx], list(src)


def _token_tile(refs, n_lat_tiles):
    if len(refs) == 1:
        return refs[0][0]
    return jnp.where(pl.program_id(1) >= n_lat_tiles, refs[1][0], refs[0][0])


def _inproj_kernel(*refs, n_src, n_lat_tiles):
    x = _token_tile(refs[:n_src], n_lat_tiles)
    (mod_ref, g1_ref, win_ref, gq_ref, gkv_ref, wq_ref, wkv_ref, bd_ref, qg_ref, qgs_ref, kg_ref, kgs_ref,
     qoff_ref, koff_ref, voff_ref, cos_ref, sin_ref, q_ref, k_ref, v_ref, f_ref) = refs[n_src:]
    d = x.shape[-1]
    m = mod_ref[0]
    shift, scale = m[:, 0:d], m[:, d:2 * d]
    h = _rms(x) * (g1_ref[...] * (1.0 + scale)) + shift
    p = _dot(h.astype(BF16), win_ref[...])
    nq = gq_ref.shape[-1]
    nkv = gkv_ref.shape[-1]
    nf = f_ref.shape[-1]
    o_f, o_pe = nq + nkv, nq + nkv + nf
    f_ref[0] = p[:, o_f:o_pe].astype(BF16)
    cq = (_rms(p[:, 0:nq]) * gq_ref[...]).astype(BF16)
    ckv = (_rms(p[:, nq:o_f]) * gkv_ref[...]).astype(BF16)
    qq = _dot(cq, wq_ref[...])
    kv = _dot(ckv, wkv_ref[...])
    hw = N_HEADS * LANES
    voff = voff_ref[...]
    for hd in range(N_HEADS):
        v_ref[0, hd] = (kv[:, hw + hd * LANES:hw + (hd + 1) * LANES] + voff).astype(BF16)
    kpe = p[:, o_pe:o_pe + LANES]
    kpe_sw = p[:, o_pe + LANES:o_pe + 2 * LANES]
    kpe2 = jnp.concatenate([kpe, kpe], axis=1)
    kpe_sw2 = jnp.concatenate([kpe_sw, kpe_sw], axis=1)

    cos, sin = cos_ref[...], sin_ref[...]

    def pair(t):
        return jnp.concatenate([t, t], axis=1)

    q_c, q_s = pair(qg_ref[...] * cos), pair(qgs_ref[...] * sin)
    k_c, k_s = pair(kg_ref[...] * cos), pair(kgs_ref[...] * sin)
    bd = bd_ref[...]

    def head_rsqrt(raw):
        ss = _dot((raw * raw).astype(BF16), bd)
        return lax.rsqrt(ss * (1.0 / QK_HEAD_DIM) + EPS)

    q_scale = QK_HEAD_DIM ** -0.5 * LOG2E
    qoff, koff = pair(qoff_ref[...]), pair(koff_ref[...])
    for hp in range(N_HEADS // 2):
        lo_, hi_ = hp * 2 * LANES, (hp + 1) * 2 * LANES
        q_raw, q_sw = qq[:, lo_:hi_], qq[:, hw + lo_:hw + hi_]
        qo = (head_rsqrt(q_raw) * q_scale) * (q_raw * q_c + q_sw * q_s) + qoff
        q_ref[0, 2 * hp] = qo[:, :LANES].astype(BF16)
        q_ref[0, 2 * hp + 1] = qo[:, LANES:].astype(BF16)
        k_raw = kv[:, lo_:hi_] + kpe2
        ko = head_rsqrt(k_raw) * (k_raw * k_c + kpe_sw2 * k_s) + koff
        k_ref[0, 2 * hp] = ko[:, :LANES].astype(BF16)
        k_ref[0, 2 * hp + 1] = ko[:, LANES:].astype(BF16)


def _mod_spec(n_lat_tiles, n_mod):
    def index(b, i):
        return (jnp.where(i < n_lat_tiles, b, pl.num_programs(0)), 0, 0)
    return pl.BlockSpec((1, 1, n_mod), index)


def _const_spec(shape):
    zeros = (0,) * len(shape)
    return pl.BlockSpec(shape, lambda b, i: zeros)


def _input_projection(xt, mod_l, lw, tabs, n_lat):
    pair = isinstance(xt, tuple)
    bsz, d = (xt[0] if pair else xt).shape[0::2]
    stot = xt[0].shape[1] + xt[1].shape[1] if pair else xt.shape[1]
    tm = ROW_TILE
    nf = FOURIER_GROUPS * FOURIER_GROUP_DIM
    tok = lambda n: pl.BlockSpec((1, tm, n), lambda b, i: (b, i, 0))
    rope = pl.BlockSpec((tm, LANES), lambda b, i: (i, 0))
    head_out = pl.BlockSpec((1, N_HEADS, tm, LANES), lambda b, i: (b, 0, i, 0))
    x_specs, x_args = _token_specs(xt, tm, n_lat // tm)
    return pl.pallas_call(
        functools.partial(_inproj_kernel, n_src=len(x_args), n_lat_tiles=n_lat // tm),
        out_shape=(
            jax.ShapeDtypeStruct((bsz, N_HEADS, stot, LANES), BF16),
            jax.ShapeDtypeStruct((bsz, N_HEADS, stot, LANES), BF16),
            jax.ShapeDtypeStruct((bsz, N_HEADS, stot, LANES), BF16),
            jax.ShapeDtypeStruct((bsz, stot, nf), BF16),
        ),
        grid=(bsz, stot // tm),
        in_specs=x_specs + [
            _mod_spec(n_lat // tm, mod_l.shape[-1]), _const_spec((1, d)),
            _const_spec(lw["w_in"].shape), _const_spec(lw["gq"].shape), _const_spec(lw["gkv"].shape),
            _const_spec(lw["w_q"].shape), _const_spec(lw["w_kv"].shape), _const_spec(tabs["bd"].shape),
        ] + [_const_spec((1, LANES))] * 7 + [rope, rope],
        out_specs=(head_out, head_out, head_out, tok(nf)),
        compiler_params=_params(),
        name="input_projection",
    )(*x_args, mod_l, lw["g1"], lw["w_in"], lw["gq"], lw["gkv"], lw["w_q"], lw["w_kv"], tabs["bd"],
      lw["qg"], lw["qg_sw"], lw["kg"], lw["kg_sw"], tabs["qoff"], lw["koff"], tabs["voff"],
      tabs["cos"], tabs["sin"])


def _scores(q, k_ref, hh, start, tk):
    k = k_ref[0, hh, pl.ds(start, tk), :]
    return lax.dot_general(q, k, (((1,), (1,)), ((), ())), preferred_element_type=F32)


def _attn_finish(accs, o_ref):
    outs = [a[:, :V_HEAD_DIM] / a[:, V_HEAD_DIM:V_HEAD_DIM + 1] for a in accs]
    o_ref[0] = jnp.concatenate(outs, axis=1).astype(BF16)


def _attn_bounded_kernel(q_ref, k_ref, v_ref, o_ref, *, n_chunks, tk):
    tq = q_ref.shape[2]
    qs = [q_ref[0, hh] for hh in range(2)]

    def body(j, accs):
        start = pl.multiple_of(j * tk, tk)
        new = []
        for hh in range(2):
            p = jnp.exp2(_scores(qs[hh], k_ref, hh, start, tk)).astype(BF16)
            new.append(accs[hh] + _dot(p, v_ref[0, hh, pl.ds(start, tk), :]))
        return tuple(new)

    zero = jnp.zeros((tq, LANES), F32)
    _attn_finish(lax.fori_loop(0, n_chunks, body, (zero, zero)), o_ref)


def _attn_online_kernel(q_ref, k_ref, v_ref, o_ref, *, n_chunks, tk):
    tq = q_ref.shape[2]
    qs = [q_ref[0, hh] for hh in range(2)]

    def body(j, carry):
        start = pl.multiple_of(j * tk, tk)
        new = []
        for hh in range(2):
            m, acc = carry[hh]
            s = _scores(qs[hh], k_ref, hh, start, tk)
            m_new = jnp.maximum(m, jnp.max(s, axis=-1, keepdims=True))
            p = jnp.exp2(s - m_new).astype(BF16)
            acc = jnp.exp2(m - m_new) * acc + _dot(p, v_ref[0, hh, pl.ds(start, tk), :])
            new.append((m_new, acc))
        return tuple(new)

    init = (jnp.full((tq, 1), -jnp.inf, F32), jnp.zeros((tq, LANES), F32))
    out = lax.fori_loop(0, n_chunks, body, (init, init))
    _attn_finish([out[0][1], out[1][1]], o_ref)


def _attention_calls(body, tag, q, k, v, n_lat, with_ctx):
    bsz, _, stot, _ = q.shape
    n_ctx = stot - n_lat
    hp = N_HEADS // 2
    tk = next(c for c in KV_CHUNKS if stot % c == 0)
    tq = Q_TILE
    nv = N_HEADS * V_HEAD_DIM
    kv_all = pl.BlockSpec((1, 2, stot, LANES), lambda b, h, i: (b, h, 0, 0))
    attn_lat = pl.pallas_call(
        functools.partial(body, n_chunks=stot // tk, tk=tk),
        out_shape=jax.ShapeDtypeStruct((bsz, n_lat, nv), BF16),
        grid=(bsz, hp, n_lat // tq),
        in_specs=[pl.BlockSpec((1, 2, tq, LANES), lambda b, h, i: (b, h, i, 0)), kv_all, kv_all],
        out_specs=pl.BlockSpec((1, tq, LANES), lambda b, h, i: (b, i, h)),
        compiler_params=_params(),
        name="latent_attention" + tag,
    )(q, k, v)
    if not with_ctx:
        return attn_lat
    cblk = n_lat // n_ctx
    ctx_rows = pl.BlockSpec((1, 2, n_ctx, LANES), lambda b, h: (b, h, cblk, 0))
    attn_ctx = pl.pallas_call(
        functools.partial(body, n_chunks=1, tk=n_ctx),
        out_shape=jax.ShapeDtypeStruct((bsz, n_ctx, nv), BF16),
        grid=(bsz, hp),
        in_specs=[ctx_rows, ctx_rows, ctx_rows],
        out_specs=pl.BlockSpec((1, n_ctx, LANES), lambda b, h: (b, 0, h)),
        compiler_params=_params(),
        name="context_attention" + tag,
    )(q, k, v)
    return attn_lat, attn_ctx


def _attention(q, k, v, bound, n_lat, with_ctx):
    out = lax.cond(
        bound <= MAX_DIRECT_BOUND,
        lambda: _attention_calls(_attn_bounded_kernel, "", q, k, v, n_lat, with_ctx),
        lambda: _attention_calls(_attn_online_kernel, "_online", q, k, v, n_lat, with_ctx))
    return out if with_ctx else (out, None)


def _fft1_kernel(cs_ref, x_ref, a_ref):
    a_ref[0] = _dot(cs_ref[...], x_ref[0]).astype(BF16)


def _fft2_kernel(ar_ref, ai_ref, m2_ref, wch_ref, o_ref):
    nf = wch_ref.shape[-1]
    n2 = ar_ref.shape[2]
    for j in range(ar_ref.shape[1]):
        slab = jnp.concatenate([ar_ref[0, j], ai_ref[0, j]], axis=0)
        z = _dot(m2_ref[j], slab)
        zc = jnp.concatenate([z[:n2], z[n2:]], axis=1).astype(BF16)
        o_ref[0, :, j * nf:(j + 1) * nf] = _dot(zc, wch_ref[...]).astype(BF16)


def _fft_ctx_kernel(cs_ref, x_ref, wch_ref, o_ref):
    n = x_ref.shape[1]
    z = _dot(cs_ref[...], x_ref[0])
    zc = jnp.concatenate([z[:n], z[n:]], axis=1).astype(BF16)
    o_ref[0] = _dot(zc, wch_ref[...]).astype(BF16)


def _fourier_mix(f, tabs, n_lat, with_ctx):
    bsz, stot, nf = f.shape
    n_ctx = stot - n_lat
    n2 = FFT_N2
    n1 = n_lat // n2
    kg = min(FFT_K1_GROUP, n1)
    tc = min(n2 * nf, 8192)
    a = pl.pallas_call(
        _fft1_kernel,
        out_shape=jax.ShapeDtypeStruct((bsz, 2 * n1, n2 * nf), BF16),
        grid=(bsz, n2 * nf // tc),
        in_specs=[
            pl.BlockSpec((2 * n1, n1), lambda b, j: (0, 0)),
            pl.BlockSpec((1, n1, tc), lambda b, j: (b, 0, j)),
        ],
        out_specs=pl.BlockSpec((1, 2 * n1, tc), lambda b, j: (b, 0, j)),
        compiler_params=_params(),
        name="fourier_stage1",
    )(tabs["cs1"], f.reshape(bsz, stot // n2, n2 * nf))
    a4 = a.reshape(bsz, 2 * n1, n2, nf)
    four_lat = pl.pallas_call(
        _fft2_kernel,
        out_shape=jax.ShapeDtypeStruct((bsz, n2, n1 * nf), BF16),
        grid=(bsz, n1 // kg),
        in_specs=[
            pl.BlockSpec((1, kg, n2, nf), lambda b, g: (b, g, 0, 0)),
            pl.BlockSpec((1, kg, n2, nf), lambda b, g: (b, n1 // kg + g, 0, 0)),
            pl.BlockSpec((kg, 2 * n2, 2 * n2), lambda b, g: (g, 0, 0)),
            pl.BlockSpec((2 * nf, nf), lambda b, g: (0, 0)),
        ],
        out_specs=pl.BlockSpec((1, n2, kg * nf), lambda b, g: (b, 0, g)),
        compiler_params=_params(),
        name="fourier_stage2",
    )(a4, a4, tabs["m2"], tabs["wch_lat"]).reshape(bsz, n_lat, nf)
    if not with_ctx:
        return four_lat, None
    cblk = n_lat // n_ctx
    four_ctx = pl.pallas_call(
        _fft_ctx_kernel,
        out_shape=jax.ShapeDtypeStruct((bsz, n_ctx, nf), BF16),
        grid=(bsz,),
        in_specs=[
            pl.BlockSpec((2 * n_ctx, n_ctx), lambda b: (0, 0)),
            pl.BlockSpec((1, n_ctx, nf), lambda b: (b, cblk, 0)),
            pl.BlockSpec((2 * nf, nf), lambda b: (0, 0)),
        ],
        out_specs=pl.BlockSpec((1, n_ctx, nf), lambda b: (b, 0, 0)),
        compiler_params=_params(),
        name="fourier_context",
    )(tabs["cs_ctx"], f, tabs["wch_ctx"])
    return four_lat, four_ctx


def _merge_kernel(*refs, moe, n_lat_tiles, n_srcs):
    tiles = []
    for n in n_srcs:
        tiles.append(_token_tile(refs[:n], n_lat_tiles))
        refs = refs[n:]
    x, a, f = tiles
    mod_ref, ga_ref, gf_ref, wout_ref, g2_ref = refs[:5]
    rest = refs[5:]
    d = x.shape[-1]
    m = mod_ref[0]
    gate1, shift2, scale2 = m[:, 2 * d:3 * d], m[:, 3 * d:4 * d], m[:, 4 * d:5 * d]
    an = _rms(a.astype(F32)) * ga_ref[...]
    fn = _rms(f.astype(F32)) * gf_ref[...]
    y = _dot(jnp.concatenate([an, fn], axis=1).astype(BF16), wout_ref[...])
    xn = x + gate1 * y
    h2 = _rms(xn) * (g2_ref[...] * (1.0 + scale2)) + shift2
    if not moe:
        xo_ref, h_ref = rest
        xo_ref[0] = xn
        h_ref[0] = h2.astype(BF16)
        return
    wr_hi_ref, wr_lo_ref, xo_ref, h_ref, rt_ref = rest
    xo_ref[0] = xn
    _to_slabs(h_ref, h2)
    hi, lo = _split(h2)
    logits = _dot(hi, wr_hi_ref[...]) + _dot(lo, wr_hi_ref[...]) + _dot(hi, wr_lo_ref[...])
    lane = lax.broadcasted_iota(jnp.int32, logits.shape, 1).astype(F32)
    lg = jnp.where(lane < N_EXPERTS, logits, -jnp.inf)
    m1 = jnp.max(lg, axis=-1, keepdims=True)
    i1 = jnp.min(jnp.where(lg == m1, lane, float(LANES)), axis=-1, keepdims=True)
    lg2 = jnp.where(lane == i1, -jnp.inf, lg)
    m2 = jnp.max(lg2, axis=-1, keepdims=True)
    i2 = jnp.min(jnp.where(lg2 == m2, lane, float(LANES)), axis=-1, keepdims=True)
    e = jnp.exp(m2 - m1)
    w1 = 1.0 / (1.0 + e)
    w2 = e * w1
    rt_ref[0] = jnp.where(lane == 0, i1, jnp.where(lane == 1, i2, jnp.where(lane == 2, w1,
                                                                          jnp.where(lane == 3, w2, 0.0))))


def _merge(xt, attn, four, mod_l, lw, n_lat, moe):
    tm = ROW_TILE
    with_ctx = attn[1] is not None
    n_lat_tiles = n_lat // tm
    n_out = n_lat + (attn[1].shape[1] if with_ctx else 0)
    if not with_ctx:
        attn, four = attn[0], four[0]
        xt = xt[0] if isinstance(xt, tuple) else xt
    bsz, d = (xt[0] if isinstance(xt, tuple) else xt).shape[0::2]
    tok = lambda n: pl.BlockSpec((1, tm, n), lambda b, i: (b, i, 0))
    na, nf = lw["ga"].shape[-1], lw["gf"].shape[-1]
    in_specs, args, n_srcs = [], [], []
    for src in (xt, attn, four):
        specs, arrs = _token_specs(src, tm, n_lat_tiles)
        in_specs += specs
        args += arrs
        n_srcs.append(len(arrs))
    in_specs += [_mod_spec(n_lat_tiles, mod_l.shape[-1]),
                 _const_spec((1, na)), _const_spec((1, nf)), _const_spec((na + nf, d)), _const_spec((1, d))]
    args += [mod_l, lw["ga"], lw["gf"], lw["w_out"], lw["g2"]]
    out_shape = [jax.ShapeDtypeStruct((bsz, n_out, d), F32), jax.ShapeDtypeStruct((bsz, n_out, d), BF16)]
    out_specs = [tok(d), tok(d)]
    if moe:
        slab, tiles = d // LANES, n_out // tm
        out_shape[1] = jax.ShapeDtypeStruct((bsz * n_out * slab, LANES), F32)
        out_specs[1] = pl.BlockSpec((tm * slab, LANES), lambda b, i: (b * tiles + i, 0))
        in_specs += [_const_spec((d, LANES)), _const_spec((d, LANES))]
        args += [lw["wr_hi"], lw["wr_lo"]]
        out_shape.append(jax.ShapeDtypeStruct((bsz, n_out, LANES), F32))
        out_specs.append(tok(LANES))
    return pl.pallas_call(
        functools.partial(_merge_kernel, moe=moe, n_lat_tiles=n_lat_tiles, n_srcs=tuple(n_srcs)),
        out_shape=tuple(out_shape),
        grid=(bsz, n_out // tm),
        in_specs=in_specs,
        out_specs=tuple(out_specs),
        compiler_params=_params(),
        name="merge_router" if moe else "merge",
    )(*args)


def _swiglu_chunks(h, wg_ref, wu_ref, wd_ref, chunk, lead=()):
    f = wg_ref.shape[-1]
    acc = None
    for c in range(f // chunk):
        sl = slice(c * chunk, (c + 1) * chunk)
        g = _dot(h, wg_ref[lead + (slice(None), sl)])
        u = _dot(h, wu_ref[lead + (slice(None), sl)])
        part = _dot((_silu(g) * u).astype(BF16), wd_ref[lead + (sl, slice(None))])
        acc = part if acc is None else acc + part
    return acc


def _ffn_kernel(x_ref, h_ref, mod_ref, wg_ref, wu_ref, wd_ref, o_ref):
    d = x_ref.shape[-1]
    gate2 = mod_ref[0][:, 5 * d:6 * d]
    y = _swiglu_chunks(h_ref[0], wg_ref, wu_ref, wd_ref, FFN_CHUNK)
    o_ref[0] = x_ref[0] + gate2 * y


def _dense_ffn(xt, h2, mod_l, lw, n_lat, n_out):
    bsz, _, d = xt.shape
    tm = ROW_TILE
    tok = lambda: pl.BlockSpec((1, tm, d), lambda b, i: (b, i, 0))
    return pl.pallas_call(
        _ffn_kernel,
        out_shape=jax.ShapeDtypeStruct((bsz, n_out, d), F32),
        grid=(bsz, n_out // tm),
        in_specs=[tok(), tok(), _mod_spec(n_lat // tm, mod_l.shape[-1]),
                  _const_spec(lw["w_gate"].shape), _const_spec(lw["w_up"].shape),
                  _const_spec(lw["w_down"].shape)],
        out_specs=tok(),
        compiler_params=_params(),
        name="dense_swiglu",
    )(xt, h2, mod_l, lw["w_gate"], lw["w_up"], lw["w_down"])


def _to_slabs(ref, val):
    rows, n = val.shape[0], val.shape[1] // LANES
    for s in range(n):
        ref[pl.ds(s, rows, stride=n), :] = val[:, s * LANES:(s + 1) * LANES]


def _from_slabs(ref, n):
    rows = ref.shape[0] // n
    return jnp.concatenate([ref[pl.ds(s, rows, stride=n), :] for s in range(n)], axis=1)


def _row_copies(n_rows, make_copy):
    def start(r, c):
        for j, cp in enumerate(make_copy(r)):
            cp.start(priority=j % 2)
        return c

    def wait(r, c):
        for cp in make_copy(r):
            cp.wait()
        return c

    lax.fori_loop(0, n_rows, start, 0, unroll=DMA_UNROLL)
    lax.fori_loop(0, n_rows, wait, 0, unroll=DMA_UNROLL)


def _dispatch_kernel(zb_ref, dest_ref, h_ref, xs_ref, zbuf, zsems, sems, *, slab):
    tm = h_ref.shape[0] // slab

    @pl.when(pl.program_id(0) == 0)
    def _():
        zbuf[...] = jnp.zeros(zbuf.shape, zbuf.dtype)
        rows = zbuf.shape[0]

        def clear(i):
            start = pl.multiple_of(zb_ref[0, 0, i] * rows, rows)
            return pltpu.make_async_copy(zbuf, xs_ref.at[pl.ds(start, rows), :], zsems.at[i])

        for i in range(zsems.shape[0]):
            pl.when(zb_ref[0, 0, i] >= 0)(lambda i=i: clear(i).start())
        for i in range(zsems.shape[0]):
            pl.when(zb_ref[0, 0, i] >= 0)(lambda i=i: clear(i).wait())

    def copies(r):
        src = h_ref.at[pl.ds(pl.multiple_of(r * slab, slab), slab), :]
        return [pltpu.make_async_copy(
            src, xs_ref.at[pl.ds(pl.multiple_of(dest_ref[0, 0, TOP_K * r + j] * slab, slab), slab), :],
            sems.at[j, r]) for j in range(TOP_K)]

    _row_copies(tm, copies)


def _dispatch(h2, dest, zero_blocks, n_rows, slab):
    tm = ROW_TILE
    n_tiles = h2.shape[0] // (tm * slab)
    nz = zero_blocks.shape[0]
    return pl.pallas_call(
        functools.partial(_dispatch_kernel, slab=slab),
        out_shape=jax.ShapeDtypeStruct((n_rows * slab, LANES), h2.dtype),
        grid=(n_tiles,),
        in_specs=[
            pl.BlockSpec((1, 1, nz), lambda n: (0, 0, 0), memory_space=pltpu.SMEM),
            pl.BlockSpec((1, 1, TOP_K * tm), lambda n: (n, 0, 0), memory_space=pltpu.SMEM),
            pl.BlockSpec((tm * slab, LANES), lambda n: (n, 0)),
        ],
        out_specs=pl.BlockSpec(memory_space=pl.ANY),
        scratch_shapes=[pltpu.VMEM((EXPERT_ROWS * slab, LANES), h2.dtype), pltpu.SemaphoreType.DMA((nz,)),
                        pltpu.SemaphoreType.DMA((TOP_K, tm))],
        compiler_params=_params(),
        name="moe_dispatch",
    )(zero_blocks.reshape(1, 1, nz), dest.reshape(n_tiles, 1, TOP_K * tm), h2)


def _expert_kernel(be_ref, nu_ref, x_ref, wg_ref, wu_ref, wd_ref, o_ref, *, slab):
    used = pl.program_id(0) < nu_ref[0]

    @pl.when(used)
    def _():
        x = _from_slabs(x_ref, slab).astype(BF16)
        _to_slabs(o_ref, _swiglu_chunks(x, wg_ref, wu_ref, wd_ref, MOE_CHUNK, lead=(0,)))

    @pl.when(jnp.logical_not(used))
    def _():
        o_ref[...] = jnp.zeros(o_ref.shape, o_ref.dtype)


def _expert_blocks(xs, block_exp, n_used, lw, slab):
    tm = EXPERT_ROWS
    d, f = lw["w_moe_gate"].shape[-2:]
    wspec = lambda shape: pl.BlockSpec((1,) + shape, lambda n, be, nu: (be[n], 0, 0),
                                       pipeline_mode=pl.Buffered(1))
    rows = pl.BlockSpec((tm * slab, LANES), lambda n, be, nu: (n, 0))
    return pl.pallas_call(
        functools.partial(_expert_kernel, slab=slab),
        out_shape=jax.ShapeDtypeStruct(xs.shape, F32),
        grid_spec=pltpu.PrefetchScalarGridSpec(
            num_scalar_prefetch=2,
            grid=(xs.shape[0] // (tm * slab),),
            in_specs=[rows, wspec((d, f)), wspec((d, f)), wspec((f, d))],
            out_specs=rows,
        ),
        compiler_params=_params(),
        name="moe_expert_blocks",
    )(block_exp, n_used, xs, lw["w_moe_gate"], lw["w_moe_up"], lw["w_moe_down"])


def _combine_kernel(pos_ref, ys_ref, x_ref, rt_ref, mod_ref, o_ref, buf, sems, *, slab):
    tm, d = x_ref.shape[1], x_ref.shape[2]

    def copies(r):
        return [pltpu.make_async_copy(
            ys_ref.at[pl.ds(pl.multiple_of(pos_ref[0, 0, TOP_K * r + j] * slab, slab), slab), :],
            buf.at[j, pl.ds(pl.multiple_of(r * slab, slab), slab), :], sems.at[j, r]) for j in range(TOP_K)]

    _row_copies(tm, copies)
    gate2 = mod_ref[0][:, 5 * d:6 * d]
    rt = rt_ref[0]
    y = rt[:, 2:3] * _from_slabs(buf.at[0], slab) + rt[:, 3:4] * _from_slabs(buf.at[1], slab)
    o_ref[0] = x_ref[0] + gate2 * y


def _combine(ys, pos, xt, rt, mod_l, n_lat, n_out, slab):
    bsz, stot, d = xt.shape
    tm = ROW_TILE
    tiles = stot // tm
    tok = lambda n: pl.BlockSpec((1, tm, n), lambda b, i: (b, i, 0))
    return pl.pallas_call(
        functools.partial(_combine_kernel, slab=slab),
        out_shape=jax.ShapeDtypeStruct((bsz, n_out, d), F32),
        grid=(bsz, n_out // tm),
        in_specs=[
            pl.BlockSpec((1, 1, TOP_K * tm), lambda b, i: (b * tiles + i, 0, 0), memory_space=pltpu.SMEM),
            pl.BlockSpec(memory_space=pl.ANY),
            tok(d), tok(LANES), _mod_spec(n_lat // tm, mod_l.shape[-1]),
        ],
        out_specs=tok(d),
        scratch_shapes=[pltpu.VMEM((TOP_K, tm * slab, LANES), F32), pltpu.SemaphoreType.DMA((TOP_K, tm))],
        compiler_params=_params(),
        name="moe_combine",
    )(pos.reshape(bsz * tiles, 1, TOP_K * tm), ys, xt, rt, mod_l)


def _moe_ffn(xt, h2, rt, mod_l, lw, n_lat, n_out):
    bsz, stot, d = xt.shape
    slab = d // LANES
    n_assign = bsz * stot * TOP_K
    tm = EXPERT_ROWS
    exp_flat = rt[..., :TOP_K].astype(jnp.int32).reshape(n_assign)
    onehot = (exp_flat[:, None] == jnp.arange(N_EXPERTS, dtype=jnp.int32)[None, :]).astype(jnp.int32)
    csum = jnp.cumsum(onehot, axis=0)
    rank = jnp.sum(jnp.where(onehot > 0, csum, 0), axis=1) - 1
    counts = csum[-1]
    padded = (counts + tm - 1) // tm * tm
    pad_ends = jnp.cumsum(padded)
    pad_starts = pad_ends - padded
    dest = (jnp.sum(onehot * pad_starts[None, :], axis=1) + rank).astype(jnp.int32)
    n_blocks = (n_assign + N_EXPERTS * (tm - 1) + tm - 1) // tm
    block_start = jnp.arange(n_blocks, dtype=jnp.int32) * tm
    block_exp = jnp.minimum(jnp.sum((pad_ends[None, :] <= block_start[:, None]).astype(jnp.int32), axis=1),
                            N_EXPERTS - 1).astype(jnp.int32)
    n_used = (pad_ends[-1:] // tm).astype(jnp.int32)
    last_blocks = jnp.where(padded > 0, pad_ends // tm - 1, -1)
    tail = n_used + jnp.arange(N_EXPERTS, dtype=jnp.int32)
    zero_blocks = jnp.concatenate([last_blocks, jnp.where(tail < n_blocks, tail, -1)]).astype(jnp.int32)

    xs = _dispatch(h2, dest, zero_blocks, n_blocks * tm, slab)
    ys = _expert_blocks(xs, block_exp, n_used, lw, slab)
    return _combine(ys, dest, xt, rt, mod_l, n_lat, n_out, slab)


def _pair_swap_index():
    j = jnp.arange(QK_ROPE_DIM)
    return jnp.where((j % 16) < 8, j + 8, j - 8)


def _head_lanes(v):
    pad = [(0, 0)] * (v.ndim - 1) + [(0, LANES - QK_HEAD_DIM)]
    return jnp.pad(v, pad)


def _swap_rope(v):
    rope = v[..., QK_NOPE_DIM:][..., _pair_swap_index()]
    return _head_lanes(jnp.concatenate([jnp.zeros_like(v[..., :QK_NOPE_DIM]), rope], axis=-1))


def _dft_angles(n_rows, n_cols, period):
    idx = (jnp.arange(n_rows, dtype=jnp.int32)[:, None] * jnp.arange(n_cols, dtype=jnp.int32)[None, :]) % period
    return idx.astype(F32) * (2.0 * math.pi / period)


def _tables(n_lat, n_ctx):
    nf = FOURIER_GROUPS * FOURIER_GROUP_DIM
    rows = n_lat // GRID_W
    r = jnp.repeat(jnp.arange(rows, dtype=F32), GRID_W)
    col = jnp.tile(jnp.arange(GRID_W, dtype=F32), rows)
    half = QK_ROPE_DIM // 2
    inv_freq = ROPE_THETA ** (-jnp.arange(0, half, 2, dtype=F32) / half)
    ar, ac = r[:, None] * inv_freq, col[:, None] * inv_freq
    ones = jnp.ones((n_lat, QK_NOPE_DIM), F32)
    cos = jnp.concatenate([ones, jnp.cos(ar), jnp.cos(ar), jnp.cos(ac), jnp.cos(ac)], axis=1)
    sin = jnp.concatenate([0 * ones, -jnp.sin(ar), jnp.sin(ar), -jnp.sin(ac), jnp.sin(ac)], axis=1)
    cos = jnp.concatenate([cos, jnp.ones((n_ctx, QK_HEAD_DIM), F32)], axis=0)
    sin = jnp.concatenate([sin, jnp.zeros((n_ctx, QK_HEAD_DIM), F32)], axis=0)
    lane = jnp.arange(2 * LANES)
    bd = (lane[:, None] // LANES == lane[None, :] // LANES).astype(BF16)
    n2 = FFT_N2
    n1 = n_lat // n2
    a1 = _dft_angles(n1, n1, n1)
    cs1 = jnp.concatenate([jnp.cos(a1), -jnp.sin(a1)], axis=0).astype(BF16)
    k = (jnp.arange(n1, dtype=jnp.int32)[:, None, None] + n1 * jnp.arange(n2, dtype=jnp.int32)[None, :, None])
    ang = ((k * jnp.arange(n2, dtype=jnp.int32)[None, None, :]) % n_lat).astype(F32) * (2.0 * math.pi / n_lat)
    c2, s2 = jnp.cos(ang), jnp.sin(ang)
    m2 = jnp.concatenate([jnp.concatenate([c2, s2], axis=2), jnp.concatenate([-s2, c2], axis=2)], axis=1)
    ach = _dft_angles(FOURIER_GROUP_DIM, FOURIER_GROUP_DIM, FOURIER_GROUP_DIM)
    eye = jnp.eye(FOURIER_GROUPS, dtype=F32)
    wch = jnp.concatenate([jnp.kron(eye, jnp.cos(ach)), jnp.kron(eye, jnp.sin(ach))], axis=0)
    actx = _dft_angles(n_ctx, n_ctx, n_ctx)
    return {
        "cos": _head_lanes(cos), "sin": _head_lanes(sin), "bd": bd,
        "qoff": jnp.zeros((1, LANES), F32).at[0, QK_HEAD_DIM].set(1.0),
        "voff": jnp.zeros((1, LANES), F32).at[0, V_HEAD_DIM].set(1.0),
        "cs1": cs1, "m2": m2.astype(BF16),
        "wch_lat": (wch * (n_lat * FOURIER_GROUP_DIM) ** -0.5).astype(BF16),
        "wch_ctx": (wch * (n_ctx * FOURIER_GROUP_DIM) ** -0.5).astype(BF16),
        "cs_ctx": jnp.concatenate([jnp.cos(actx), -jnp.sin(actx)], axis=0).astype(BF16),
    }


def _layer_weights(layer, p):
    q_rank = p["q_lora_norm"].shape[-1]
    kv_rank = p["kv_lora_norm"].shape[-1]
    o_pe = q_rank + kv_rank
    o_f = o_pe + QK_ROPE_DIM
    w_in = p["w_in"][layer]
    d = w_in.shape[0]
    w_pe = w_in[:, o_pe:o_f]
    z64 = jnp.zeros((d, QK_NOPE_DIM), F32)
    z32 = jnp.zeros((d, LANES - QK_HEAD_DIM), F32)
    w_in_cat = jnp.concatenate(
        [w_in[:, :o_pe], w_in[:, o_f:], z64, w_pe, z32, z64, w_pe[:, _pair_swap_index()], z32], axis=1)
    w_uq = p["w_uq"][layer].reshape(q_rank, N_HEADS, QK_HEAD_DIM)
    w_q = jnp.concatenate([_head_lanes(w_uq).reshape(q_rank, -1), _swap_rope(w_uq).reshape(q_rank, -1)], axis=1)
    w_ukv = p["w_ukv"][layer].reshape(kv_rank, N_HEADS, QK_NOPE_DIM + V_HEAD_DIM)
    w_k = jnp.pad(w_ukv[..., :QK_NOPE_DIM], ((0, 0), (0, 0), (0, LANES - QK_NOPE_DIM))).reshape(kv_rank, -1)
    w_v = jnp.pad(w_ukv[..., QK_NOPE_DIM:], ((0, 0), (0, 0), (0, LANES - V_HEAD_DIM))).reshape(kv_rank, -1)
    row = lambda v: v.reshape(1, -1)
    bound = LOG2E * (1.01 * QK_HEAD_DIM ** 0.5 * jnp.max(jnp.abs(p["q_norm"][layer]))
                     * jnp.max(jnp.abs(p["k_norm"][layer])) + 0.1)
    lw = {
        "g1": row(p["norm1"][layer]), "g2": row(p["norm2"][layer]),
        "w_in": w_in_cat.astype(BF16),
        "gq": row(p["q_lora_norm"][layer]), "gkv": row(p["kv_lora_norm"][layer]),
        "w_q": w_q.astype(BF16), "w_kv": jnp.concatenate([w_k, w_v], axis=1).astype(BF16),
        "bound": bound, "koff": jnp.zeros((1, LANES), F32).at[0, QK_HEAD_DIM].set(-bound),
        "qg": row(_head_lanes(p["q_norm"][layer])), "qg_sw": row(_swap_rope(p["q_norm"][layer])),
        "kg": row(_head_lanes(p["k_norm"][layer])), "kg_sw": row(_swap_rope(p["k_norm"][layer])),
        "ga": row(p["out_norm_attn"][layer]), "gf": row(p["out_norm_fourier"][layer]),
        "w_out": p["w_out"][layer].astype(BF16),
    }
    if layer % 2 == 0:
        lw.update(w_gate=p["w_ffn_gate"][layer // 2].astype(BF16), w_up=p["w_ffn_up"][layer // 2].astype(BF16),
                  w_down=p["w_ffn_down"][layer // 2].astype(BF16))
    else:
        wr = jnp.pad(p["w_router"][layer // 2], ((0, 0), (0, LANES - N_EXPERTS)))
        wr_hi, wr_lo = _split(wr)
        lw.update(wr_hi=wr_hi, wr_lo=wr_lo,
                  w_moe_gate=p["w_moe_gate"][layer // 2].astype(BF16),
                  w_moe_up=p["w_moe_up"][layer // 2].astype(BF16),
                  w_moe_down=p["w_moe_down"][layer // 2].astype(BF16))
    return lw


def kernel(x, c, ctx, c_ctx, w_ada, b_ada, norm1, w_in, q_lora_norm, kv_lora_norm, w_uq, w_ukv, q_norm, k_norm,
           out_norm_attn, out_norm_fourier, w_out, norm2, w_ffn_gate, w_ffn_up, w_ffn_down, w_router,
           w_moe_gate, w_moe_up, w_moe_down):
    params = dict(norm1=norm1, w_in=w_in, q_lora_norm=q_lora_norm, kv_lora_norm=kv_lora_norm, w_uq=w_uq,
                  w_ukv=w_ukv, q_norm=q_norm, k_norm=k_norm, out_norm_attn=out_norm_attn,
                  out_norm_fourier=out_norm_fourier, w_out=w_out, norm2=norm2, w_ffn_gate=w_ffn_gate,
                  w_ffn_up=w_ffn_up, w_ffn_down=w_ffn_down, w_router=w_router, w_moe_gate=w_moe_gate,
                  w_moe_up=w_moe_up, w_moe_down=w_moe_down)
    bsz, n_lat, d = x.shape
    n_ctx = ctx.shape[1]
    depth = w_ada.shape[0]
    assert n_lat % Q_TILE == 0 and n_lat % n_ctx == 0 and n_ctx % ROW_TILE == 0
    assert n_lat % FFT_N2 == 0 and n_lat % GRID_W == 0 and n_ctx % FFT_N2 == 0

    cond = jnp.concatenate([c, c_ctx[None, :], jnp.zeros((8 - bsz - 1, d), F32)], axis=0)
    mod = _modulation(cond, w_ada, b_ada)
    tabs = _tables(n_lat, n_ctx)
    xt = (x, ctx)

    for layer in range(depth):
        last = layer == depth - 1
        lw = _layer_weights(layer, params)
        mod_l = mod[layer].reshape(mod.shape[1], 1, mod.shape[2])
        q, k, v, f = _input_projection(xt, mod_l, lw, tabs, n_lat)
        attn = _attention(q, k, v, lw["bound"], n_lat, not last)
        four = _fourier_mix(f, tabs, n_lat, not last)
        moe = layer % 2 == 1
        n_out = n_lat if last else n_lat + n_ctx
        if moe:
            xt, h2, rt = _merge(xt, attn, four, mod_l, lw, n_lat, True)
            xt = _moe_ffn(xt, h2, rt, mod_l, lw, n_lat, n_out)
        else:
            xt, h2 = _merge(xt, attn, four, mod_l, lw, n_lat, False)
            xt = _dense_ffn(xt, h2, mod_l, lw, n_lat, n_out)
    return xt[:, :n_lat]
```

```python
import functools
import math

import jax
import jax.numpy as jnp
from jax import lax
from jax.experimental import pallas as pl
from jax.experimental.pallas import tpu as pltpu

F32 = jnp.float32
BF16 = jnp.bfloat16

N_HEADS = 8
QK_NOPE_DIM = 64
QK_ROPE_DIM = 32
QK_HEAD_DIM = QK_NOPE_DIM + QK_ROPE_DIM
V_HEAD_DIM = 64
GRID_W = 64
ROPE_THETA = 10000.0
FOURIER_GROUPS = 4
FOURIER_GROUP_DIM = 128
N_EXPERTS = 8
TOP_K = 2
EPS = 1e-6
LOG2E = 1.4426950408889634
MAX_DIRECT_BOUND = 50.0

LANES = 128
MXU_DIM = 256
VMEM_LIMIT_BYTES = 60 * 1024 * 1024

ROW_TILE = 256
INPROJ_ROWS = 256
Q_TILE = 1024
KV_CHUNKS = (1408, 640, 512, 256)
FFT_N2 = 128
FFT_K1_GROUP = 8
EXPERT_ROWS = 512
FFN_CHUNK = 1408
MOE_CHUNK = 512
DMA_UNROLL = 8


def _params(**kw):
    return pltpu.CompilerParams(vmem_limit_bytes=VMEM_LIMIT_BYTES, **kw)


def _dot(a, b):
    return jnp.dot(a, b, preferred_element_type=F32)


def _split(a):
    hi = a.astype(BF16)
    lo = (a - hi.astype(F32)).astype(BF16)
    return hi, lo


def _rms(x):
    return x * lax.rsqrt(jnp.mean(x * x, axis=-1, keepdims=True) + EPS)


def _silu(x):
    return x / (1.0 + jnp.exp(-x))


def _mod_kernel(c_ref, w_ref, b_ref, o_ref):
    chi, clo = _split(_silu(c_ref[...]))
    whi, wlo = _split(w_ref[0])
    o_ref[0] = _dot(chi, whi) + _dot(clo, whi) + _dot(chi, wlo) + b_ref[0]


def _modulation(cond, w_ada, b_ada):
    depth, d, n = w_ada.shape
    tn = n // 4
    return pl.pallas_call(
        _mod_kernel,
        out_shape=jax.ShapeDtypeStruct((depth, cond.shape[0], n), F32),
        grid=(depth, n // tn),
        in_specs=[
            pl.BlockSpec(cond.shape, lambda l, j: (0, 0)),
            pl.BlockSpec((1, d, tn), lambda l, j: (l, 0, j)),
            pl.BlockSpec((1, 1, tn), lambda l, j: (l, 0, j)),
        ],
        out_specs=pl.BlockSpec((1, cond.shape[0], tn), lambda l, j: (l, 0, j)),
        compiler_params=_params(),
        name="adaln_modulation",
    )(cond, w_ada, b_ada.reshape(depth, 1, n))


def _token_specs(src, tm, n_lat_tiles):
    if not isinstance(src, tuple):
        return [pl.BlockSpec((1, tm, src.shape[-1]), lambda b, i: (b, i, 0))], [src]
    n = src[0].shape[-1]
    lat = pl.BlockSpec((1, tm, n), lambda b, i: (b, jnp.minimum(i, n_lat_tiles - 1), 0))
    ctx = pl.BlockSpec((1, tm, n), lambda b, i: (b, jnp.maximum(i - n_lat_tiles, 0), 0))
    return [lat, ctx], list(src)


def _token_tile(refs, n_lat_tiles):
    if len(refs) == 1:
        return refs[0][0]
    return jnp.where(pl.program_id(1) >= n_lat_tiles, refs[1][0], refs[0][0])


def _inproj_kernel(*refs, n_src, n_lat_tiles):
    x = _token_tile(refs[:n_src], n_lat_tiles)
    (mod_ref, g1_ref, win_ref, gq_ref, gkv_ref, wq_ref, wkv_ref, bd_ref, qg_ref, qgs_ref, kg_ref, kgs_ref,
     qoff_ref, koff_ref, voff_ref, cos_ref, sin_ref, q_ref, k_ref, v_ref, f_ref) = refs[n_src:]
    d = x.shape[-1]
    m = mod_ref[0]
    shift, scale = m[:, 0:d], m[:, d:2 * d]
    g1 = g1_ref[...] * (1.0 + scale)
    nq = gq_ref.shape[-1]
    nkv = gkv_ref.shape[-1]
    nf = f_ref.shape[-1]
    o_f, o_pe = nq + nkv, nq + nkv + nf
    hw = N_HEADS * LANES
    voff = voff_ref[...]
    bd = bd_ref[...]

    def pair(t):
        return jnp.concatenate([t, t], axis=1)

    def head_rsqrt(raw):
        ss = _dot((raw * raw).astype(BF16), bd)
        return lax.rsqrt(ss * (1.0 / QK_HEAD_DIM) + EPS)

    q_scale = QK_HEAD_DIM ** -0.5 * LOG2E
    qoff, koff = pair(qoff_ref[...]), pair(koff_ref[...])

    for r0 in range(0, x.shape[0], INPROJ_ROWS):
        rows = slice(r0, r0 + INPROJ_ROWS)
        h = _rms(x[rows]) * g1 + shift
        p = _dot(h.astype(BF16), win_ref[...])
        f_ref[0, rows] = p[:, o_f:o_pe].astype(BF16)
        cq = (_rms(p[:, 0:nq]) * gq_ref[...]).astype(BF16)
        ckv = (_rms(p[:, nq:o_f]) * gkv_ref[...]).astype(BF16)
        qq = _dot(cq, wq_ref[...])
        kv = _dot(ckv, wkv_ref[...])
        for hd in range(N_HEADS):
            v_ref[0, hd, rows] = (kv[:, hw + hd * LANES:hw + (hd + 1) * LANES] + voff).astype(BF16)
        kpe2 = pair(p[:, o_pe:o_pe + LANES])
        kpe_sw2 = pair(p[:, o_pe + LANES:o_pe + 2 * LANES])
        cos, sin = cos_ref[rows], sin_ref[rows]
        q_c, q_s = pair(qg_ref[...] * cos), pair(qgs_ref[...] * sin)
        k_c, k_s = pair(kg_ref[...] * cos), pair(kgs_ref[...] * sin)
        for hp in range(N_HEADS // 2):
            lo_, hi_ = hp * 2 * LANES, (hp + 1) * 2 * LANES
            q_raw, q_sw = qq[:, lo_:hi_], qq[:, hw + lo_:hw + hi_]
            qo = (head_rsqrt(q_raw) * q_scale) * (q_raw * q_c + q_sw * q_s) + qoff
            q_ref[0, 2 * hp, rows] = qo[:, :LANES].astype(BF16)
            q_ref[0, 2 * hp + 1, rows] = qo[:, LANES:].astype(BF16)
            k_raw = kv[:, lo_:hi_] + kpe2
            ko = head_rsqrt(k_raw) * (k_raw * k_c + kpe_sw2 * k_s) + koff
            k_ref[0, 2 * hp, rows] = ko[:, :LANES].astype(BF16)
            k_ref[0, 2 * hp + 1, rows] = ko[:, LANES:].astype(BF16)


def _mod_spec(n_lat_tiles, n_mod):
    def index(b, i):
        return (jnp.where(i < n_lat_tiles, b, pl.num_programs(0)), 0, 0)
    return pl.BlockSpec((1, 1, n_mod), index)


def _const_spec(shape):
    zeros = (0,) * len(shape)
    return pl.BlockSpec(shape, lambda b, i: zeros)


def _input_projection(xt, mod_l, lw, tabs, n_lat):
    pair = isinstance(xt, tuple)
    bsz, d = (xt[0] if pair else xt).shape[0::2]
    stot = xt[0].shape[1] + xt[1].shape[1] if pair else xt.shape[1]
    tm = ROW_TILE
    nf = FOURIER_GROUPS * FOURIER_GROUP_DIM
    tok = lambda n: pl.BlockSpec((1, tm, n), lambda b, i: (b, i, 0))
    rope = pl.BlockSpec((tm, LANES), lambda b, i: (i, 0))
    head_out = pl.BlockSpec((1, N_HEADS, tm, LANES), lambda b, i: (b, 0, i, 0))
    x_specs, x_args = _token_specs(xt, tm, n_lat // tm)
    return pl.pallas_call(
        functools.partial(_inproj_kernel, n_src=len(x_args), n_lat_tiles=n_lat // tm),
        out_shape=(
            jax.ShapeDtypeStruct((bsz, N_HEADS, stot, LANES), BF16),
            jax.ShapeDtypeStruct((bsz, N_HEADS, stot, LANES), BF16),
            jax.ShapeDtypeStruct((bsz, N_HEADS, stot, LANES), BF16),
            jax.ShapeDtypeStruct((bsz, stot, nf), BF16),
        ),
        grid=(bsz, stot // tm),
        in_specs=x_specs + [
            _mod_spec(n_lat // tm, mod_l.shape[-1]), _const_spec((1, d)),
            _const_spec(lw["w_in"].shape), _const_spec(lw["gq"].shape), _const_spec(lw["gkv"].shape),
            _const_spec(lw["w_q"].shape), _const_spec(lw["w_kv"].shape), _const_spec(tabs["bd"].shape),
        ] + [_const_spec((1, LANES))] * 7 + [rope, rope],
        out_specs=(head_out, head_out, head_out, tok(nf)),
        compiler_params=_params(),
        name="input_projection",
    )(*x_args, mod_l, lw["g1"], lw["w_in"], lw["gq"], lw["gkv"], lw["w_q"], lw["w_kv"], tabs["bd"],
      lw["qg"], lw["qg_sw"], lw["kg"], lw["kg_sw"], tabs["qoff"], lw["koff"], tabs["voff"],
      tabs["cos"], tabs["sin"])


def _scores(q, k_ref, hh, start, tk):
    k = k_ref[0, hh, pl.ds(start, tk), :]
    return lax.dot_general(q, k, (((1,), (1,)), ((), ())), preferred_element_type=F32)


def _attn_finish(accs, o_ref):
    outs = [a[:, :V_HEAD_DIM] / a[:, V_HEAD_DIM:V_HEAD_DIM + 1] for a in accs]
    o_ref[0] = jnp.concatenate(outs, axis=1).astype(BF16)


def _attn_bounded_kernel(q_ref, k_ref, v_ref, o_ref, *, n_chunks, tk):
    tq = q_ref.shape[2]
    qs = [q_ref[0, hh] for hh in range(2)]

    def body(j, accs):
        start = pl.multiple_of(j * tk, tk)
        new = []
        for hh in range(2):
            p = jnp.exp2(_scores(qs[hh], k_ref, hh, start, tk)).astype(BF16)
            new.append(accs[hh] + _dot(p, v_ref[0, hh, pl.ds(start, tk), :]))
        return tuple(new)

    zero = jnp.zeros((tq, LANES), F32)
    unroll = 2 if n_chunks % 2 == 0 else 1
    _attn_finish(lax.fori_loop(0, n_chunks, body, (zero, zero), unroll=unroll), o_ref)


def _attn_online_kernel(q_ref, k_ref, v_ref, o_ref, *, n_chunks, tk):
    tq = q_ref.shape[2]
    qs = [q_ref[0, hh] for hh in range(2)]

    def body(j, carry):
        start = pl.multiple_of(j * tk, tk)
        new = []
        for hh in range(2):
            m, acc = carry[hh]
            s = _scores(qs[hh], k_ref, hh, start, tk)
            m_new = jnp.maximum(m, jnp.max(s, axis=-1, keepdims=True))
            p = jnp.exp2(s - m_new).astype(BF16)
            acc = jnp.exp2(m - m_new) * acc + _dot(p, v_ref[0, hh, pl.ds(start, tk), :])
            new.append((m_new, acc))
        return tuple(new)

    init = (jnp.full((tq, 1), -jnp.inf, F32), jnp.zeros((tq, LANES), F32))
    out = lax.fori_loop(0, n_chunks, body, (init, init))
    _attn_finish([out[0][1], out[1][1]], o_ref)


def _attention_calls(body, tag, q, k, v, n_lat, with_ctx):
    bsz, _, stot, _ = q.shape
    n_ctx = stot - n_lat
    hp = N_HEADS // 2
    tk = next(c for c in KV_CHUNKS if stot % c == 0)
    tq = Q_TILE
    nv = N_HEADS * V_HEAD_DIM
    kv_all = pl.BlockSpec((1, 2, stot, LANES), lambda b, h, i: (b, h, 0, 0))
    attn_lat = pl.pallas_call(
        functools.partial(body, n_chunks=stot // tk, tk=tk),
        out_shape=jax.ShapeDtypeStruct((bsz, n_lat, nv), BF16),
        grid=(bsz, hp, n_lat // tq),
        in_specs=[pl.BlockSpec((1, 2, tq, LANES), lambda b, h, i: (b, h, i, 0)), kv_all, kv_all],
        out_specs=pl.BlockSpec((1, tq, LANES), lambda b, h, i: (b, i, h)),
        compiler_params=_params(),
        name="latent_attention" + tag,
    )(q, k, v)
    if not with_ctx:
        return attn_lat
    cblk = n_lat // n_ctx
    ctx_rows = pl.BlockSpec((1, 2, n_ctx, LANES), lambda b, h: (b, h, cblk, 0))
    attn_ctx = pl.pallas_call(
        functools.partial(body, n_chunks=1, tk=n_ctx),
        out_shape=jax.ShapeDtypeStruct((bsz, n_ctx, nv), BF16),
        grid=(bsz, hp),
        in_specs=[ctx_rows, ctx_rows, ctx_rows],
        out_specs=pl.BlockSpec((1, n_ctx, LANES), lambda b, h: (b, 0, h)),
        compiler_params=_params(),
        name="context_attention" + tag,
    )(q, k, v)
    return attn_lat, attn_ctx


def _attention(q, k, v, bound, n_lat, with_ctx):
    out = lax.cond(
        bound <= MAX_DIRECT_BOUND,
        lambda: _attention_calls(_attn_bounded_kernel, "", q, k, v, n_lat, with_ctx),
        lambda: _attention_calls(_attn_online_kernel, "_online", q, k, v, n_lat, with_ctx))
    return out if with_ctx else (out, None)


def _fft1_kernel(cs_ref, x_ref, a_ref):
    a_ref[0] = _dot(cs_ref[...], x_ref[0]).astype(BF16)


def _fft2_kernel(ar_ref, ai_ref, m2_ref, wch_ref, o_ref):
    nf = wch_ref.shape[-1]
    n2 = ar_ref.shape[2]
    for j in range(ar_ref.shape[1]):
        slab = jnp.concatenate([ar_ref[0, j], ai_ref[0, j]], axis=0)
        z = _dot(m2_ref[j], slab)
        zc = jnp.concatenate([z[:n2], z[n2:]], axis=1).astype(BF16)
        o_ref[0, :, j * nf:(j + 1) * nf] = _dot(zc, wch_ref[...]).astype(BF16)


def _fft_ctx_kernel(cs_ref, x_ref, wch_ref, o_ref):
    n = x_ref.shape[1]
    z = _dot(cs_ref[...], x_ref[0])
    zc = jnp.concatenate([z[:n], z[n:]], axis=1).astype(BF16)
    o_ref[0] = _dot(zc, wch_ref[...]).astype(BF16)


def _fourier_mix(f, tabs, n_lat, with_ctx):
    bsz, stot, nf = f.shape
    n_ctx = stot - n_lat
    n2 = FFT_N2
    n1 = n_lat // n2
    kg = min(FFT_K1_GROUP, n1)
    tc = min(n2 * nf, 8192)
    a = pl.pallas_call(
        _fft1_kernel,
        out_shape=jax.ShapeDtypeStruct((bsz, 2 * n1, n2 * nf), BF16),
        grid=(bsz, n2 * nf // tc),
        in_specs=[
            pl.BlockSpec((2 * n1, n1), lambda b, j: (0, 0)),
            pl.BlockSpec((1, n1, tc), lambda b, j: (b, 0, j)),
        ],
        out_specs=pl.BlockSpec((1, 2 * n1, tc), lambda b, j: (b, 0, j)),
        compiler_params=_params(),
        name="fourier_stage1",
    )(tabs["cs1"], f.reshape(bsz, stot // n2, n2 * nf))
    a4 = a.reshape(bsz, 2 * n1, n2, nf)
    four_lat = pl.pallas_call(
        _fft2_kernel,
        out_shape=jax.ShapeDtypeStruct((bsz, n2, n1 * nf), BF16),
        grid=(bsz, n1 // kg),
        in_specs=[
            pl.BlockSpec((1, kg, n2, nf), lambda b, g: (b, g, 0, 0)),
            pl.BlockSpec((1, kg, n2, nf), lambda b, g: (b, n1 // kg + g, 0, 0)),
            pl.BlockSpec((kg, 2 * n2, 2 * n2), lambda b, g: (g, 0, 0)),
            pl.BlockSpec((2 * nf, nf), lambda b, g: (0, 0)),
        ],
        out_specs=pl.BlockSpec((1, n2, kg * nf), lambda b, g: (b, 0, g)),
        compiler_params=_params(),
        name="fourier_stage2",
    )(a4, a4, tabs["m2"], tabs["wch_lat"]).reshape(bsz, n_lat, nf)
    if not with_ctx:
        return four_lat, None
    cblk = n_lat // n_ctx
    four_ctx = pl.pallas_call(
        _fft_ctx_kernel,
        out_shape=jax.ShapeDtypeStruct((bsz, n_ctx, nf), BF16),
        grid=(bsz,),
        in_specs=[
            pl.BlockSpec((2 * n_ctx, n_ctx), lambda b: (0, 0)),
            pl.BlockSpec((1, n_ctx, nf), lambda b: (b, cblk, 0)),
            pl.BlockSpec((2 * nf, nf), lambda b: (0, 0)),
        ],
        out_specs=pl.BlockSpec((1, n_ctx, nf), lambda b: (b, 0, 0)),
        compiler_params=_params(),
        name="fourier_context",
    )(tabs["cs_ctx"], f, tabs["wch_ctx"])
    return four_lat, four_ctx


def _merge_kernel(*refs, moe, n_lat_tiles, n_srcs):
    tiles = []
    for n in n_srcs:
        tiles.append(_token_tile(refs[:n], n_lat_tiles))
        refs = refs[n:]
    x, a, f = tiles
    mod_ref, ga_ref, gf_ref, wout_ref, g2_ref = refs[:5]
    rest = refs[5:]
    d = x.shape[-1]
    m = mod_ref[0]
    gate1, shift2, scale2 = m[:, 2 * d:3 * d], m[:, 3 * d:4 * d], m[:, 4 * d:5 * d]
    an = _rms(a.astype(F32)) * ga_ref[...]
    fn = _rms(f.astype(F32)) * gf_ref[...]
    y = _dot(jnp.concatenate([an, fn], axis=1).astype(BF16), wout_ref[...])
    xn = x + gate1 * y
    h2 = _rms(xn) * (g2_ref[...] * (1.0 + scale2)) + shift2
    if not moe:
        xo_ref, h_ref = rest
        xo_ref[0] = xn
        h_ref[0] = h2.astype(BF16)
        return
    wr_hi_ref, wr_lo_ref, xo_ref, h_ref, rt_ref = rest
    xo_ref[0] = xn
    _to_slabs(h_ref, h2)
    hi, lo = _split(h2)
    logits = _dot(hi, wr_hi_ref[...]) + _dot(lo, wr_hi_ref[...]) + _dot(hi, wr_lo_ref[...])
    lane = lax.broadcasted_iota(jnp.int32, logits.shape, 1).astype(F32)
    lg = jnp.where(lane < N_EXPERTS, logits, -jnp.inf)
    m1 = jnp.max(lg, axis=-1, keepdims=True)
    i1 = jnp.min(jnp.where(lg == m1, lane, float(LANES)), axis=-1, keepdims=True)
    lg2 = jnp.where(lane == i1, -jnp.inf, lg)
    m2 = jnp.max(lg2, axis=-1, keepdims=True)
    i2 = jnp.min(jnp.where(lg2 == m2, lane, float(LANES)), axis=-1, keepdims=True)
    e = jnp.exp(m2 - m1)
    w1 = 1.0 / (1.0 + e)
    w2 = e * w1
    rt_ref[0] = jnp.where(lane == 0, i1, jnp.where(lane == 1, i2, jnp.where(lane == 2, w1,
                                                                          jnp.where(lane == 3, w2, 0.0))))


def _merge(xt, attn, four, mod_l, lw, n_lat, moe):
    tm = ROW_TILE
    with_ctx = attn[1] is not None
    n_lat_tiles = n_lat // tm
    n_out = n_lat + (attn[1].shape[1] if with_ctx else 0)
    if not with_ctx:
        attn, four = attn[0], four[0]
        xt = xt[0] if isinstance(xt, tuple) else xt
    bsz, d = (xt[0] if isinstance(xt, tuple) else xt).shape[0::2]
    tok = lambda n: pl.BlockSpec((1, tm, n), lambda b, i: (b, i, 0))
    na, nf = lw["ga"].shape[-1], lw["gf"].shape[-1]
    in_specs, args, n_srcs = [], [], []
    for src in (xt, attn, four):
        specs, arrs = _token_specs(src, tm, n_lat_tiles)
        in_specs += specs
        args += arrs
        n_srcs.append(len(arrs))
    in_specs += [_mod_spec(n_lat_tiles, mod_l.shape[-1]),
                 _const_spec((1, na)), _const_spec((1, nf)), _const_spec((na + nf, d)), _const_spec((1, d))]
    args += [mod_l, lw["ga"], lw["gf"], lw["w_out"], lw["g2"]]
    out_shape = [jax.ShapeDtypeStruct((bsz, n_out, d), F32), jax.ShapeDtypeStruct((bsz, n_out, d), BF16)]
    out_specs = [tok(d), tok(d)]
    if moe:
        slab, tiles = d // LANES, n_out // tm
        out_shape[1] = jax.ShapeDtypeStruct((bsz * n_out * slab, LANES), F32)
        out_specs[1] = pl.BlockSpec((tm * slab, LANES), lambda b, i: (b * tiles + i, 0))
        in_specs += [_const_spec((d, LANES)), _const_spec((d, LANES))]
        args += [lw["wr_hi"], lw["wr_lo"]]
        out_shape.append(jax.ShapeDtypeStruct((bsz, n_out, LANES), F32))
        out_specs.append(tok(LANES))
    return pl.pallas_call(
        functools.partial(_merge_kernel, moe=moe, n_lat_tiles=n_lat_tiles, n_srcs=tuple(n_srcs)),
        out_shape=tuple(out_shape),
        grid=(bsz, n_out // tm),
        in_specs=in_specs,
        out_specs=tuple(out_specs),
        compiler_params=_params(),
        name="merge_router" if moe else "merge",
    )(*args)


def _swiglu_chunks(h, wg_ref, wu_ref, wd_ref, chunk, lead=()):
    f = wg_ref.shape[-1]
    acc = None
    for c in range(f // chunk):
        sl = slice(c * chunk, (c + 1) * chunk)
        g = _dot(h, wg_ref[lead + (slice(None), sl)])
        u = _dot(h, wu_ref[lead + (slice(None), sl)])
        part = _dot((_silu(g) * u).astype(BF16), wd_ref[lead + (sl, slice(None))])
        acc = part if acc is None else acc + part
    return acc


def _ffn_kernel(x_ref, h_ref, mod_ref, wg_ref, wu_ref, wd_ref, o_ref):
    d = x_ref.shape[-1]
    gate2 = mod_ref[0][:, 5 * d:6 * d]
    y = _swiglu_chunks(h_ref[0], wg_ref, wu_ref, wd_ref, FFN_CHUNK)
    o_ref[0] = x_ref[0] + gate2 * y


def _dense_ffn(xt, h2, mod_l, lw, n_lat, n_out):
    bsz, _, d = xt.shape
    tm = ROW_TILE
    tok = lambda: pl.BlockSpec((1, tm, d), lambda b, i: (b, i, 0))
    return pl.pallas_call(
        _ffn_kernel,
        out_shape=jax.ShapeDtypeStruct((bsz, n_out, d), F32),
        grid=(bsz, n_out // tm),
        in_specs=[tok(), tok(), _mod_spec(n_lat // tm, mod_l.shape[-1]),
                  _const_spec(lw["w_gate"].shape), _const_spec(lw["w_up"].shape),
                  _const_spec(lw["w_down"].shape)],
        out_specs=tok(),
        compiler_params=_params(),
        name="dense_swiglu",
    )(xt, h2, mod_l, lw["w_gate"], lw["w_up"], lw["w_down"])


def _to_slabs(ref, val):
    rows, n = val.shape[0], val.shape[1] // LANES
    for s in range(n):
        ref[pl.ds(s, rows, stride=n), :] = val[:, s * LANES:(s + 1) * LANES]


def _from_slabs(ref, n):
    rows = ref.shape[0] // n
    return jnp.concatenate([ref[pl.ds(s, rows, stride=n), :] for s in range(n)], axis=1)


def _row_copies(n_rows, make_copy):
    def start(r, c):
        for j, cp in enumerate(make_copy(r)):
            cp.start(priority=j % 2)
        return c

    def wait(r, c):
        for cp in make_copy(r):
            cp.wait()
        return c

    lax.fori_loop(0, n_rows, start, 0, unroll=DMA_UNROLL)
    lax.fori_loop(0, n_rows, wait, 0, unroll=DMA_UNROLL)


def _dispatch_kernel(zb_ref, dest_ref, h_ref, xs_ref, zbuf, zsems, sems, *, slab):
    tm = h_ref.shape[0] // slab

    @pl.when(pl.program_id(0) == 0)
    def _():
        zbuf[...] = jnp.zeros(zbuf.shape, zbuf.dtype)
        rows = zbuf.shape[0]

        def clear(i):
            start = pl.multiple_of(zb_ref[0, 0, i] * rows, rows)
            return pltpu.make_async_copy(zbuf, xs_ref.at[pl.ds(start, rows), :], zsems.at[i])

        for i in range(zsems.shape[0]):
            pl.when(zb_ref[0, 0, i] >= 0)(lambda i=i: clear(i).start())
        for i in range(zsems.shape[0]):
            pl.when(zb_ref[0, 0, i] >= 0)(lambda i=i: clear(i).wait())

    def copies(r):
        src = h_ref.at[pl.ds(pl.multiple_of(r * slab, slab), slab), :]
        return [pltpu.make_async_copy(
            src, xs_ref.at[pl.ds(pl.multiple_of(dest_ref[0, 0, TOP_K * r + j] * slab, slab), slab), :],
            sems.at[j, r]) for j in range(TOP_K)]

    _row_copies(tm, copies)


def _dispatch(h2, dest, zero_blocks, n_rows, slab):
    tm = ROW_TILE
    n_tiles = h2.shape[0] // (tm * slab)
    nz = zero_blocks.shape[0]
    return pl.pallas_call(
        functools.partial(_dispatch_kernel, slab=slab),
        out_shape=jax.ShapeDtypeStruct((n_rows * slab, LANES), h2.dtype),
        grid=(n_tiles,),
        in_specs=[
            pl.BlockSpec((1, 1, nz), lambda n: (0, 0, 0), memory_space=pltpu.SMEM),
            pl.BlockSpec((1, 1, TOP_K * tm), lambda n: (n, 0, 0), memory_space=pltpu.SMEM),
            pl.BlockSpec((tm * slab, LANES), lambda n: (n, 0)),
        ],
        out_specs=pl.BlockSpec(memory_space=pl.ANY),
        scratch_shapes=[pltpu.VMEM((EXPERT_ROWS * slab, LANES), h2.dtype), pltpu.SemaphoreType.DMA((nz,)),
                        pltpu.SemaphoreType.DMA((TOP_K, tm))],
        compiler_params=_params(),
        name="moe_dispatch",
    )(zero_blocks.reshape(1, 1, nz), dest.reshape(n_tiles, 1, TOP_K * tm), h2)


def _expert_kernel(be_ref, nu_ref, x_ref, wg_ref, wu_ref, wd_ref, o_ref, *, slab):
    used = pl.program_id(0) < nu_ref[0]

    @pl.when(used)
    def _():
        x = _from_slabs(x_ref, slab).astype(BF16)
        _to_slabs(o_ref, _swiglu_chunks(x, wg_ref, wu_ref, wd_ref, MOE_CHUNK, lead=(0, 0)))

    @pl.when(jnp.logical_not(used))
    def _():
        o_ref[...] = jnp.zeros(o_ref.shape, o_ref.dtype)


def _expert_blocks(xs, block_exp, n_used, lw, slab):
    tm = EXPERT_ROWS
    d, f = lw["w_moe_gate"].shape[-2:]
    li = lw["moe_index"]
    wspec = lambda shape: pl.BlockSpec((1, 1) + shape, lambda n, be, nu: (li, be[n], 0, 0),
                                       pipeline_mode=pl.Buffered(1))
    rows = pl.BlockSpec((tm * slab, LANES), lambda n, be, nu: (n, 0))
    return pl.pallas_call(
        functools.partial(_expert_kernel, slab=slab),
        out_shape=jax.ShapeDtypeStruct(xs.shape, F32),
        grid_spec=pltpu.PrefetchScalarGridSpec(
            num_scalar_prefetch=2,
            grid=(xs.shape[0] // (tm * slab),),
            in_specs=[rows, wspec((d, f)), wspec((d, f)), wspec((f, d))],
            out_specs=rows,
        ),
        compiler_params=_params(),
        name="moe_expert_blocks",
    )(block_exp, n_used, xs, lw["w_moe_gate"], lw["w_moe_up"], lw["w_moe_down"])


def _combine_kernel(pos_ref, ys_ref, x_ref, rt_ref, mod_ref, o_ref, buf, sems, *, slab):
    tm, d = x_ref.shape[1], x_ref.shape[2]

    def copies(r):
        return [pltpu.make_async_copy(
            ys_ref.at[pl.ds(pl.multiple_of(pos_ref[0, 0, TOP_K * r + j] * slab, slab), slab), :],
            buf.at[j, pl.ds(pl.multiple_of(r * slab, slab), slab), :], sems.at[j, r]) for j in range(TOP_K)]

    _row_copies(tm, copies)
    gate2 = mod_ref[0][:, 5 * d:6 * d]
    rt = rt_ref[0]
    y = rt[:, 2:3] * _from_slabs(buf.at[0], slab) + rt[:, 3:4] * _from_slabs(buf.at[1], slab)
    o_ref[0] = x_ref[0] + gate2 * y


def _combine(ys, pos, xt, rt, mod_l, n_lat, n_out, slab):
    bsz, stot, d = xt.shape
    tm = ROW_TILE
    tiles = stot // tm
    tok = lambda n: pl.BlockSpec((1, tm, n), lambda b, i: (b, i, 0))
    return pl.pallas_call(
        functools.partial(_combine_kernel, slab=slab),
        out_shape=jax.ShapeDtypeStruct((bsz, n_out, d), F32),
        grid=(bsz, n_out // tm),
        in_specs=[
            pl.BlockSpec((1, 1, TOP_K * tm), lambda b, i: (b * tiles + i, 0, 0), memory_space=pltpu.SMEM),
            pl.BlockSpec(memory_space=pl.ANY),
            tok(d), tok(LANES), _mod_spec(n_lat // tm, mod_l.shape[-1]),
        ],
        out_specs=tok(d),
        scratch_shapes=[pltpu.VMEM((TOP_K, tm * slab, LANES), F32), pltpu.SemaphoreType.DMA((TOP_K, tm))],
        compiler_params=_params(),
        name="moe_combine",
    )(pos.reshape(bsz * tiles, 1, TOP_K * tm), ys, xt, rt, mod_l)


def _moe_ffn(xt, h2, rt, mod_l, lw, n_lat, n_out):
    bsz, stot, d = xt.shape
    slab = d // LANES
    n_assign = bsz * stot * TOP_K
    tm = EXPERT_ROWS
    exp_flat = rt[..., :TOP_K].astype(jnp.int32).reshape(n_assign)
    onehot = (exp_flat[:, None] == jnp.arange(N_EXPERTS, dtype=jnp.int32)[None, :]).astype(jnp.int32)
    csum = jnp.cumsum(onehot, axis=0)
    rank = jnp.sum(jnp.where(onehot > 0, csum, 0), axis=1) - 1
    counts = csum[-1]
    padded = (counts + tm - 1) // tm * tm
    pad_ends = jnp.cumsum(padded)
    pad_starts = pad_ends - padded
    dest = (jnp.sum(onehot * pad_starts[None, :], axis=1) + rank).astype(jnp.int32)
    n_blocks = (n_assign + N_EXPERTS * (tm - 1) + tm - 1) // tm
    block_start = jnp.arange(n_blocks, dtype=jnp.int32) * tm
    block_exp = jnp.minimum(jnp.sum((pad_ends[None, :] <= block_start[:, None]).astype(jnp.int32), axis=1),
                            N_EXPERTS - 1).astype(jnp.int32)
    n_used = (pad_ends[-1:] // tm).astype(jnp.int32)
    last_blocks = jnp.where(padded > 0, pad_ends // tm - 1, -1)
    tail = n_used + jnp.arange(N_EXPERTS, dtype=jnp.int32)
    zero_blocks = jnp.concatenate([last_blocks, jnp.where(tail < n_blocks, tail, -1)]).astype(jnp.int32)

    xs = _dispatch(h2, dest, zero_blocks, n_blocks * tm, slab)
    ys = _expert_blocks(xs, block_exp, n_used, lw, slab)
    return _combine(ys, dest, xt, rt, mod_l, n_lat, n_out, slab)


def _pair_swap_index():
    j = jnp.arange(QK_ROPE_DIM)
    return jnp.where((j % 16) < 8, j + 8, j - 8)


def _head_lanes(v):
    pad = [(0, 0)] * (v.ndim - 1) + [(0, LANES - QK_HEAD_DIM)]
    return jnp.pad(v, pad)


def _swap_rope(v):
    rope = v[..., QK_NOPE_DIM:][..., _pair_swap_index()]
    return _head_lanes(jnp.concatenate([jnp.zeros_like(v[..., :QK_NOPE_DIM]), rope], axis=-1))


def _dft_angles(n_rows, n_cols, period):
    idx = (jnp.arange(n_rows, dtype=jnp.int32)[:, None] * jnp.arange(n_cols, dtype=jnp.int32)[None, :]) % period
    return idx.astype(F32) * (2.0 * math.pi / period)


def _tables(n_lat, n_ctx):
    nf = FOURIER_GROUPS * FOURIER_GROUP_DIM
    rows = n_lat // GRID_W
    r = jnp.repeat(jnp.arange(rows, dtype=F32), GRID_W)
    col = jnp.tile(jnp.arange(GRID_W, dtype=F32), rows)
    half = QK_ROPE_DIM // 2
    inv_freq = ROPE_THETA ** (-jnp.arange(0, half, 2, dtype=F32) / half)
    ar, ac = r[:, None] * inv_freq, col[:, None] * inv_freq
    ones = jnp.ones((n_lat, QK_NOPE_DIM), F32)
    cos = jnp.concatenate([ones, jnp.cos(ar), jnp.cos(ar), jnp.cos(ac), jnp.cos(ac)], axis=1)
    sin = jnp.concatenate([0 * ones, -jnp.sin(ar), jnp.sin(ar), -jnp.sin(ac), jnp.sin(ac)], axis=1)
    cos = jnp.concatenate([cos, jnp.ones((n_ctx, QK_HEAD_DIM), F32)], axis=0)
    sin = jnp.concatenate([sin, jnp.zeros((n_ctx, QK_HEAD_DIM), F32)], axis=0)
    lane = jnp.arange(2 * LANES)
    bd = (lane[:, None] // LANES == lane[None, :] // LANES).astype(BF16)
    n2 = FFT_N2
    n1 = n_lat // n2
    a1 = _dft_angles(n1, n1, n1)
    cs1 = jnp.concatenate([jnp.cos(a1), -jnp.sin(a1)], axis=0).astype(BF16)
    k = (jnp.arange(n1, dtype=jnp.int32)[:, None, None] + n1 * jnp.arange(n2, dtype=jnp.int32)[None, :, None])
    ang = ((k * jnp.arange(n2, dtype=jnp.int32)[None, None, :]) % n_lat).astype(F32) * (2.0 * math.pi / n_lat)
    c2, s2 = jnp.cos(ang), jnp.sin(ang)
    m2 = jnp.concatenate([jnp.concatenate([c2, s2], axis=2), jnp.concatenate([-s2, c2], axis=2)], axis=1)
    ach = _dft_angles(FOURIER_GROUP_DIM, FOURIER_GROUP_DIM, FOURIER_GROUP_DIM)
    eye = jnp.eye(FOURIER_GROUPS, dtype=F32)
    wch = jnp.concatenate([jnp.kron(eye, jnp.cos(ach)), jnp.kron(eye, jnp.sin(ach))], axis=0)
    actx = _dft_angles(n_ctx, n_ctx, n_ctx)
    return {
        "cos": _head_lanes(cos), "sin": _head_lanes(sin), "bd": bd,
        "qoff": jnp.zeros((1, LANES), F32).at[0, QK_HEAD_DIM].set(1.0),
        "voff": jnp.zeros((1, LANES), F32).at[0, V_HEAD_DIM].set(1.0),
        "cs1": cs1, "m2": m2.astype(BF16),
        "wch_lat": (wch * (n_lat * FOURIER_GROUP_DIM) ** -0.5).astype(BF16),
        "wch_ctx": (wch * (n_ctx * FOURIER_GROUP_DIM) ** -0.5).astype(BF16),
        "cs_ctx": jnp.concatenate([jnp.cos(actx), -jnp.sin(actx)], axis=0).astype(BF16),
    }


def _layer_weights(layer, p):
    q_rank = p["q_lora_norm"].shape[-1]
    kv_rank = p["kv_lora_norm"].shape[-1]
    o_pe = q_rank + kv_rank
    o_f = o_pe + QK_ROPE_DIM
    w_in = p["w_in"][layer]
    d = w_in.shape[0]
    w_pe = w_in[:, o_pe:o_f]
    z64 = jnp.zeros((d, QK_NOPE_DIM), F32)
    z32 = jnp.zeros((d, LANES - QK_HEAD_DIM), F32)
    w_in_cat = jnp.concatenate(
        [w_in[:, :o_pe], w_in[:, o_f:], z64, w_pe, z32, z64, w_pe[:, _pair_swap_index()], z32], axis=1)
    w_uq = p["w_uq"][layer].reshape(q_rank, N_HEADS, QK_HEAD_DIM)
    w_q = jnp.concatenate([_head_lanes(w_uq).reshape(q_rank, -1), _swap_rope(w_uq).reshape(q_rank, -1)], axis=1)
    w_ukv = p["w_ukv"][layer].reshape(kv_rank, N_HEADS, QK_NOPE_DIM + V_HEAD_DIM)
    w_k = jnp.pad(w_ukv[..., :QK_NOPE_DIM], ((0, 0), (0, 0), (0, LANES - QK_NOPE_DIM))).reshape(kv_rank, -1)
    w_v = jnp.pad(w_ukv[..., QK_NOPE_DIM:], ((0, 0), (0, 0), (0, LANES - V_HEAD_DIM))).reshape(kv_rank, -1)
    row = lambda v: v.reshape(1, -1)
    bound = LOG2E * (1.01 * QK_HEAD_DIM ** 0.5 * jnp.max(jnp.abs(p["q_norm"][layer]))
                     * jnp.max(jnp.abs(p["k_norm"][layer])) + 0.1)
    lw = {
        "g1": row(p["norm1"][layer]), "g2": row(p["norm2"][layer]),
        "w_in": w_in_cat.astype(BF16),
        "gq": row(p["q_lora_norm"][layer]), "gkv": row(p["kv_lora_norm"][layer]),
        "w_q": w_q.astype(BF16), "w_kv": jnp.concatenate([w_k, w_v], axis=1).astype(BF16),
        "bound": bound, "koff": jnp.zeros((1, LANES), F32).at[0, QK_HEAD_DIM].set(-bound),
        "qg": row(_head_lanes(p["q_norm"][layer])), "qg_sw": row(_swap_rope(p["q_norm"][layer])),
        "kg": row(_head_lanes(p["k_norm"][layer])), "kg_sw": row(_swap_rope(p["k_norm"][layer])),
        "ga": row(p["out_norm_attn"][layer]), "gf": row(p["out_norm_fourier"][layer]),
        "w_out": p["w_out"][layer].astype(BF16),
    }
    if layer % 2 == 0:
        lw.update(w_gate=p["w_ffn_gate"][layer // 2].astype(BF16), w_up=p["w_ffn_up"][layer // 2].astype(BF16),
                  w_down=p["w_ffn_down"][layer // 2].astype(BF16))
    else:
        wr = jnp.pad(p["w_router"][layer // 2], ((0, 0), (0, LANES - N_EXPERTS)))
        wr_hi, wr_lo = _split(wr)
        lw.update(wr_hi=wr_hi, wr_lo=wr_lo,
                  moe_index=layer // 2, w_moe_gate=p["w_moe_gate_bf16"], w_moe_up=p["w_moe_up_bf16"],
                  w_moe_down=p["w_moe_down_bf16"])
    return lw


def kernel(x, c, ctx, c_ctx, w_ada, b_ada, norm1, w_in, q_lora_norm, kv_lora_norm, w_uq, w_ukv, q_norm, k_norm,
           out_norm_attn, out_norm_fourier, w_out, norm2, w_ffn_gate, w_ffn_up, w_ffn_down, w_router,
           w_moe_gate, w_moe_up, w_moe_down):
    params = dict(norm1=norm1, w_in=w_in, q_lora_norm=q_lora_norm, kv_lora_norm=kv_lora_norm, w_uq=w_uq,
                  w_ukv=w_ukv, q_norm=q_norm, k_norm=k_norm, out_norm_attn=out_norm_attn,
                  out_norm_fourier=out_norm_fourier, w_out=w_out, norm2=norm2, w_ffn_gate=w_ffn_gate,
                  w_ffn_up=w_ffn_up, w_ffn_down=w_ffn_down, w_router=w_router, w_moe_gate=w_moe_gate,
                  w_moe_up=w_moe_up, w_moe_down=w_moe_down,
                  w_moe_gate_bf16=w_moe_gate.astype(BF16), w_moe_up_bf16=w_moe_up.astype(BF16),
                  w_moe_down_bf16=w_moe_down.astype(BF16))
    bsz, n_lat, d = x.shape
    n_ctx = ctx.shape[1]
    depth = w_ada.shape[0]
    assert n_lat % Q_TILE == 0 and n_lat % n_ctx == 0 and n_ctx % ROW_TILE == 0
    assert n_lat % FFT_N2 == 0 and n_lat % GRID_W == 0 and n_ctx % FFT_N2 == 0

    cond = jnp.concatenate([c, c_ctx[None, :], jnp.zeros((8 - bsz - 1, d), F32)], axis=0)
    mod = _modulation(cond, w_ada, b_ada)
    tabs = _tables(n_lat, n_ctx)
    xt = (x, ctx)

    for layer in range(depth):
        last = layer == depth - 1
        lw = _layer_weights(layer, params)
        mod_l = mod[layer].reshape(mod.shape[1], 1, mod.shape[2])
        q, k, v, f = _input_projection(xt, mod_l, lw, tabs, n_lat)
        attn = _attention(q, k, v, lw["bound"], n_lat, not last)
        four = _fourier_mix(f, tabs, n_lat, not last)
        moe = layer % 2 == 1
        n_out = n_lat if last else n_lat + n_ctx
        if moe:
            xt, h2, rt = _merge(xt, attn, four, mod_l, lw, n_lat, True)
            xt = _moe_ffn(xt, h2, rt, mod_l, lw, n_lat, n_out)
        else:
            xt, h2 = _merge(xt, attn, four, mod_l, lw, n_lat, False)
            xt = _dense_ffn(xt, h2, mod_l, lw, n_lat, n_out)
    return xt[:, :n_lat]
```

```python
import functools
import math

import jax
import jax.numpy as jnp
from jax import lax
from jax.experimental import pallas as pl
from jax.experimental.pallas import tpu as pltpu

F32 = jnp.float32
BF16 = jnp.bfloat16

N_HEADS = 8
QK_NOPE_DIM = 64
QK_ROPE_DIM = 32
QK_HEAD_DIM = QK_NOPE_DIM + QK_ROPE_DIM
V_HEAD_DIM = 64
GRID_W = 64
ROPE_THETA = 10000.0
FOURIER_GROUPS = 4
FOURIER_GROUP_DIM = 128
N_EXPERTS = 8
TOP_K = 2
EPS = 1e-6
LOG2E = 1.4426950408889634
MAX_DIRECT_BOUND = 50.0

LANES = 128
MXU_DIM = 256
VMEM_LIMIT_BYTES = 60 * 1024 * 1024

ROW_TILE = 256
INPROJ_ROWS = 256
Q_TILE = 1024
KV_CHUNKS = (1408, 640, 512, 256)
FFT_N2 = 128
FFT_K1_GROUP = 8
EXPERT_ROWS = 512
FFN_CHUNK = 1408
MOE_CHUNK = 512
DMA_UNROLL = 8
WEIGHT_STAGES = 8


def _params(**kw):
    return pltpu.CompilerParams(vmem_limit_bytes=VMEM_LIMIT_BYTES, **kw)


def _dot(a, b):
    return jnp.dot(a, b, preferred_element_type=F32)


def _split(a):
    hi = a.astype(BF16)
    lo = (a - hi.astype(F32)).astype(BF16)
    return hi, lo


def _rms(x):
    return x * lax.rsqrt(jnp.mean(x * x, axis=-1, keepdims=True) + EPS)


def _silu(x):
    return x / (1.0 + jnp.exp(-x))


def _mod_kernel(c_ref, w_ref, b_ref, o_ref):
    chi, clo = _split(_silu(c_ref[...]))
    whi, wlo = _split(w_ref[0])
    o_ref[0] = _dot(chi, whi) + _dot(clo, whi) + _dot(chi, wlo) + b_ref[0]


def _modulation(cond, w_ada, b_ada):
    depth, d, n = w_ada.shape
    tn = n // 4
    return pl.pallas_call(
        _mod_kernel,
        out_shape=jax.ShapeDtypeStruct((depth, cond.shape[0], n), F32),
        grid=(depth, n // tn),
        in_specs=[
            pl.BlockSpec(cond.shape, lambda l, j: (0, 0)),
            pl.BlockSpec((1, d, tn), lambda l, j: (l, 0, j)),
            pl.BlockSpec((1, 1, tn), lambda l, j: (l, 0, j)),
        ],
        out_specs=pl.BlockSpec((1, cond.shape[0], tn), lambda l, j: (l, 0, j)),
        compiler_params=_params(),
        name="adaln_modulation",
    )(cond, w_ada, b_ada.reshape(depth, 1, n))


def _token_specs(src, tm, n_lat_tiles):
    if not isinstance(src, tuple):
        return [pl.BlockSpec((1, tm, src.shape[-1]), lambda b, i: (b, i, 0))], [src]
    n = src[0].shape[-1]
    lat = pl.BlockSpec((1, tm, n), lambda b, i: (b, jnp.minimum(i, n_lat_tiles - 1), 0))
    ctx = pl.BlockSpec((1, tm, n), lambda b, i: (b, jnp.maximum(i - n_lat_tiles, 0), 0))
    return [lat, ctx], list(src)


def _token_tile(refs, n_lat_tiles):
    if len(refs) == 1:
        return refs[0][0]
    return jnp.where(pl.program_id(1) >= n_lat_tiles, refs[1][0], refs[0][0])


def _inproj_kernel(*refs, n_src, n_lat_tiles):
    x = _token_tile(refs[:n_src], n_lat_tiles)
    (mod_ref, g1_ref, win_ref, gq_ref, gkv_ref, wq_ref, wkv_ref, bd_ref, qg_ref, qgs_ref, kg_ref, kgs_ref,
     qoff_ref, koff_ref, voff_ref, cos_ref, sin_ref, q_ref, k_ref, v_ref, f_ref) = refs[n_src:]
    d = x.shape[-1]
    m = mod_ref[0]
    shift, scale = m[:, 0:d], m[:, d:2 * d]
    g1 = g1_ref[...] * (1.0 + scale)
    nq = gq_ref.shape[-1]
    nkv = gkv_ref.shape[-1]
    nf = f_ref.shape[-1]
    o_f, o_pe = nq + nkv, nq + nkv + nf
    hw = N_HEADS * LANES
    voff = voff_ref[...]
    bd = bd_ref[...]

    def pair(t):
        return jnp.concatenate([t, t], axis=1)

    def head_rsqrt(raw):
        ss = _dot((raw * raw).astype(BF16), bd)
        return lax.rsqrt(ss * (1.0 / QK_HEAD_DIM) + EPS)

    q_scale = QK_HEAD_DIM ** -0.5 * LOG2E
    qoff, koff = pair(qoff_ref[...]), pair(koff_ref[...])

    for r0 in range(0, x.shape[0], INPROJ_ROWS):
        rows = slice(r0, r0 + INPROJ_ROWS)
        h = _rms(x[rows]) * g1 + shift
        p = _dot(h.astype(BF16), win_ref[...])
        f_ref[0, rows] = p[:, o_f:o_pe].astype(BF16)
        cq = (_rms(p[:, 0:nq]) * gq_ref[...]).astype(BF16)
        ckv = (_rms(p[:, nq:o_f]) * gkv_ref[...]).astype(BF16)
        qq = _dot(cq, wq_ref[...])
        kv = _dot(ckv, wkv_ref[...])
        for hd in range(N_HEADS):
            v_ref[0, hd, rows] = (kv[:, hw + hd * LANES:hw + (hd + 1) * LANES] + voff).astype(BF16)
        kpe2 = pair(p[:, o_pe:o_pe + LANES])
        kpe_sw2 = pair(p[:, o_pe + LANES:o_pe + 2 * LANES])
        cos, sin = cos_ref[rows], sin_ref[rows]
        q_c, q_s = pair(qg_ref[...] * cos), pair(qgs_ref[...] * sin)
        k_c, k_s = pair(kg_ref[...] * cos), pair(kgs_ref[...] * sin)
        for hp in range(N_HEADS // 2):
            lo_, hi_ = hp * 2 * LANES, (hp + 1) * 2 * LANES
            q_raw, q_sw = qq[:, lo_:hi_], qq[:, hw + lo_:hw + hi_]
            qo = (head_rsqrt(q_raw) * q_scale) * (q_raw * q_c + q_sw * q_s) + qoff
            q_ref[0, 2 * hp, rows] = qo[:, :LANES].astype(BF16)
            q_ref[0, 2 * hp + 1, rows] = qo[:, LANES:].astype(BF16)
            k_raw = kv[:, lo_:hi_] + kpe2
            ko = head_rsqrt(k_raw) * (k_raw * k_c + kpe_sw2 * k_s) + koff
            k_ref[0, 2 * hp, rows] = ko[:, :LANES].astype(BF16)
            k_ref[0, 2 * hp + 1, rows] = ko[:, LANES:].astype(BF16)


def _mod_spec(n_lat_tiles, n_mod):
    def index(b, i):
        return (jnp.where(i < n_lat_tiles, b, pl.num_programs(0)), 0, 0)
    return pl.BlockSpec((1, 1, n_mod), index)


def _const_spec(shape):
    zeros = (0,) * len(shape)
    return pl.BlockSpec(shape, lambda b, i: zeros)


def _input_projection(xt, mod_l, lw, tabs, n_lat):
    pair = isinstance(xt, tuple)
    bsz, d = (xt[0] if pair else xt).shape[0::2]
    stot = xt[0].shape[1] + xt[1].shape[1] if pair else xt.shape[1]
    tm = ROW_TILE
    nf = FOURIER_GROUPS * FOURIER_GROUP_DIM
    tok = lambda n: pl.BlockSpec((1, tm, n), lambda b, i: (b, i, 0))
    rope = pl.BlockSpec((tm, LANES), lambda b, i: (i, 0))
    head_out = pl.BlockSpec((1, N_HEADS, tm, LANES), lambda b, i: (b, 0, i, 0))
    x_specs, x_args = _token_specs(xt, tm, n_lat // tm)
    return pl.pallas_call(
        functools.partial(_inproj_kernel, n_src=len(x_args), n_lat_tiles=n_lat // tm),
        out_shape=(
            jax.ShapeDtypeStruct((bsz, N_HEADS, stot, LANES), BF16),
            jax.ShapeDtypeStruct((bsz, N_HEADS, stot, LANES), BF16),
            jax.ShapeDtypeStruct((bsz, N_HEADS, stot, LANES), BF16),
            jax.ShapeDtypeStruct((bsz, stot, nf), BF16),
        ),
        grid=(bsz, stot // tm),
        in_specs=x_specs + [
            _mod_spec(n_lat // tm, mod_l.shape[-1]), _const_spec((1, d)),
            _const_spec(lw["w_in"].shape), _const_spec(lw["gq"].shape), _const_spec(lw["gkv"].shape),
            _const_spec(lw["w_q"].shape), _const_spec(lw["w_kv"].shape), _const_spec(tabs["bd"].shape),
        ] + [_const_spec((1, LANES))] * 7 + [rope, rope],
        out_specs=(head_out, head_out, head_out, tok(nf)),
        compiler_params=_params(),
        name="input_projection",
    )(*x_args, mod_l, lw["g1"], lw["w_in"], lw["gq"], lw["gkv"], lw["w_q"], lw["w_kv"], tabs["bd"],
      lw["qg"], lw["qg_sw"], lw["kg"], lw["kg_sw"], tabs["qoff"], lw["koff"], tabs["voff"],
      tabs["cos"], tabs["sin"])


def _scores(q, k_ref, hh, start, tk):
    k = k_ref[0, hh, pl.ds(start, tk), :]
    return lax.dot_general(q, k, (((1,), (1,)), ((), ())), preferred_element_type=F32)


def _attn_finish(accs, o_ref):
    outs = [a[:, :V_HEAD_DIM] / a[:, V_HEAD_DIM:V_HEAD_DIM + 1] for a in accs]
    o_ref[0] = jnp.concatenate(outs, axis=1).astype(BF16)


def _attn_bounded_kernel(q_ref, k_ref, v_ref, o_ref, *, n_chunks, tk):
    tq = q_ref.shape[2]
    qs = [q_ref[0, hh] for hh in range(2)]

    def body(j, accs):
        start = pl.multiple_of(j * tk, tk)
        new = []
        for hh in range(2):
            p = jnp.exp2(_scores(qs[hh], k_ref, hh, start, tk)).astype(BF16)
            new.append(accs[hh] + _dot(p, v_ref[0, hh, pl.ds(start, tk), :]))
        return tuple(new)

    zero = jnp.zeros((tq, LANES), F32)
    unroll = 2 if n_chunks % 2 == 0 else 1
    _attn_finish(lax.fori_loop(0, n_chunks, body, (zero, zero), unroll=unroll), o_ref)


def _attn_online_kernel(q_ref, k_ref, v_ref, o_ref, *, n_chunks, tk):
    tq = q_ref.shape[2]
    qs = [q_ref[0, hh] for hh in range(2)]

    def body(j, carry):
        start = pl.multiple_of(j * tk, tk)
        new = []
        for hh in range(2):
            m, acc = carry[hh]
            s = _scores(qs[hh], k_ref, hh, start, tk)
            m_new = jnp.maximum(m, jnp.max(s, axis=-1, keepdims=True))
            p = jnp.exp2(s - m_new).astype(BF16)
            acc = jnp.exp2(m - m_new) * acc + _dot(p, v_ref[0, hh, pl.ds(start, tk), :])
            new.append((m_new, acc))
        return tuple(new)

    init = (jnp.full((tq, 1), -jnp.inf, F32), jnp.zeros((tq, LANES), F32))
    out = lax.fori_loop(0, n_chunks, body, (init, init))
    _attn_finish([out[0][1], out[1][1]], o_ref)


def _attention_calls(body, tag, q, k, v, n_lat, with_ctx):
    bsz, _, stot, _ = q.shape
    n_ctx = stot - n_lat
    hp = N_HEADS // 2
    tk = next(c for c in KV_CHUNKS if stot % c == 0)
    tq = Q_TILE
    nv = N_HEADS * V_HEAD_DIM
    kv_all = pl.BlockSpec((1, 2, stot, LANES), lambda b, h, i: (b, h, 0, 0))
    attn_lat = pl.pallas_call(
        functools.partial(body, n_chunks=stot // tk, tk=tk),
        out_shape=jax.ShapeDtypeStruct((bsz, n_lat, nv), BF16),
        grid=(bsz, hp, n_lat // tq),
        in_specs=[pl.BlockSpec((1, 2, tq, LANES), lambda b, h, i: (b, h, i, 0)), kv_all, kv_all],
        out_specs=pl.BlockSpec((1, tq, LANES), lambda b, h, i: (b, i, h)),
        compiler_params=_params(),
        name="latent_attention" + tag,
    )(q, k, v)
    if not with_ctx:
        return attn_lat
    cblk = n_lat // n_ctx
    ctx_rows = pl.BlockSpec((1, 2, n_ctx, LANES), lambda b, h: (b, h, cblk, 0))
    attn_ctx = pl.pallas_call(
        functools.partial(body, n_chunks=1, tk=n_ctx),
        out_shape=jax.ShapeDtypeStruct((bsz, n_ctx, nv), BF16),
        grid=(bsz, hp),
        in_specs=[ctx_rows, ctx_rows, ctx_rows],
        out_specs=pl.BlockSpec((1, n_ctx, LANES), lambda b, h: (b, 0, h)),
        compiler_params=_params(),
        name="context_attention" + tag,
    )(q, k, v)
    return attn_lat, attn_ctx


def _attention(q, k, v, bound, n_lat, with_ctx):
    out = lax.cond(
        bound <= MAX_DIRECT_BOUND,
        lambda: _attention_calls(_attn_bounded_kernel, "", q, k, v, n_lat, with_ctx),
        lambda: _attention_calls(_attn_online_kernel, "_online", q, k, v, n_lat, with_ctx))
    return out if with_ctx else (out, None)


def _fft1_kernel(cs_ref, x_ref, a_ref):
    a_ref[0] = _dot(cs_ref[...], x_ref[0]).astype(BF16)


def _fft2_kernel(ar_ref, ai_ref, m2_ref, wch_ref, o_ref):
    nf = wch_ref.shape[-1]
    n2 = ar_ref.shape[2]
    for j in range(ar_ref.shape[1]):
        slab = jnp.concatenate([ar_ref[0, j], ai_ref[0, j]], axis=0)
        z = _dot(m2_ref[j], slab)
        zc = jnp.concatenate([z[:n2], z[n2:]], axis=1).astype(BF16)
        o_ref[0, :, j * nf:(j + 1) * nf] = _dot(zc, wch_ref[...]).astype(BF16)


def _fft_ctx_kernel(cs_ref, x_ref, wch_ref, o_ref):
    n = x_ref.shape[1]
    z = _dot(cs_ref[...], x_ref[0])
    zc = jnp.concatenate([z[:n], z[n:]], axis=1).astype(BF16)
    o_ref[0] = _dot(zc, wch_ref[...]).astype(BF16)


def _fourier_mix(f, tabs, n_lat, with_ctx):
    bsz, stot, nf = f.shape
    n_ctx = stot - n_lat
    n2 = FFT_N2
    n1 = n_lat // n2
    kg = min(FFT_K1_GROUP, n1)
    tc = min(n2 * nf, 8192)
    a = pl.pallas_call(
        _fft1_kernel,
        out_shape=jax.ShapeDtypeStruct((bsz, 2 * n1, n2 * nf), BF16),
        grid=(bsz, n2 * nf // tc),
        in_specs=[
            pl.BlockSpec((2 * n1, n1), lambda b, j: (0, 0)),
            pl.BlockSpec((1, n1, tc), lambda b, j: (b, 0, j)),
        ],
        out_specs=pl.BlockSpec((1, 2 * n1, tc), lambda b, j: (b, 0, j)),
        compiler_params=_params(),
        name="fourier_stage1",
    )(tabs["cs1"], f.reshape(bsz, stot // n2, n2 * nf))
    a4 = a.reshape(bsz, 2 * n1, n2, nf)
    four_lat = pl.pallas_call(
        _fft2_kernel,
        out_shape=jax.ShapeDtypeStruct((bsz, n2, n1 * nf), BF16),
        grid=(bsz, n1 // kg),
        in_specs=[
            pl.BlockSpec((1, kg, n2, nf), lambda b, g: (b, g, 0, 0)),
            pl.BlockSpec((1, kg, n2, nf), lambda b, g: (b, n1 // kg + g, 0, 0)),
            pl.BlockSpec((kg, 2 * n2, 2 * n2), lambda b, g: (g, 0, 0)),
            pl.BlockSpec((2 * nf, nf), lambda b, g: (0, 0)),
        ],
        out_specs=pl.BlockSpec((1, n2, kg * nf), lambda b, g: (b, 0, g)),
        compiler_params=_params(),
        name="fourier_stage2",
    )(a4, a4, tabs["m2"], tabs["wch_lat"]).reshape(bsz, n_lat, nf)
    if not with_ctx:
        return four_lat, None
    cblk = n_lat // n_ctx
    four_ctx = pl.pallas_call(
        _fft_ctx_kernel,
        out_shape=jax.ShapeDtypeStruct((bsz, n_ctx, nf), BF16),
        grid=(bsz,),
        in_specs=[
            pl.BlockSpec((2 * n_ctx, n_ctx), lambda b: (0, 0)),
            pl.BlockSpec((1, n_ctx, nf), lambda b: (b, cblk, 0)),
            pl.BlockSpec((2 * nf, nf), lambda b: (0, 0)),
        ],
        out_specs=pl.BlockSpec((1, n_ctx, nf), lambda b: (b, 0, 0)),
        compiler_params=_params(),
        name="fourier_context",
    )(tabs["cs_ctx"], f, tabs["wch_ctx"])
    return four_lat, four_ctx


def _merge_kernel(*refs, moe, n_lat_tiles, n_srcs):
    tiles = []
    for n in n_srcs:
        tiles.append(_token_tile(refs[:n], n_lat_tiles))
        refs = refs[n:]
    x, a, f = tiles
    mod_ref, ga_ref, gf_ref, wout_ref, g2_ref = refs[:5]
    rest = refs[5:]
    d = x.shape[-1]
    m = mod_ref[0]
    gate1, shift2, scale2 = m[:, 2 * d:3 * d], m[:, 3 * d:4 * d], m[:, 4 * d:5 * d]
    an = _rms(a.astype(F32)) * ga_ref[...]
    fn = _rms(f.astype(F32)) * gf_ref[...]
    y = _dot(jnp.concatenate([an, fn], axis=1).astype(BF16), wout_ref[...])
    xn = x + gate1 * y
    h2 = _rms(xn) * (g2_ref[...] * (1.0 + scale2)) + shift2
    if not moe:
        xo_ref, h_ref = rest
        xo_ref[0] = xn
        h_ref[0] = h2.astype(BF16)
        return
    wr_hi_ref, wr_lo_ref, xo_ref, h_ref, rt_ref = rest
    xo_ref[0] = xn
    _to_slabs(h_ref, h2)
    hi, lo = _split(h2)
    logits = _dot(hi, wr_hi_ref[...]) + _dot(lo, wr_hi_ref[...]) + _dot(hi, wr_lo_ref[...])
    lane = lax.broadcasted_iota(jnp.int32, logits.shape, 1).astype(F32)
    lg = jnp.where(lane < N_EXPERTS, logits, -jnp.inf)
    m1 = jnp.max(lg, axis=-1, keepdims=True)
    i1 = jnp.min(jnp.where(lg == m1, lane, float(LANES)), axis=-1, keepdims=True)
    lg2 = jnp.where(lane == i1, -jnp.inf, lg)
    m2 = jnp.max(lg2, axis=-1, keepdims=True)
    i2 = jnp.min(jnp.where(lg2 == m2, lane, float(LANES)), axis=-1, keepdims=True)
    e = jnp.exp(m2 - m1)
    w1 = 1.0 / (1.0 + e)
    w2 = e * w1
    rt_ref[0] = jnp.where(lane == 0, i1, jnp.where(lane == 1, i2, jnp.where(lane == 2, w1,
                                                                          jnp.where(lane == 3, w2, 0.0))))


def _merge(xt, attn, four, mod_l, lw, n_lat, moe):
    tm = ROW_TILE
    with_ctx = attn[1] is not None
    n_lat_tiles = n_lat // tm
    n_out = n_lat + (attn[1].shape[1] if with_ctx else 0)
    if not with_ctx:
        attn, four = attn[0], four[0]
        xt = xt[0] if isinstance(xt, tuple) else xt
    bsz, d = (xt[0] if isinstance(xt, tuple) else xt).shape[0::2]
    tok = lambda n: pl.BlockSpec((1, tm, n), lambda b, i: (b, i, 0))
    na, nf = lw["ga"].shape[-1], lw["gf"].shape[-1]
    in_specs, args, n_srcs = [], [], []
    for src in (xt, attn, four):
        specs, arrs = _token_specs(src, tm, n_lat_tiles)
        in_specs += specs
        args += arrs
        n_srcs.append(len(arrs))
    in_specs += [_mod_spec(n_lat_tiles, mod_l.shape[-1]),
                 _const_spec((1, na)), _const_spec((1, nf)), _const_spec((na + nf, d)), _const_spec((1, d))]
    args += [mod_l, lw["ga"], lw["gf"], lw["w_out"], lw["g2"]]
    out_shape = [jax.ShapeDtypeStruct((bsz, n_out, d), F32), jax.ShapeDtypeStruct((bsz, n_out, d), BF16)]
    out_specs = [tok(d), tok(d)]
    if moe:
        slab, tiles = d // LANES, n_out // tm
        out_shape[1] = jax.ShapeDtypeStruct((bsz * n_out * slab, LANES), F32)
        out_specs[1] = pl.BlockSpec((tm * slab, LANES), lambda b, i: (b * tiles + i, 0))
        in_specs += [_const_spec((d, LANES)), _const_spec((d, LANES))]
        args += [lw["wr_hi"], lw["wr_lo"]]
        out_shape.append(jax.ShapeDtypeStruct((bsz, n_out, LANES), F32))
        out_specs.append(tok(LANES))
    return pl.pallas_call(
        functools.partial(_merge_kernel, moe=moe, n_lat_tiles=n_lat_tiles, n_srcs=tuple(n_srcs)),
        out_shape=tuple(out_shape),
        grid=(bsz, n_out // tm),
        in_specs=in_specs,
        out_specs=tuple(out_specs),
        compiler_params=_params(),
        name="merge_router" if moe else "merge",
    )(*args)


def _swiglu_chunks(h, wg_ref, wu_ref, wd_ref, chunk, lead=()):
    f = wg_ref.shape[-1]
    acc = None
    for c in range(f // chunk):
        sl = slice(c * chunk, (c + 1) * chunk)
        g = _dot(h, wg_ref[lead + (slice(None), sl)])
        u = _dot(h, wu_ref[lead + (slice(None), sl)])
        part = _dot((_silu(g) * u).astype(BF16), wd_ref[lead + (sl, slice(None))])
        acc = part if acc is None else acc + part
    return acc


def _ffn_kernel(x_ref, h_ref, mod_ref, wg_ref, wu_ref, wd_ref, o_ref):
    d = x_ref.shape[-1]
    gate2 = mod_ref[0][:, 5 * d:6 * d]
    y = _swiglu_chunks(h_ref[0], wg_ref, wu_ref, wd_ref, FFN_CHUNK)
    o_ref[0] = x_ref[0] + gate2 * y


def _dense_ffn(xt, h2, mod_l, lw, n_lat, n_out):
    bsz, _, d = xt.shape
    tm = ROW_TILE
    tok = lambda: pl.BlockSpec((1, tm, d), lambda b, i: (b, i, 0))
    return pl.pallas_call(
        _ffn_kernel,
        out_shape=jax.ShapeDtypeStruct((bsz, n_out, d), F32),
        grid=(bsz, n_out // tm),
        in_specs=[tok(), tok(), _mod_spec(n_lat // tm, mod_l.shape[-1]),
                  _const_spec(lw["w_gate"].shape), _const_spec(lw["w_up"].shape),
                  _const_spec(lw["w_down"].shape)],
        out_specs=tok(),
        compiler_params=_params(),
        name="dense_swiglu",
    )(xt, h2, mod_l, lw["w_gate"], lw["w_up"], lw["w_down"])


def _to_slabs(ref, val):
    rows, n = val.shape[0], val.shape[1] // LANES
    for s in range(n):
        ref[pl.ds(s, rows, stride=n), :] = val[:, s * LANES:(s + 1) * LANES]


def _from_slabs(ref, n):
    rows = ref.shape[0] // n
    return jnp.concatenate([ref[pl.ds(s, rows, stride=n), :] for s in range(n)], axis=1)


def _row_copies(n_rows, make_copy):
    def start(r, c):
        for j, cp in enumerate(make_copy(r)):
            cp.start(priority=j % 2)
        return c

    def wait(r, c):
        for cp in make_copy(r):
            cp.wait()
        return c

    lax.fori_loop(0, n_rows, start, 0, unroll=DMA_UNROLL)
    lax.fori_loop(0, n_rows, wait, 0, unroll=DMA_UNROLL)


def _dispatch_kernel(zb_ref, dest_ref, h_ref, xs_ref, zbuf, zsems, sems, *, slab):
    tm = h_ref.shape[0] // slab

    @pl.when(pl.program_id(0) == 0)
    def _():
        zbuf[...] = jnp.zeros(zbuf.shape, zbuf.dtype)
        rows = zbuf.shape[0]

        def clear(i):
            start = pl.multiple_of(zb_ref[0, 0, i] * rows, rows)
            return pltpu.make_async_copy(zbuf, xs_ref.at[pl.ds(start, rows), :], zsems.at[i])

        for i in range(zsems.shape[0]):
            pl.when(zb_ref[0, 0, i] >= 0)(lambda i=i: clear(i).start())
        for i in range(zsems.shape[0]):
            pl.when(zb_ref[0, 0, i] >= 0)(lambda i=i: clear(i).wait())

    def copies(r):
        src = h_ref.at[pl.ds(pl.multiple_of(r * slab, slab), slab), :]
        return [pltpu.make_async_copy(
            src, xs_ref.at[pl.ds(pl.multiple_of(dest_ref[0, 0, TOP_K * r + j] * slab, slab), slab), :],
            sems.at[j, r]) for j in range(TOP_K)]

    _row_copies(tm, copies)


def _dispatch(h2, dest, zero_blocks, n_rows, slab):
    tm = ROW_TILE
    n_tiles = h2.shape[0] // (tm * slab)
    nz = zero_blocks.shape[0]
    return pl.pallas_call(
        functools.partial(_dispatch_kernel, slab=slab),
        out_shape=jax.ShapeDtypeStruct((n_rows * slab, LANES), h2.dtype),
        grid=(n_tiles,),
        in_specs=[
            pl.BlockSpec((1, 1, nz), lambda n: (0, 0, 0), memory_space=pltpu.SMEM),
            pl.BlockSpec((1, 1, TOP_K * tm), lambda n: (n, 0, 0), memory_space=pltpu.SMEM),
            pl.BlockSpec((tm * slab, LANES), lambda n: (n, 0)),
        ],
        out_specs=pl.BlockSpec(memory_space=pl.ANY),
        scratch_shapes=[pltpu.VMEM((EXPERT_ROWS * slab, LANES), h2.dtype), pltpu.SemaphoreType.DMA((nz,)),
                        pltpu.SemaphoreType.DMA((TOP_K, tm))],
        compiler_params=_params(),
        name="moe_dispatch",
    )(zero_blocks.reshape(1, 1, nz), dest.reshape(n_tiles, 1, TOP_K * tm), h2)


def _expert_kernel(be_ref, nu_ref, x_ref, wg_hbm, wu_hbm, wd_hbm, o_ref, wg_s, wu_s, wd_s, stage_in, stage_out,
                   sems, *, slab, layer):
    n = pl.program_id(0)
    used = n < nu_ref[0]
    e = be_ref[n]

    @pl.when(jnp.logical_and(used, jnp.logical_or(n == 0, e != be_ref[jnp.maximum(n - 1, 0)])))
    def _():
        rin, rout = stage_in.shape[1], stage_out.shape[1]
        chunks = [(hbm, dst, stage_in, rin, c) for hbm, dst in ((wg_hbm, wg_s), (wu_hbm, wu_s))
                  for c in range(wg_s.shape[0] // rin)]
        chunks += [(wd_hbm, wd_s, stage_out, rout, c) for c in range(wd_s.shape[0] // rout)]

        def copy(i):
            hbm, _, stage, rows, c = chunks[i]
            return pltpu.make_async_copy(hbm.at[layer, e, pl.ds(c * rows, rows), :], stage.at[i % 2], sems.at[i % 2])

        copy(0).start()
        for i, (_, dst, stage, rows, c) in enumerate(chunks):
            if i + 1 < len(chunks):
                copy(i + 1).start()
            copy(i).wait()
            dst[pl.ds(c * rows, rows), :] = stage[i % 2].astype(BF16)

    @pl.when(used)
    def _():
        x = _from_slabs(x_ref, slab).astype(BF16)
        _to_slabs(o_ref, _swiglu_chunks(x, wg_s, wu_s, wd_s, MOE_CHUNK))

    @pl.when(jnp.logical_not(used))
    def _():
        o_ref[...] = jnp.zeros(o_ref.shape, o_ref.dtype)


def _expert_blocks(xs, block_exp, n_used, lw, slab):
    tm = EXPERT_ROWS
    d, f = lw["w_moe_gate"].shape[-2:]
    rows = pl.BlockSpec((tm * slab, LANES), lambda n, be, nu: (n, 0))
    hbm = pl.BlockSpec(memory_space=pl.ANY)
    return pl.pallas_call(
        functools.partial(_expert_kernel, slab=slab, layer=lw["moe_index"]),
        out_shape=jax.ShapeDtypeStruct(xs.shape, F32),
        grid_spec=pltpu.PrefetchScalarGridSpec(
            num_scalar_prefetch=2,
            grid=(xs.shape[0] // (tm * slab),),
            in_specs=[rows, hbm, hbm, hbm],
            out_specs=rows,
            scratch_shapes=[pltpu.VMEM((d, f), BF16), pltpu.VMEM((d, f), BF16), pltpu.VMEM((f, d), BF16),
                            pltpu.VMEM((2, d // WEIGHT_STAGES, f), F32), pltpu.VMEM((2, f // WEIGHT_STAGES, d), F32),
                            pltpu.SemaphoreType.DMA((2,))],
        ),
        compiler_params=_params(),
        name="moe_expert_blocks",
    )(block_exp, n_used, xs, lw["w_moe_gate"], lw["w_moe_up"], lw["w_moe_down"])


def _combine_kernel(pos_ref, ys_ref, x_ref, rt_ref, mod_ref, o_ref, buf, sems, *, slab):
    tm, d = x_ref.shape[1], x_ref.shape[2]

    def copies(r):
        return [pltpu.make_async_copy(
            ys_ref.at[pl.ds(pl.multiple_of(pos_ref[0, 0, TOP_K * r + j] * slab, slab), slab), :],
            buf.at[j, pl.ds(pl.multiple_of(r * slab, slab), slab), :], sems.at[j, r]) for j in range(TOP_K)]

    _row_copies(tm, copies)
    gate2 = mod_ref[0][:, 5 * d:6 * d]
    rt = rt_ref[0]
    y = rt[:, 2:3] * _from_slabs(buf.at[0], slab) + rt[:, 3:4] * _from_slabs(buf.at[1], slab)
    o_ref[0] = x_ref[0] + gate2 * y


def _combine(ys, pos, xt, rt, mod_l, n_lat, n_out, slab):
    bsz, stot, d = xt.shape
    tm = ROW_TILE
    tiles = stot // tm
    tok = lambda n: pl.BlockSpec((1, tm, n), lambda b, i: (b, i, 0))
    return pl.pallas_call(
        functools.partial(_combine_kernel, slab=slab),
        out_shape=jax.ShapeDtypeStruct((bsz, n_out, d), F32),
        grid=(bsz, n_out // tm),
        in_specs=[
            pl.BlockSpec((1, 1, TOP_K * tm), lambda b, i: (b * tiles + i, 0, 0), memory_space=pltpu.SMEM),
            pl.BlockSpec(memory_space=pl.ANY),
            tok(d), tok(LANES), _mod_spec(n_lat // tm, mod_l.shape[-1]),
        ],
        out_specs=tok(d),
        scratch_shapes=[pltpu.VMEM((TOP_K, tm * slab, LANES), F32), pltpu.SemaphoreType.DMA((TOP_K, tm))],
        compiler_params=_params(),
        name="moe_combine",
    )(pos.reshape(bsz * tiles, 1, TOP_K * tm), ys, xt, rt, mod_l)


def _moe_ffn(xt, h2, rt, mod_l, lw, n_lat, n_out):
    bsz, stot, d = xt.shape
    slab = d // LANES
    n_assign = bsz * stot * TOP_K
    tm = EXPERT_ROWS
    exp_flat = rt[..., :TOP_K].astype(jnp.int32).reshape(n_assign)
    onehot = (exp_flat[:, None] == jnp.arange(N_EXPERTS, dtype=jnp.int32)[None, :]).astype(jnp.int32)
    csum = jnp.cumsum(onehot, axis=0)
    rank = jnp.sum(jnp.where(onehot > 0, csum, 0), axis=1) - 1
    counts = csum[-1]
    padded = (counts + tm - 1) // tm * tm
    pad_ends = jnp.cumsum(padded)
    pad_starts = pad_ends - padded
    dest = (jnp.sum(onehot * pad_starts[None, :], axis=1) + rank).astype(jnp.int32)
    n_blocks = (n_assign + N_EXPERTS * (tm - 1) + tm - 1) // tm
    block_start = jnp.arange(n_blocks, dtype=jnp.int32) * tm
    block_exp = jnp.minimum(jnp.sum((pad_ends[None, :] <= block_start[:, None]).astype(jnp.int32), axis=1),
                            N_EXPERTS - 1).astype(jnp.int32)
    n_used = (pad_ends[-1:] // tm).astype(jnp.int32)
    last_blocks = jnp.where(padded > 0, pad_ends // tm - 1, -1)
    tail = n_used + jnp.arange(N_EXPERTS, dtype=jnp.int32)
    zero_blocks = jnp.concatenate([last_blocks, jnp.where(tail < n_blocks, tail, -1)]).astype(jnp.int32)

    xs = _dispatch(h2, dest, zero_blocks, n_blocks * tm, slab)
    ys = _expert_blocks(xs, block_exp, n_used, lw, slab)
    return _combine(ys, dest, xt, rt, mod_l, n_lat, n_out, slab)


def _pair_swap_index():
    j = jnp.arange(QK_ROPE_DIM)
    return jnp.where((j % 16) < 8, j + 8, j - 8)


def _head_lanes(v):
    pad = [(0, 0)] * (v.ndim - 1) + [(0, LANES - QK_HEAD_DIM)]
    return jnp.pad(v, pad)


def _swap_rope(v):
    rope = v[..., QK_NOPE_DIM:][..., _pair_swap_index()]
    return _head_lanes(jnp.concatenate([jnp.zeros_like(v[..., :QK_NOPE_DIM]), rope], axis=-1))


def _dft_angles(n_rows, n_cols, period):
    idx = (jnp.arange(n_rows, dtype=jnp.int32)[:, None] * jnp.arange(n_cols, dtype=jnp.int32)[None, :]) % period
    return idx.astype(F32) * (2.0 * math.pi / period)


def _tables(n_lat, n_ctx):
    nf = FOURIER_GROUPS * FOURIER_GROUP_DIM
    rows = n_lat // GRID_W
    r = jnp.repeat(jnp.arange(rows, dtype=F32), GRID_W)
    col = jnp.tile(jnp.arange(GRID_W, dtype=F32), rows)
    half = QK_ROPE_DIM // 2
    inv_freq = ROPE_THETA ** (-jnp.arange(0, half, 2, dtype=F32) / half)
    ar, ac = r[:, None] * inv_freq, col[:, None] * inv_freq
    ones = jnp.ones((n_lat, QK_NOPE_DIM), F32)
    cos = jnp.concatenate([ones, jnp.cos(ar), jnp.cos(ar), jnp.cos(ac), jnp.cos(ac)], axis=1)
    sin = jnp.concatenate([0 * ones, -jnp.sin(ar), jnp.sin(ar), -jnp.sin(ac), jnp.sin(ac)], axis=1)
    cos = jnp.concatenate([cos, jnp.ones((n_ctx, QK_HEAD_DIM), F32)], axis=0)
    sin = jnp.concatenate([sin, jnp.zeros((n_ctx, QK_HEAD_DIM), F32)], axis=0)
    lane = jnp.arange(2 * LANES)
    bd = (lane[:, None] // LANES == lane[None, :] // LANES).astype(BF16)
    n2 = FFT_N2
    n1 = n_lat // n2
    a1 = _dft_angles(n1, n1, n1)
    cs1 = jnp.concatenate([jnp.cos(a1), -jnp.sin(a1)], axis=0).astype(BF16)
    k = (jnp.arange(n1, dtype=jnp.int32)[:, None, None] + n1 * jnp.arange(n2, dtype=jnp.int32)[None, :, None])
    ang = ((k * jnp.arange(n2, dtype=jnp.int32)[None, None, :]) % n_lat).astype(F32) * (2.0 * math.pi / n_lat)
    c2, s2 = jnp.cos(ang), jnp.sin(ang)
    m2 = jnp.concatenate([jnp.concatenate([c2, s2], axis=2), jnp.concatenate([-s2, c2], axis=2)], axis=1)
    ach = _dft_angles(FOURIER_GROUP_DIM, FOURIER_GROUP_DIM, FOURIER_GROUP_DIM)
    eye = jnp.eye(FOURIER_GROUPS, dtype=F32)
    wch = jnp.concatenate([jnp.kron(eye, jnp.cos(ach)), jnp.kron(eye, jnp.sin(ach))], axis=0)
    actx = _dft_angles(n_ctx, n_ctx, n_ctx)
    return {
        "cos": _head_lanes(cos), "sin": _head_lanes(sin), "bd": bd,
        "qoff": jnp.zeros((1, LANES), F32).at[0, QK_HEAD_DIM].set(1.0),
        "voff": jnp.zeros((1, LANES), F32).at[0, V_HEAD_DIM].set(1.0),
        "cs1": cs1, "m2": m2.astype(BF16),
        "wch_lat": (wch * (n_lat * FOURIER_GROUP_DIM) ** -0.5).astype(BF16),
        "wch_ctx": (wch * (n_ctx * FOURIER_GROUP_DIM) ** -0.5).astype(BF16),
        "cs_ctx": jnp.concatenate([jnp.cos(actx), -jnp.sin(actx)], axis=0).astype(BF16),
    }


def _layer_weights(layer, p):
    q_rank = p["q_lora_norm"].shape[-1]
    kv_rank = p["kv_lora_norm"].shape[-1]
    o_pe = q_rank + kv_rank
    o_f = o_pe + QK_ROPE_DIM
    w_in = p["w_in"][layer]
    d = w_in.shape[0]
    w_pe = w_in[:, o_pe:o_f]
    z64 = jnp.zeros((d, QK_NOPE_DIM), F32)
    z32 = jnp.zeros((d, LANES - QK_HEAD_DIM), F32)
    w_in_cat = jnp.concatenate(
        [w_in[:, :o_pe], w_in[:, o_f:], z64, w_pe, z32, z64, w_pe[:, _pair_swap_index()], z32], axis=1)
    w_uq = p["w_uq"][layer].reshape(q_rank, N_HEADS, QK_HEAD_DIM)
    w_q = jnp.concatenate([_head_lanes(w_uq).reshape(q_rank, -1), _swap_rope(w_uq).reshape(q_rank, -1)], axis=1)
    w_ukv = p["w_ukv"][layer].reshape(kv_rank, N_HEADS, QK_NOPE_DIM + V_HEAD_DIM)
    w_k = jnp.pad(w_ukv[..., :QK_NOPE_DIM], ((0, 0), (0, 0), (0, LANES - QK_NOPE_DIM))).reshape(kv_rank, -1)
    w_v = jnp.pad(w_ukv[..., QK_NOPE_DIM:], ((0, 0), (0, 0), (0, LANES - V_HEAD_DIM))).reshape(kv_rank, -1)
    row = lambda v: v.reshape(1, -1)
    bound = LOG2E * (1.01 * QK_HEAD_DIM ** 0.5 * jnp.max(jnp.abs(p["q_norm"][layer]))
                     * jnp.max(jnp.abs(p["k_norm"][layer])) + 0.1)
    lw = {
        "g1": row(p["norm1"][layer]), "g2": row(p["norm2"][layer]),
        "w_in": w_in_cat.astype(BF16),
        "gq": row(p["q_lora_norm"][layer]), "gkv": row(p["kv_lora_norm"][layer]),
        "w_q": w_q.astype(BF16), "w_kv": jnp.concatenate([w_k, w_v], axis=1).astype(BF16),
        "bound": bound, "koff": jnp.zeros((1, LANES), F32).at[0, QK_HEAD_DIM].set(-bound),
        "qg": row(_head_lanes(p["q_norm"][layer])), "qg_sw": row(_swap_rope(p["q_norm"][layer])),
        "kg": row(_head_lanes(p["k_norm"][layer])), "kg_sw": row(_swap_rope(p["k_norm"][layer])),
        "ga": row(p["out_norm_attn"][layer]), "gf": row(p["out_norm_fourier"][layer]),
        "w_out": p["w_out"][layer].astype(BF16),
    }
    if layer % 2 == 0:
        lw.update(w_gate=p["w_ffn_gate"][layer // 2].astype(BF16), w_up=p["w_ffn_up"][layer // 2].astype(BF16),
                  w_down=p["w_ffn_down"][layer // 2].astype(BF16))
    else:
        wr = jnp.pad(p["w_router"][layer // 2], ((0, 0), (0, LANES - N_EXPERTS)))
        wr_hi, wr_lo = _split(wr)
        lw.update(wr_hi=wr_hi, wr_lo=wr_lo,
                  moe_index=layer // 2, w_moe_gate=p["w_moe_gate"], w_moe_up=p["w_moe_up"],
                  w_moe_down=p["w_moe_down"])
    return lw


def kernel(x, c, ctx, c_ctx, w_ada, b_ada, norm1, w_in, q_lora_norm, kv_lora_norm, w_uq, w_ukv, q_norm, k_norm,
           out_norm_attn, out_norm_fourier, w_out, norm2, w_ffn_gate, w_ffn_up, w_ffn_down, w_router,
           w_moe_gate, w_moe_up, w_moe_down):
    params = dict(norm1=norm1, w_in=w_in, q_lora_norm=q_lora_norm, kv_lora_norm=kv_lora_norm, w_uq=w_uq,
                  w_ukv=w_ukv, q_norm=q_norm, k_norm=k_norm, out_norm_attn=out_norm_attn,
                  out_norm_fourier=out_norm_fourier, w_out=w_out, norm2=norm2, w_ffn_gate=w_ffn_gate,
                  w_ffn_up=w_ffn_up, w_ffn_down=w_ffn_down, w_router=w_router, w_moe_gate=w_moe_gate,
                  w_moe_up=w_moe_up, w_moe_down=w_moe_down)
    bsz, n_lat, d = x.shape
    n_ctx = ctx.shape[1]
    depth = w_ada.shape[0]
    assert n_lat % Q_TILE == 0 and n_lat % n_ctx == 0 and n_ctx % ROW_TILE == 0
    assert n_lat % FFT_N2 == 0 and n_lat % GRID_W == 0 and n_ctx % FFT_N2 == 0

    cond = jnp.concatenate([c, c_ctx[None, :], jnp.zeros((8 - bsz - 1, d), F32)], axis=0)
    mod = _modulation(cond, w_ada, b_ada)
    tabs = _tables(n_lat, n_ctx)
    xt = (x, ctx)

    for layer in range(depth):
        last = layer == depth - 1
        lw = _layer_weights(layer, params)
        mod_l = mod[layer].reshape(mod.shape[1], 1, mod.shape[2])
        q, k, v, f = _input_projection(xt, mod_l, lw, tabs, n_lat)
        attn = _attention(q, k, v, lw["bound"], n_lat, not last)
        four = _fourier_mix(f, tabs, n_lat, not last)
        moe = layer % 2 == 1
        n_out = n_lat if last else n_lat + n_ctx
        if moe:
            xt, h2, rt = _merge(xt, attn, four, mod_l, lw, n_lat, True)
            xt = _moe_ffn(xt, h2, rt, mod_l, lw, n_lat, n_out)
        else:
            xt, h2 = _merge(xt, attn, four, mod_l, lw, n_lat, False)
            xt = _dense_ffn(xt, h2, mod_l, lw, n_lat, n_out)
    return xt[:, :n_lat]
```

```python
import functools
import math

import jax
import jax.numpy as jnp
from jax import lax
from jax.experimental import pallas as pl
from jax.experimental.pallas import tpu as pltpu

F32 = jnp.float32
BF16 = jnp.bfloat16

N_HEADS = 8
QK_NOPE_DIM = 64
QK_ROPE_DIM = 32
QK_HEAD_DIM = QK_NOPE_DIM + QK_ROPE_DIM
V_HEAD_DIM = 64
GRID_W = 64
ROPE_THETA = 10000.0
FOURIER_GROUPS = 4
FOURIER_GROUP_DIM = 128
N_EXPERTS = 8
TOP_K = 2
EPS = 1e-6
LOG2E = 1.4426950408889634
MAX_DIRECT_BOUND = 50.0

LANES = 128
MXU_DIM = 256
VMEM_LIMIT_BYTES = 60 * 1024 * 1024

ROW_TILE = 256
INPROJ_ROWS = 256
Q_TILE = 1024
KV_CHUNKS = (1408, 640, 512, 256)
FFT_N2 = 128
FFT_K1_GROUP = 8
EXPERT_ROWS = 512
FFN_CHUNK = 1408
MOE_CHUNK = 512
DMA_UNROLL = 8
WEIGHT_STAGES = 8


def _params(**kw):
    return pltpu.CompilerParams(vmem_limit_bytes=VMEM_LIMIT_BYTES, **kw)


def _dot(a, b):
    return jnp.dot(a, b, preferred_element_type=F32)


def _split(a):
    hi = a.astype(BF16)
    lo = (a - hi.astype(F32)).astype(BF16)
    return hi, lo


def _rms(x):
    return x * lax.rsqrt(jnp.mean(x * x, axis=-1, keepdims=True) + EPS)


def _silu(x):
    return x / (1.0 + jnp.exp(-x))


def _mod_kernel(c_ref, w_ref, b_ref, o_ref):
    chi, clo = _split(_silu(c_ref[...]))
    whi, wlo = _split(w_ref[0])
    o_ref[0] = _dot(chi, whi) + _dot(clo, whi) + _dot(chi, wlo) + b_ref[0]


def _modulation(cond, w_ada, b_ada):
    depth, d, n = w_ada.shape
    tn = n // 4
    return pl.pallas_call(
        _mod_kernel,
        out_shape=jax.ShapeDtypeStruct((depth, cond.shape[0], n), F32),
        grid=(depth, n // tn),
        in_specs=[
            pl.BlockSpec(cond.shape, lambda l, j: (0, 0)),
            pl.BlockSpec((1, d, tn), lambda l, j: (l, 0, j)),
            pl.BlockSpec((1, 1, tn), lambda l, j: (l, 0, j)),
        ],
        out_specs=pl.BlockSpec((1, cond.shape[0], tn), lambda l, j: (l, 0, j)),
        compiler_params=_params(),
        name="adaln_modulation",
    )(cond, w_ada, b_ada.reshape(depth, 1, n))


def _token_specs(src, tm, n_lat_tiles):
    if not isinstance(src, tuple):
        return [pl.BlockSpec((1, tm, src.shape[-1]), lambda b, i: (b, i, 0))], [src]
    n = src[0].shape[-1]
    lat = pl.BlockSpec((1, tm, n), lambda b, i: (b, jnp.minimum(i, n_lat_tiles - 1), 0))
    ctx = pl.BlockSpec((1, tm, n), lambda b, i: (b, jnp.maximum(i - n_lat_tiles, 0), 0))
    return [lat, ctx], list(src)


def _token_tile(refs, n_lat_tiles):
    if len(refs) == 1:
        return refs[0][0]
    return jnp.where(pl.program_id(1) >= n_lat_tiles, refs[1][0], refs[0][0])


def _inproj_kernel(*refs, n_src, n_lat_tiles):
    x = _token_tile(refs[:n_src], n_lat_tiles)
    (mod_ref, g1_ref, win_ref, gq_ref, gkv_ref, wq_ref, wkv_ref, bd_ref, qg_ref, qgs_ref, kg_ref, kgs_ref,
     qoff_ref, koff_ref, voff_ref, cos_ref, sin_ref, q_ref, k_ref, v_ref, f_ref) = refs[n_src:]
    d = x.shape[-1]
    m = mod_ref[0]
    shift, scale = m[:, 0:d], m[:, d:2 * d]
    g1 = g1_ref[...] * (1.0 + scale)
    nq = gq_ref.shape[-1]
    nkv = gkv_ref.shape[-1]
    nf = f_ref.shape[-1]
    o_f, o_pe = nq + nkv, nq + nkv + nf
    hw = N_HEADS * LANES
    voff = voff_ref[...]
    bd = bd_ref[...]

    def pair(t):
        return jnp.concatenate([t, t], axis=1)

    def head_rsqrt(raw):
        ss = _dot((raw * raw).astype(BF16), bd)
        return lax.rsqrt(ss * (1.0 / QK_HEAD_DIM) + EPS)

    q_scale = QK_HEAD_DIM ** -0.5 * LOG2E
    qoff, koff = pair(qoff_ref[...]), pair(koff_ref[...])

    for r0 in range(0, x.shape[0], INPROJ_ROWS):
        rows = slice(r0, r0 + INPROJ_ROWS)
        h = _rms(x[rows]) * g1 + shift
        p = _dot(h.astype(BF16), win_ref[...])
        f_ref[0, rows] = p[:, o_f:o_pe].astype(BF16)
        cq = (_rms(p[:, 0:nq]) * gq_ref[...]).astype(BF16)
        ckv = (_rms(p[:, nq:o_f]) * gkv_ref[...]).astype(BF16)
        qq = _dot(cq, wq_ref[...])
        kv = _dot(ckv, wkv_ref[...])
        for hd in range(N_HEADS):
            v_ref[0, hd, rows] = (kv[:, hw + hd * LANES:hw + (hd + 1) * LANES]
                                  + voff[hd % 2:hd % 2 + 1]).astype(BF16)
        kpe2 = pair(p[:, o_pe:o_pe + LANES])
        kpe_sw2 = pair(p[:, o_pe + LANES:o_pe + 2 * LANES])
        cos, sin = cos_ref[rows], sin_ref[rows]
        q_c, q_s = pair(qg_ref[...] * cos), pair(qgs_ref[...] * sin)
        k_c, k_s = pair(kg_ref[...] * cos), pair(kgs_ref[...] * sin)
        for hp in range(N_HEADS // 2):
            lo_, hi_ = hp * 2 * LANES, (hp + 1) * 2 * LANES
            q_raw, q_sw = qq[:, lo_:hi_], qq[:, hw + lo_:hw + hi_]
            qo = (head_rsqrt(q_raw) * q_scale) * (q_raw * q_c + q_sw * q_s) + qoff
            q_ref[0, 2 * hp, rows] = qo[:, :LANES].astype(BF16)
            q_ref[0, 2 * hp + 1, rows] = qo[:, LANES:].astype(BF16)
            k_raw = kv[:, lo_:hi_] + kpe2
            ko = head_rsqrt(k_raw) * (k_raw * k_c + kpe_sw2 * k_s) + koff
            k_ref[0, 2 * hp, rows] = ko[:, :LANES].astype(BF16)
            k_ref[0, 2 * hp + 1, rows] = ko[:, LANES:].astype(BF16)


def _mod_spec(n_lat_tiles, n_mod):
    def index(b, i):
        return (jnp.where(i < n_lat_tiles, b, pl.num_programs(0)), 0, 0)
    return pl.BlockSpec((1, 1, n_mod), index)


def _const_spec(shape):
    zeros = (0,) * len(shape)
    return pl.BlockSpec(shape, lambda b, i: zeros)


def _input_projection(xt, mod_l, lw, tabs, n_lat):
    pair = isinstance(xt, tuple)
    bsz, d = (xt[0] if pair else xt).shape[0::2]
    stot = xt[0].shape[1] + xt[1].shape[1] if pair else xt.shape[1]
    tm = ROW_TILE
    nf = FOURIER_GROUPS * FOURIER_GROUP_DIM
    tok = lambda n: pl.BlockSpec((1, tm, n), lambda b, i: (b, i, 0))
    rope = pl.BlockSpec((tm, LANES), lambda b, i: (i, 0))
    head_out = pl.BlockSpec((1, N_HEADS, tm, LANES), lambda b, i: (b, 0, i, 0))
    x_specs, x_args = _token_specs(xt, tm, n_lat // tm)
    return pl.pallas_call(
        functools.partial(_inproj_kernel, n_src=len(x_args), n_lat_tiles=n_lat // tm),
        out_shape=(
            jax.ShapeDtypeStruct((bsz, N_HEADS, stot, LANES), BF16),
            jax.ShapeDtypeStruct((bsz, N_HEADS, stot, LANES), BF16),
            jax.ShapeDtypeStruct((bsz, N_HEADS, stot, LANES), BF16),
            jax.ShapeDtypeStruct((bsz, stot, nf), BF16),
        ),
        grid=(bsz, stot // tm),
        in_specs=x_specs + [
            _mod_spec(n_lat // tm, mod_l.shape[-1]), _const_spec((1, d)),
            _const_spec(lw["w_in"].shape), _const_spec(lw["gq"].shape), _const_spec(lw["gkv"].shape),
            _const_spec(lw["w_q"].shape), _const_spec(lw["w_kv"].shape), _const_spec(tabs["bd"].shape),
        ] + [_const_spec((1, LANES))] * 6 + [_const_spec((2, LANES)), rope, rope],
        out_specs=(head_out, head_out, head_out, tok(nf)),
        compiler_params=_params(),
        name="input_projection",
    )(*x_args, mod_l, lw["g1"], lw["w_in"], lw["gq"], lw["gkv"], lw["w_q"], lw["w_kv"], tabs["bd"],
      lw["qg"], lw["qg_sw"], lw["kg"], lw["kg_sw"], tabs["qoff"], lw["koff"], tabs["voff"],
      tabs["cos"], tabs["sin"])


def _scores(q, k_ref, hh, start, tk):
    k = k_ref[0, hh, pl.ds(start, tk), :]
    return lax.dot_general(q, k, (((1,), (1,)), ((), ())), preferred_element_type=F32)


def _attn_finish(accs, o_ref):
    even, odd = accs
    left = lax.broadcasted_iota(jnp.int32, even.shape, 1) < V_HEAD_DIM
    num = jnp.where(left, even, odd)
    den = pltpu.roll(jnp.where(left, odd, even), V_HEAD_DIM, axis=1)
    o_ref[0] = (num / den).astype(BF16)


def _attn_bounded_kernel(q_ref, k_ref, v_ref, o_ref, *, n_chunks, tk):
    tq = q_ref.shape[2]
    qs = [q_ref[0, hh] for hh in range(2)]

    def body(j, accs):
        start = pl.multiple_of(j * tk, tk)
        new = []
        for hh in range(2):
            p = jnp.exp2(_scores(qs[hh], k_ref, hh, start, tk)).astype(BF16)
            new.append(accs[hh] + _dot(p, v_ref[0, hh, pl.ds(start, tk), :]))
        return tuple(new)

    zero = jnp.zeros((tq, LANES), F32)
    unroll = 2 if n_chunks % 2 == 0 else 1
    _attn_finish(lax.fori_loop(0, n_chunks, body, (zero, zero), unroll=unroll), o_ref)


def _attn_online_kernel(q_ref, k_ref, v_ref, o_ref, *, n_chunks, tk):
    tq = q_ref.shape[2]
    qs = [q_ref[0, hh] for hh in range(2)]

    def body(j, carry):
        start = pl.multiple_of(j * tk, tk)
        new = []
        for hh in range(2):
            m, acc = carry[hh]
            s = _scores(qs[hh], k_ref, hh, start, tk)
            m_new = jnp.maximum(m, jnp.max(s, axis=-1, keepdims=True))
            p = jnp.exp2(s - m_new).astype(BF16)
            acc = jnp.exp2(m - m_new) * acc + _dot(p, v_ref[0, hh, pl.ds(start, tk), :])
            new.append((m_new, acc))
        return tuple(new)

    init = (jnp.full((tq, 1), -jnp.inf, F32), jnp.zeros((tq, LANES), F32))
    out = lax.fori_loop(0, n_chunks, body, (init, init))
    _attn_finish([out[0][1], out[1][1]], o_ref)


def _attention_calls(body, tag, q, k, v, n_lat, with_ctx):
    bsz, _, stot, _ = q.shape
    n_ctx = stot - n_lat
    hp = N_HEADS // 2
    tk = next(c for c in KV_CHUNKS if stot % c == 0)
    tq = Q_TILE
    nv = N_HEADS * V_HEAD_DIM
    kv_all = pl.BlockSpec((1, 2, stot, LANES), lambda b, h, i: (b, h, 0, 0))
    attn_lat = pl.pallas_call(
        functools.partial(body, n_chunks=stot // tk, tk=tk),
        out_shape=jax.ShapeDtypeStruct((bsz, n_lat, nv), BF16),
        grid=(bsz, hp, n_lat // tq),
        in_specs=[pl.BlockSpec((1, 2, tq, LANES), lambda b, h, i: (b, h, i, 0)), kv_all, kv_all],
        out_specs=pl.BlockSpec((1, tq, LANES), lambda b, h, i: (b, i, h)),
        compiler_params=_params(),
        name="latent_attention" + tag,
    )(q, k, v)
    if not with_ctx:
        return attn_lat
    cblk = n_lat // n_ctx
    ctx_rows = pl.BlockSpec((1, 2, n_ctx, LANES), lambda b, h: (b, h, cblk, 0))
    attn_ctx = pl.pallas_call(
        functools.partial(body, n_chunks=1, tk=n_ctx),
        out_shape=jax.ShapeDtypeStruct((bsz, n_ctx, nv), BF16),
        grid=(bsz, hp),
        in_specs=[ctx_rows, ctx_rows, ctx_rows],
        out_specs=pl.BlockSpec((1, n_ctx, LANES), lambda b, h: (b, 0, h)),
        compiler_params=_params(),
        name="context_attention" + tag,
    )(q, k, v)
    return attn_lat, attn_ctx


def _attention(q, k, v, bound, n_lat, with_ctx):
    out = lax.cond(
        bound <= MAX_DIRECT_BOUND,
        lambda: _attention_calls(_attn_bounded_kernel, "", q, k, v, n_lat, with_ctx),
        lambda: _attention_calls(_attn_online_kernel, "_online", q, k, v, n_lat, with_ctx))
    return out if with_ctx else (out, None)


def _fft1_kernel(cs_ref, x_ref, a_ref):
    a_ref[0] = _dot(cs_ref[...], x_ref[0]).astype(BF16)


def _fft2_kernel(ar_ref, ai_ref, m2_ref, wch_ref, o_ref):
    nf = wch_ref.shape[-1]
    n2 = ar_ref.shape[2]
    for j in range(ar_ref.shape[1]):
        slab = jnp.concatenate([ar_ref[0, j], ai_ref[0, j]], axis=0)
        z = _dot(m2_ref[j], slab)
        zc = jnp.concatenate([z[:n2], z[n2:]], axis=1).astype(BF16)
        o_ref[0, :, j * nf:(j + 1) * nf] = _dot(zc, wch_ref[...]).astype(BF16)


def _fft_ctx_kernel(cs_ref, x_ref, wch_ref, o_ref):
    n = x_ref.shape[1]
    z = _dot(cs_ref[...], x_ref[0])
    zc = jnp.concatenate([z[:n], z[n:]], axis=1).astype(BF16)
    o_ref[0] = _dot(zc, wch_ref[...]).astype(BF16)


def _fourier_mix(f, tabs, n_lat, with_ctx):
    bsz, stot, nf = f.shape
    n_ctx = stot - n_lat
    n2 = FFT_N2
    n1 = n_lat // n2
    kg = min(FFT_K1_GROUP, n1)
    tc = min(n2 * nf, 8192)
    a = pl.pallas_call(
        _fft1_kernel,
        out_shape=jax.ShapeDtypeStruct((bsz, 2 * n1, n2 * nf), BF16),
        grid=(bsz, n2 * nf // tc),
        in_specs=[
            pl.BlockSpec((2 * n1, n1), lambda b, j: (0, 0)),
            pl.BlockSpec((1, n1, tc), lambda b, j: (b, 0, j)),
        ],
        out_specs=pl.BlockSpec((1, 2 * n1, tc), lambda b, j: (b, 0, j)),
        compiler_params=_params(),
        name="fourier_stage1",
    )(tabs["cs1"], f.reshape(bsz, stot // n2, n2 * nf))
    a4 = a.reshape(bsz, 2 * n1, n2, nf)
    four_lat = pl.pallas_call(
        _fft2_kernel,
        out_shape=jax.ShapeDtypeStruct((bsz, n2, n1 * nf), BF16),
        grid=(bsz, n1 // kg),
        in_specs=[
            pl.BlockSpec((1, kg, n2, nf), lambda b, g: (b, g, 0, 0)),
            pl.BlockSpec((1, kg, n2, nf), lambda b, g: (b, n1 // kg + g, 0, 0)),
            pl.BlockSpec((kg, 2 * n2, 2 * n2), lambda b, g: (g, 0, 0)),
            pl.BlockSpec((2 * nf, nf), lambda b, g: (0, 0)),
        ],
        out_specs=pl.BlockSpec((1, n2, kg * nf), lambda b, g: (b, 0, g)),
        compiler_params=_params(),
        name="fourier_stage2",
    )(a4, a4, tabs["m2"], tabs["wch_lat"]).reshape(bsz, n_lat, nf)
    if not with_ctx:
        return four_lat, None
    cblk = n_lat // n_ctx
    four_ctx = pl.pallas_call(
        _fft_ctx_kernel,
        out_shape=jax.ShapeDtypeStruct((bsz, n_ctx, nf), BF16),
        grid=(bsz,),
        in_specs=[
            pl.BlockSpec((2 * n_ctx, n_ctx), lambda b: (0, 0)),
            pl.BlockSpec((1, n_ctx, nf), lambda b: (b, cblk, 0)),
            pl.BlockSpec((2 * nf, nf), lambda b: (0, 0)),
        ],
        out_specs=pl.BlockSpec((1, n_ctx, nf), lambda b: (b, 0, 0)),
        compiler_params=_params(),
        name="fourier_context",
    )(tabs["cs_ctx"], f, tabs["wch_ctx"])
    return four_lat, four_ctx


def _merge_kernel(*refs, moe, n_lat_tiles, n_srcs):
    tiles = []
    for n in n_srcs:
        tiles.append(_token_tile(refs[:n], n_lat_tiles))
        refs = refs[n:]
    x, a, f = tiles
    mod_ref, ga_ref, gf_ref, wout_ref, g2_ref = refs[:5]
    rest = refs[5:]
    d = x.shape[-1]
    m = mod_ref[0]
    gate1, shift2, scale2 = m[:, 2 * d:3 * d], m[:, 3 * d:4 * d], m[:, 4 * d:5 * d]
    an = _rms(a.astype(F32)) * ga_ref[...]
    fn = _rms(f.astype(F32)) * gf_ref[...]
    y = _dot(jnp.concatenate([an, fn], axis=1).astype(BF16), wout_ref[...])
    xn = x + gate1 * y
    h2 = _rms(xn) * (g2_ref[...] * (1.0 + scale2)) + shift2
    if not moe:
        xo_ref, h_ref = rest
        xo_ref[0] = xn
        h_ref[0] = h2.astype(BF16)
        return
    wr_hi_ref, wr_lo_ref, xo_ref, h_ref, rt_ref = rest
    xo_ref[0] = xn
    _to_slabs(h_ref, h2)
    hi, lo = _split(h2)
    logits = _dot(hi, wr_hi_ref[...]) + _dot(lo, wr_hi_ref[...]) + _dot(hi, wr_lo_ref[...])
    lane = lax.broadcasted_iota(jnp.int32, logits.shape, 1).astype(F32)
    lg = jnp.where(lane < N_EXPERTS, logits, -jnp.inf)
    m1 = jnp.max(lg, axis=-1, keepdims=True)
    i1 = jnp.min(jnp.where(lg == m1, lane, float(LANES)), axis=-1, keepdims=True)
    lg2 = jnp.where(lane == i1, -jnp.inf, lg)
    m2 = jnp.max(lg2, axis=-1, keepdims=True)
    i2 = jnp.min(jnp.where(lg2 == m2, lane, float(LANES)), axis=-1, keepdims=True)
    e = jnp.exp(m2 - m1)
    w1 = 1.0 / (1.0 + e)
    w2 = e * w1
    rt_ref[0] = jnp.where(lane == 0, i1, jnp.where(lane == 1, i2, jnp.where(lane == 2, w1,
                                                                          jnp.where(lane == 3, w2, 0.0))))


def _merge(xt, attn, four, mod_l, lw, n_lat, moe):
    tm = ROW_TILE
    with_ctx = attn[1] is not None
    n_lat_tiles = n_lat // tm
    n_out = n_lat + (attn[1].shape[1] if with_ctx else 0)
    if not with_ctx:
        attn, four = attn[0], four[0]
        xt = xt[0] if isinstance(xt, tuple) else xt
    bsz, d = (xt[0] if isinstance(xt, tuple) else xt).shape[0::2]
    tok = lambda n: pl.BlockSpec((1, tm, n), lambda b, i: (b, i, 0))
    na, nf = lw["ga"].shape[-1], lw["gf"].shape[-1]
    in_specs, args, n_srcs = [], [], []
    for src in (xt, attn, four):
        specs, arrs = _token_specs(src, tm, n_lat_tiles)
        in_specs += specs
        args += arrs
        n_srcs.append(len(arrs))
    in_specs += [_mod_spec(n_lat_tiles, mod_l.shape[-1]),
                 _const_spec((1, na)), _const_spec((1, nf)), _const_spec((na + nf, d)), _const_spec((1, d))]
    args += [mod_l, lw["ga"], lw["gf"], lw["w_out"], lw["g2"]]
    out_shape = [jax.ShapeDtypeStruct((bsz, n_out, d), F32), jax.ShapeDtypeStruct((bsz, n_out, d), BF16)]
    out_specs = [tok(d), tok(d)]
    if moe:
        slab, tiles = d // LANES, n_out // tm
        out_shape[1] = jax.ShapeDtypeStruct((bsz * n_out * slab, LANES), F32)
        out_specs[1] = pl.BlockSpec((tm * slab, LANES), lambda b, i: (b * tiles + i, 0))
        in_specs += [_const_spec((d, LANES)), _const_spec((d, LANES))]
        args += [lw["wr_hi"], lw["wr_lo"]]
        out_shape.append(jax.ShapeDtypeStruct((bsz, n_out, LANES), F32))
        out_specs.append(tok(LANES))
    return pl.pallas_call(
        functools.partial(_merge_kernel, moe=moe, n_lat_tiles=n_lat_tiles, n_srcs=tuple(n_srcs)),
        out_shape=tuple(out_shape),
        grid=(bsz, n_out // tm),
        in_specs=in_specs,
        out_specs=tuple(out_specs),
        compiler_params=_params(),
        name="merge_router" if moe else "merge",
    )(*args)


def _swiglu_chunks(h, wg_ref, wu_ref, wd_ref, chunk, lead=()):
    f = wg_ref.shape[-1]
    acc = None
    for c in range(f // chunk):
        sl = slice(c * chunk, (c + 1) * chunk)
        g = _dot(h, wg_ref[lead + (slice(None), sl)])
        u = _dot(h, wu_ref[lead + (slice(None), sl)])
        part = _dot((_silu(g) * u).astype(BF16), wd_ref[lead + (sl, slice(None))])
        acc = part if acc is None else acc + part
    return acc


def _ffn_kernel(x_ref, h_ref, mod_ref, wg_ref, wu_ref, wd_ref, o_ref):
    d = x_ref.shape[-1]
    gate2 = mod_ref[0][:, 5 * d:6 * d]
    y = _swiglu_chunks(h_ref[0], wg_ref, wu_ref, wd_ref, FFN_CHUNK)
    o_ref[0] = x_ref[0] + gate2 * y


def _dense_ffn(xt, h2, mod_l, lw, n_lat, n_out):
    bsz, _, d = xt.shape
    tm = ROW_TILE
    tok = lambda: pl.BlockSpec((1, tm, d), lambda b, i: (b, i, 0))
    return pl.pallas_call(
        _ffn_kernel,
        out_shape=jax.ShapeDtypeStruct((bsz, n_out, d), F32),
        grid=(bsz, n_out // tm),
        in_specs=[tok(), tok(), _mod_spec(n_lat // tm, mod_l.shape[-1]),
                  _const_spec(lw["w_gate"].shape), _const_spec(lw["w_up"].shape),
                  _const_spec(lw["w_down"].shape)],
        out_specs=tok(),
        compiler_params=_params(),
        name="dense_swiglu",
    )(xt, h2, mod_l, lw["w_gate"], lw["w_up"], lw["w_down"])


def _to_slabs(ref, val):
    rows, n = val.shape[0], val.shape[1] // LANES
    for s in range(n):
        ref[pl.ds(s, rows, stride=n), :] = val[:, s * LANES:(s + 1) * LANES]


def _from_slabs(ref, n):
    rows = ref.shape[0] // n
    return jnp.concatenate([ref[pl.ds(s, rows, stride=n), :] for s in range(n)], axis=1)


def _start_rows(n_rows, make_copy):
    def start(r, c):
        for j, cp in enumerate(make_copy(r)):
            cp.start(priority=j % 2)
        return c

    lax.fori_loop(0, n_rows, start, 0, unroll=DMA_UNROLL)


def _wait_rows(n_rows, make_copy):
    def wait(r, c):
        for cp in make_copy(r):
            cp.wait()
        return c

    lax.fori_loop(0, n_rows, wait, 0, unroll=DMA_UNROLL)


def _row_copies(n_rows, make_copy):
    _start_rows(n_rows, make_copy)
    _wait_rows(n_rows, make_copy)


def _dispatch_kernel(zb_ref, dest_ref, h_ref, xs_ref, zbuf, zsems, sems, *, slab):
    tm = h_ref.shape[0] // slab

    @pl.when(pl.program_id(0) == 0)
    def _():
        zbuf[...] = jnp.zeros(zbuf.shape, zbuf.dtype)
        rows = zbuf.shape[0]

        def clear(i):
            start = pl.multiple_of(zb_ref[0, 0, i] * rows, rows)
            return pltpu.make_async_copy(zbuf, xs_ref.at[pl.ds(start, rows), :], zsems.at[i])

        for i in range(zsems.shape[0]):
            pl.when(zb_ref[0, 0, i] >= 0)(lambda i=i: clear(i).start())
        for i in range(zsems.shape[0]):
            pl.when(zb_ref[0, 0, i] >= 0)(lambda i=i: clear(i).wait())

    def copies(r):
        src = h_ref.at[pl.ds(pl.multiple_of(r * slab, slab), slab), :]
        return [pltpu.make_async_copy(
            src, xs_ref.at[pl.ds(pl.multiple_of(dest_ref[0, 0, TOP_K * r + j] * slab, slab), slab), :],
            sems.at[j, r]) for j in range(TOP_K)]

    _row_copies(tm, copies)


def _dispatch(h2, dest, zero_blocks, n_rows, slab):
    tm = ROW_TILE
    n_tiles = h2.shape[0] // (tm * slab)
    nz = zero_blocks.shape[0]
    return pl.pallas_call(
        functools.partial(_dispatch_kernel, slab=slab),
        out_shape=jax.ShapeDtypeStruct((n_rows * slab, LANES), h2.dtype),
        grid=(n_tiles,),
        in_specs=[
            pl.BlockSpec((1, 1, nz), lambda n: (0, 0, 0), memory_space=pltpu.SMEM),
            pl.BlockSpec((1, 1, TOP_K * tm), lambda n: (n, 0, 0), memory_space=pltpu.SMEM),
            pl.BlockSpec((tm * slab, LANES), lambda n: (n, 0)),
        ],
        out_specs=pl.BlockSpec(memory_space=pl.ANY),
        scratch_shapes=[pltpu.VMEM((EXPERT_ROWS * slab, LANES), h2.dtype), pltpu.SemaphoreType.DMA((nz,)),
                        pltpu.SemaphoreType.DMA((TOP_K, tm))],
        compiler_params=_params(),
        name="moe_dispatch",
    )(zero_blocks.reshape(1, 1, nz), dest.reshape(n_tiles, 1, TOP_K * tm), h2)


def _expert_kernel(be_ref, nu_ref, x_ref, wg_hbm, wu_hbm, wd_hbm, o_ref, wg_s, wu_s, wd_s, stage_in, stage_out,
                   sems, *, slab, layer):
    n = pl.program_id(0)
    used = n < nu_ref[0]
    e = be_ref[n]

    @pl.when(jnp.logical_and(used, jnp.logical_or(n == 0, e != be_ref[jnp.maximum(n - 1, 0)])))
    def _():
        rin, rout = stage_in.shape[1], stage_out.shape[1]
        chunks = [(hbm, dst, stage_in, rin, c) for hbm, dst in ((wg_hbm, wg_s), (wu_hbm, wu_s))
                  for c in range(wg_s.shape[0] // rin)]
        chunks += [(wd_hbm, wd_s, stage_out, rout, c) for c in range(wd_s.shape[0] // rout)]

        def copy(i):
            hbm, _, stage, rows, c = chunks[i]
            return pltpu.make_async_copy(hbm.at[layer, e, pl.ds(c * rows, rows), :], stage.at[i % 2], sems.at[i % 2])

        copy(0).start()
        for i, (_, dst, stage, rows, c) in enumerate(chunks):
            if i + 1 < len(chunks):
                copy(i + 1).start()
            copy(i).wait()
            dst[pl.ds(c * rows, rows), :] = stage[i % 2].astype(BF16)

    @pl.when(used)
    def _():
        x = _from_slabs(x_ref, slab).astype(BF16)
        _to_slabs(o_ref, _swiglu_chunks(x, wg_s, wu_s, wd_s, MOE_CHUNK))

    @pl.when(jnp.logical_not(used))
    def _():
        o_ref[...] = jnp.zeros(o_ref.shape, o_ref.dtype)


def _expert_blocks(xs, block_exp, n_used, lw, slab):
    tm = EXPERT_ROWS
    d, f = lw["w_moe_gate"].shape[-2:]
    rows = pl.BlockSpec((tm * slab, LANES), lambda n, be, nu: (n, 0))
    hbm = pl.BlockSpec(memory_space=pl.ANY)
    return pl.pallas_call(
        functools.partial(_expert_kernel, slab=slab, layer=lw["moe_index"]),
        out_shape=jax.ShapeDtypeStruct(xs.shape, F32),
        grid_spec=pltpu.PrefetchScalarGridSpec(
            num_scalar_prefetch=2,
            grid=(xs.shape[0] // (tm * slab),),
            in_specs=[rows, hbm, hbm, hbm],
            out_specs=rows,
            scratch_shapes=[pltpu.VMEM((d, f), BF16), pltpu.VMEM((d, f), BF16), pltpu.VMEM((f, d), BF16),
                            pltpu.VMEM((2, d // WEIGHT_STAGES, f), F32), pltpu.VMEM((2, f // WEIGHT_STAGES, d), F32),
                            pltpu.SemaphoreType.DMA((2,))],
        ),
        compiler_params=_params(),
        name="moe_expert_blocks",
    )(block_exp, n_used, xs, lw["w_moe_gate"], lw["w_moe_up"], lw["w_moe_down"])


def _combine_kernel(pos_ref, pos_next_ref, ys_ref, x_ref, rt_ref, mod_ref, o_ref, buf, sems, *, slab):
    tm, d = x_ref.shape[1], x_ref.shape[2]
    step = pl.program_id(0) * pl.num_programs(1) + pl.program_id(1)
    n_steps = pl.num_programs(0) * pl.num_programs(1)
    slot = step % 2

    def copies(idx_ref, slot_):
        def make(r):
            return [pltpu.make_async_copy(
                ys_ref.at[pl.ds(pl.multiple_of(idx_ref[0, 0, TOP_K * r + j] * slab, slab), slab), :],
                buf.at[slot_, j, pl.ds(pl.multiple_of(r * slab, slab), slab), :],
                sems.at[slot_, j, r]) for j in range(TOP_K)]
        return make

    pl.when(step == 0)(lambda: _start_rows(tm, copies(pos_ref, slot)))
    pl.when(step + 1 < n_steps)(lambda: _start_rows(tm, copies(pos_next_ref, 1 - slot)))
    _wait_rows(tm, copies(pos_ref, slot))
    gate2 = mod_ref[0][:, 5 * d:6 * d]
    rt = rt_ref[0]
    y = rt[:, 2:3] * _from_slabs(buf.at[slot, 0], slab) + rt[:, 3:4] * _from_slabs(buf.at[slot, 1], slab)
    o_ref[0] = x_ref[0] + gate2 * y


def _combine(ys, pos, xt, rt, mod_l, n_lat, n_out, slab):
    bsz, stot, d = xt.shape
    tm = ROW_TILE
    tiles = stot // tm
    tok = lambda n: pl.BlockSpec((1, tm, n), lambda b, i: (b, i, 0))
    last = bsz * tiles - 1
    pos = pos.reshape(bsz * tiles, 1, TOP_K * tm)
    return pl.pallas_call(
        functools.partial(_combine_kernel, slab=slab),
        out_shape=jax.ShapeDtypeStruct((bsz, n_out, d), F32),
        grid=(bsz, n_out // tm),
        in_specs=[
            pl.BlockSpec((1, 1, TOP_K * tm), lambda b, i: (b * tiles + i, 0, 0), memory_space=pltpu.SMEM),
            pl.BlockSpec((1, 1, TOP_K * tm), lambda b, i: (jnp.minimum(b * tiles + i + 1, last), 0, 0),
                         memory_space=pltpu.SMEM),
            pl.BlockSpec(memory_space=pl.ANY),
            tok(d), tok(LANES), _mod_spec(n_lat // tm, mod_l.shape[-1]),
        ],
        out_specs=tok(d),
        scratch_shapes=[pltpu.VMEM((2, TOP_K, tm * slab, LANES), F32), pltpu.SemaphoreType.DMA((2, TOP_K, tm))],
        compiler_params=_params(),
        name="moe_combine",
    )(pos, pos, ys, xt, rt, mod_l)


def _moe_ffn(xt, h2, rt, mod_l, lw, n_lat, n_out):
    bsz, stot, d = xt.shape
    slab = d // LANES
    n_assign = bsz * stot * TOP_K
    tm = EXPERT_ROWS
    exp_flat = rt[..., :TOP_K].astype(jnp.int32).reshape(n_assign)
    onehot = (exp_flat[:, None] == jnp.arange(N_EXPERTS, dtype=jnp.int32)[None, :]).astype(jnp.int32)
    csum = jnp.cumsum(onehot, axis=0)
    rank = jnp.sum(jnp.where(onehot > 0, csum, 0), axis=1) - 1
    counts = csum[-1]
    padded = (counts + tm - 1) // tm * tm
    pad_ends = jnp.cumsum(padded)
    pad_starts = pad_ends - padded
    dest = (jnp.sum(onehot * pad_starts[None, :], axis=1) + rank).astype(jnp.int32)
    n_blocks = (n_assign + N_EXPERTS * (tm - 1) + tm - 1) // tm
    block_start = jnp.arange(n_blocks, dtype=jnp.int32) * tm
    block_exp = jnp.minimum(jnp.sum((pad_ends[None, :] <= block_start[:, None]).astype(jnp.int32), axis=1),
                            N_EXPERTS - 1).astype(jnp.int32)
    n_used = (pad_ends[-1:] // tm).astype(jnp.int32)
    last_blocks = jnp.where(padded > 0, pad_ends // tm - 1, -1)
    tail = n_used + jnp.arange(N_EXPERTS, dtype=jnp.int32)
    zero_blocks = jnp.concatenate([last_blocks, jnp.where(tail < n_blocks, tail, -1)]).astype(jnp.int32)

    xs = _dispatch(h2, dest, zero_blocks, n_blocks * tm, slab)
    ys = _expert_blocks(xs, block_exp, n_used, lw, slab)
    return _combine(ys, dest, xt, rt, mod_l, n_lat, n_out, slab)


def _pair_swap_index():
    j = jnp.arange(QK_ROPE_DIM)
    return jnp.where((j % 16) < 8, j + 8, j - 8)


def _head_lanes(v):
    pad = [(0, 0)] * (v.ndim - 1) + [(0, LANES - QK_HEAD_DIM)]
    return jnp.pad(v, pad)


def _swap_rope(v):
    rope = v[..., QK_NOPE_DIM:][..., _pair_swap_index()]
    return _head_lanes(jnp.concatenate([jnp.zeros_like(v[..., :QK_NOPE_DIM]), rope], axis=-1))


def _dft_angles(n_rows, n_cols, period):
    idx = (jnp.arange(n_rows, dtype=jnp.int32)[:, None] * jnp.arange(n_cols, dtype=jnp.int32)[None, :]) % period
    return idx.astype(F32) * (2.0 * math.pi / period)


def _tables(n_lat, n_ctx):
    nf = FOURIER_GROUPS * FOURIER_GROUP_DIM
    rows = n_lat // GRID_W
    r = jnp.repeat(jnp.arange(rows, dtype=F32), GRID_W)
    col = jnp.tile(jnp.arange(GRID_W, dtype=F32), rows)
    half = QK_ROPE_DIM // 2
    inv_freq = ROPE_THETA ** (-jnp.arange(0, half, 2, dtype=F32) / half)
    ar, ac = r[:, None] * inv_freq, col[:, None] * inv_freq
    ones = jnp.ones((n_lat, QK_NOPE_DIM), F32)
    cos = jnp.concatenate([ones, jnp.cos(ar), jnp.cos(ar), jnp.cos(ac), jnp.cos(ac)], axis=1)
    sin = jnp.concatenate([0 * ones, -jnp.sin(ar), jnp.sin(ar), -jnp.sin(ac), jnp.sin(ac)], axis=1)
    cos = jnp.concatenate([cos, jnp.ones((n_ctx, QK_HEAD_DIM), F32)], axis=0)
    sin = jnp.concatenate([sin, jnp.zeros((n_ctx, QK_HEAD_DIM), F32)], axis=0)
    lane = jnp.arange(2 * LANES)
    bd = (lane[:, None] // LANES == lane[None, :] // LANES).astype(BF16)
    n2 = FFT_N2
    n1 = n_lat // n2
    a1 = _dft_angles(n1, n1, n1)
    cs1 = jnp.concatenate([jnp.cos(a1), -jnp.sin(a1)], axis=0).astype(BF16)
    k = (jnp.arange(n1, dtype=jnp.int32)[:, None, None] + n1 * jnp.arange(n2, dtype=jnp.int32)[None, :, None])
    ang = ((k * jnp.arange(n2, dtype=jnp.int32)[None, None, :]) % n_lat).astype(F32) * (2.0 * math.pi / n_lat)
    c2, s2 = jnp.cos(ang), jnp.sin(ang)
    m2 = jnp.concatenate([jnp.concatenate([c2, s2], axis=2), jnp.concatenate([-s2, c2], axis=2)], axis=1)
    ach = _dft_angles(FOURIER_GROUP_DIM, FOURIER_GROUP_DIM, FOURIER_GROUP_DIM)
    eye = jnp.eye(FOURIER_GROUPS, dtype=F32)
    wch = jnp.concatenate([jnp.kron(eye, jnp.cos(ach)), jnp.kron(eye, jnp.sin(ach))], axis=0)
    actx = _dft_angles(n_ctx, n_ctx, n_ctx)
    return {
        "cos": _head_lanes(cos), "sin": _head_lanes(sin), "bd": bd,
        "qoff": jnp.zeros((1, LANES), F32).at[0, QK_HEAD_DIM].set(1.0),
        "voff": jnp.stack([(jnp.arange(LANES) >= V_HEAD_DIM), (jnp.arange(LANES) < V_HEAD_DIM)]).astype(F32),
        "cs1": cs1, "m2": m2.astype(BF16),
        "wch_lat": (wch * (n_lat * FOURIER_GROUP_DIM) ** -0.5).astype(BF16),
        "wch_ctx": (wch * (n_ctx * FOURIER_GROUP_DIM) ** -0.5).astype(BF16),
        "cs_ctx": jnp.concatenate([jnp.cos(actx), -jnp.sin(actx)], axis=0).astype(BF16),
    }


def _layer_weights(layer, p):
    q_rank = p["q_lora_norm"].shape[-1]
    kv_rank = p["kv_lora_norm"].shape[-1]
    o_pe = q_rank + kv_rank
    o_f = o_pe + QK_ROPE_DIM
    w_in = p["w_in"][layer]
    d = w_in.shape[0]
    w_pe = w_in[:, o_pe:o_f]
    z64 = jnp.zeros((d, QK_NOPE_DIM), F32)
    z32 = jnp.zeros((d, LANES - QK_HEAD_DIM), F32)
    w_in_cat = jnp.concatenate(
        [w_in[:, :o_pe], w_in[:, o_f:], z64, w_pe, z32, z64, w_pe[:, _pair_swap_index()], z32], axis=1)
    w_uq = p["w_uq"][layer].reshape(q_rank, N_HEADS, QK_HEAD_DIM)
    w_q = jnp.concatenate([_head_lanes(w_uq).reshape(q_rank, -1), _swap_rope(w_uq).reshape(q_rank, -1)], axis=1)
    w_ukv = p["w_ukv"][layer].reshape(kv_rank, N_HEADS, QK_NOPE_DIM + V_HEAD_DIM)
    w_k = jnp.pad(w_ukv[..., :QK_NOPE_DIM], ((0, 0), (0, 0), (0, LANES - QK_NOPE_DIM))).reshape(kv_rank, -1)
    w_v = w_ukv[..., QK_NOPE_DIM:]
    zv = jnp.zeros_like(w_v)
    odd = (jnp.arange(N_HEADS) % 2 == 1)[None, :, None]
    w_v = jnp.where(odd, jnp.concatenate([zv, w_v], axis=-1), jnp.concatenate([w_v, zv], axis=-1))
    w_v = w_v.reshape(kv_rank, -1)
    row = lambda v: v.reshape(1, -1)
    bound = LOG2E * (1.01 * QK_HEAD_DIM ** 0.5 * jnp.max(jnp.abs(p["q_norm"][layer]))
                     * jnp.max(jnp.abs(p["k_norm"][layer])) + 0.1)
    lw = {
        "g1": row(p["norm1"][layer]), "g2": row(p["norm2"][layer]),
        "w_in": w_in_cat.astype(BF16),
        "gq": row(p["q_lora_norm"][layer]), "gkv": row(p["kv_lora_norm"][layer]),
        "w_q": w_q.astype(BF16), "w_kv": jnp.concatenate([w_k, w_v], axis=1).astype(BF16),
        "bound": bound, "koff": jnp.zeros((1, LANES), F32).at[0, QK_HEAD_DIM].set(-bound),
        "qg": row(_head_lanes(p["q_norm"][layer])), "qg_sw": row(_swap_rope(p["q_norm"][layer])),
        "kg": row(_head_lanes(p["k_norm"][layer])), "kg_sw": row(_swap_rope(p["k_norm"][layer])),
        "ga": row(p["out_norm_attn"][layer]), "gf": row(p["out_norm_fourier"][layer]),
        "w_out": p["w_out"][layer].astype(BF16),
    }
    if layer % 2 == 0:
        lw.update(w_gate=p["w_ffn_gate"][layer // 2].astype(BF16), w_up=p["w_ffn_up"][layer // 2].astype(BF16),
                  w_down=p["w_ffn_down"][layer // 2].astype(BF16))
    else:
        wr = jnp.pad(p["w_router"][layer // 2], ((0, 0), (0, LANES - N_EXPERTS)))
        wr_hi, wr_lo = _split(wr)
        lw.update(wr_hi=wr_hi, wr_lo=wr_lo,
                  moe_index=layer // 2, w_moe_gate=p["w_moe_gate"], w_moe_up=p["w_moe_up"],
                  w_moe_down=p["w_moe_down"])
    return lw


def kernel(x, c, ctx, c_ctx, w_ada, b_ada, norm1, w_in, q_lora_norm, kv_lora_norm, w_uq, w_ukv, q_norm, k_norm,
           out_norm_attn, out_norm_fourier, w_out, norm2, w_ffn_gate, w_ffn_up, w_ffn_down, w_router,
           w_moe_gate, w_moe_up, w_moe_down):
    params = dict(norm1=norm1, w_in=w_in, q_lora_norm=q_lora_norm, kv_lora_norm=kv_lora_norm, w_uq=w_uq,
                  w_ukv=w_ukv, q_norm=q_norm, k_norm=k_norm, out_norm_attn=out_norm_attn,
                  out_norm_fourier=out_norm_fourier, w_out=w_out, norm2=norm2, w_ffn_gate=w_ffn_gate,
                  w_ffn_up=w_ffn_up, w_ffn_down=w_ffn_down, w_router=w_router, w_moe_gate=w_moe_gate,
                  w_moe_up=w_moe_up, w_moe_down=w_moe_down)
    bsz, n_lat, d = x.shape
    n_ctx = ctx.shape[1]
    depth = w_ada.shape[0]
    assert n_lat % Q_TILE == 0 and n_lat % n_ctx == 0 and n_ctx % ROW_TILE == 0
    assert n_lat % FFT_N2 == 0 and n_lat % GRID_W == 0 and n_ctx % FFT_N2 == 0

    cond = jnp.concatenate([c, c_ctx[None, :], jnp.zeros((8 - bsz - 1, d), F32)], axis=0)
    mod = _modulation(cond, w_ada, b_ada)
    tabs = _tables(n_lat, n_ctx)
    xt = (x, ctx)

    for layer in range(depth):
        last = layer == depth - 1
        lw = _layer_weights(layer, params)
        mod_l = mod[layer].reshape(mod.shape[1], 1, mod.shape[2])
        q, k, v, f = _input_projection(xt, mod_l, lw, tabs, n_lat)
        attn = _attention(q, k, v, lw["bound"], n_lat, not last)
        four = _fourier_mix(f, tabs, n_lat, not last)
        moe = layer % 2 == 1
        n_out = n_lat if last else n_lat + n_ctx
        if moe:
            xt, h2, rt = _merge(xt, attn, four, mod_l, lw, n_lat, True)
            xt = _moe_ffn(xt, h2, rt, mod_l, lw, n_lat, n_out)
        else:
            xt, h2 = _merge(xt, attn, four, mod_l, lw, n_lat, False)
            xt = _dense_ffn(xt, h2, mod_l, lw, n_lat, n_out)
    return xt[:, :n_lat]
```

```python
import functools
import math

import jax
import jax.numpy as jnp
from jax import lax
from jax.experimental import pallas as pl
from jax.experimental.pallas import tpu as pltpu

F32 = jnp.float32
BF16 = jnp.bfloat16

N_HEADS = 8
QK_NOPE_DIM = 64
QK_ROPE_DIM = 32
QK_HEAD_DIM = QK_NOPE_DIM + QK_ROPE_DIM
V_HEAD_DIM = 64
GRID_W = 64
ROPE_THETA = 10000.0
FOURIER_GROUPS = 4
FOURIER_GROUP_DIM = 128
N_EXPERTS = 8
TOP_K = 2
EPS = 1e-6
LOG2E = 1.4426950408889634
MAX_DIRECT_BOUND = 50.0

LANES = 128
MXU_DIM = 256
VMEM_LIMIT_BYTES = 60 * 1024 * 1024

ROW_TILE = 256
INPROJ_ROWS = 256
Q_TILE = 1024
KV_CHUNKS = (1408, 640, 512, 256)
FFT_N2 = 128
FFT_K1_GROUP = 8
EXPERT_ROWS = 512
FFN_CHUNK = 1408
MOE_CHUNK = 512
DMA_UNROLL = 8


def _params(**kw):
    return pltpu.CompilerParams(vmem_limit_bytes=VMEM_LIMIT_BYTES, **kw)


def _dot(a, b):
    return jnp.dot(a, b, preferred_element_type=F32)


def _split(a):
    hi = a.astype(BF16)
    lo = (a - hi.astype(F32)).astype(BF16)
    return hi, lo


def _rms(x):
    return x * lax.rsqrt(jnp.mean(x * x, axis=-1, keepdims=True) + EPS)


def _silu(x):
    return x / (1.0 + jnp.exp(-x))


def _mod_kernel(c_ref, w_ref, b_ref, o_ref):
    chi, clo = _split(_silu(c_ref[...]))
    whi, wlo = _split(w_ref[0])
    o_ref[0] = _dot(chi, whi) + _dot(clo, whi) + _dot(chi, wlo) + b_ref[0]


def _modulation(cond, w_ada, b_ada):
    depth, d, n = w_ada.shape
    tn = n // 4
    return pl.pallas_call(
        _mod_kernel,
        out_shape=jax.ShapeDtypeStruct((depth, cond.shape[0], n), F32),
        grid=(depth, n // tn),
        in_specs=[
            pl.BlockSpec(cond.shape, lambda l, j: (0, 0)),
            pl.BlockSpec((1, d, tn), lambda l, j: (l, 0, j)),
            pl.BlockSpec((1, 1, tn), lambda l, j: (l, 0, j)),
        ],
        out_specs=pl.BlockSpec((1, cond.shape[0], tn), lambda l, j: (l, 0, j)),
        compiler_params=_params(),
        name="adaln_modulation",
    )(cond, w_ada, b_ada.reshape(depth, 1, n))


def _token_specs(src, tm, n_lat_tiles):
    if not isinstance(src, tuple):
        return [pl.BlockSpec((1, tm, src.shape[-1]), lambda b, i: (b, i, 0))], [src]
    n = src[0].shape[-1]
    lat = pl.BlockSpec((1, tm, n), lambda b, i: (b, jnp.minimum(i, n_lat_tiles - 1), 0))
    ctx = pl.BlockSpec((1, tm, n), lambda b, i: (b, jnp.maximum(i - n_lat_tiles, 0), 0))
    return [lat, ctx], list(src)


def _token_tile(refs, n_lat_tiles):
    if len(refs) == 1:
        return refs[0][0]
    return jnp.where(pl.program_id(1) >= n_lat_tiles, refs[1][0], refs[0][0])


def _inproj_kernel(*refs, n_src, n_lat_tiles):
    x = _token_tile(refs[:n_src], n_lat_tiles)
    (mod_ref, g1_ref, win_ref, gq_ref, gkv_ref, wq_ref, wkv_ref, bd_ref, qg_ref, qgs_ref, kg_ref, kgs_ref,
     qoff_ref, koff_ref, voff_ref, cos_ref, sin_ref, q_ref, k_ref, v_ref, f_ref) = refs[n_src:]
    d = x.shape[-1]
    m = mod_ref[0]
    shift, scale = m[:, 0:d], m[:, d:2 * d]
    g1 = g1_ref[...] * (1.0 + scale)
    nq = gq_ref.shape[-1]
    nkv = gkv_ref.shape[-1]
    nf = f_ref.shape[-1]
    o_f, o_pe = nq + nkv, nq + nkv + nf
    hw = N_HEADS * LANES
    voff = voff_ref[...]
    bd = bd_ref[...]

    def pair(t):
        return jnp.concatenate([t, t], axis=1)

    def head_rsqrt(raw):
        ss = _dot((raw * raw).astype(BF16), bd)
        return lax.rsqrt(ss * (1.0 / QK_HEAD_DIM) + EPS)

    q_scale = QK_HEAD_DIM ** -0.5 * LOG2E
    qoff, koff = pair(qoff_ref[...]), pair(koff_ref[...])

    for r0 in range(0, x.shape[0], INPROJ_ROWS):
        rows = slice(r0, r0 + INPROJ_ROWS)
        h = _rms(x[rows]) * g1 + shift
        p = _dot(h.astype(BF16), win_ref[...])
        f_ref[0, rows] = p[:, o_f:o_pe].astype(BF16)
        cq = (_rms(p[:, 0:nq]) * gq_ref[...]).astype(BF16)
        ckv = (_rms(p[:, nq:o_f]) * gkv_ref[...]).astype(BF16)
        qq = _dot(cq, wq_ref[...])
        kv = _dot(ckv, wkv_ref[...])
        for hd in range(N_HEADS):
            v_ref[0, hd, rows] = (kv[:, hw + hd * LANES:hw + (hd + 1) * LANES]
                                  + voff[hd % 2:hd % 2 + 1]).astype(BF16)
        kpe2 = pair(p[:, o_pe:o_pe + LANES])
        kpe_sw2 = pair(p[:, o_pe + LANES:o_pe + 2 * LANES])
        cos, sin = cos_ref[rows], sin_ref[rows]
        q_c, q_s = pair(qg_ref[...] * cos), pair(qgs_ref[...] * sin)
        k_c, k_s = pair(kg_ref[...] * cos), pair(kgs_ref[...] * sin)
        for hp in range(N_HEADS // 2):
            lo_, hi_ = hp * 2 * LANES, (hp + 1) * 2 * LANES
            q_raw, q_sw = qq[:, lo_:hi_], qq[:, hw + lo_:hw + hi_]
            qo = (head_rsqrt(q_raw) * q_scale) * (q_raw * q_c + q_sw * q_s) + qoff
            q_ref[0, 2 * hp, rows] = qo[:, :LANES].astype(BF16)
            q_ref[0, 2 * hp + 1, rows] = qo[:, LANES:].astype(BF16)
            k_raw = kv[:, lo_:hi_] + kpe2
            ko = head_rsqrt(k_raw) * (k_raw * k_c + kpe_sw2 * k_s) + koff
            k_ref[0, 2 * hp, rows] = ko[:, :LANES].astype(BF16)
            k_ref[0, 2 * hp + 1, rows] = ko[:, LANES:].astype(BF16)


def _mod_spec(n_lat_tiles, n_mod):
    def index(b, i):
        return (jnp.where(i < n_lat_tiles, b, pl.num_programs(0)), 0, 0)
    return pl.BlockSpec((1, 1, n_mod), index)


def _const_spec(shape):
    zeros = (0,) * len(shape)
    return pl.BlockSpec(shape, lambda b, i: zeros)


def _input_projection(xt, mod_l, lw, tabs, n_lat):
    pair = isinstance(xt, tuple)
    bsz, d = (xt[0] if pair else xt).shape[0::2]
    stot = xt[0].shape[1] + xt[1].shape[1] if pair else xt.shape[1]
    tm = ROW_TILE
    nf = FOURIER_GROUPS * FOURIER_GROUP_DIM
    tok = lambda n: pl.BlockSpec((1, tm, n), lambda b, i: (b, i, 0))
    rope = pl.BlockSpec((tm, LANES), lambda b, i: (i, 0))
    head_out = pl.BlockSpec((1, N_HEADS, tm, LANES), lambda b, i: (b, 0, i, 0))
    x_specs, x_args = _token_specs(xt, tm, n_lat // tm)
    return pl.pallas_call(
        functools.partial(_inproj_kernel, n_src=len(x_args), n_lat_tiles=n_lat // tm),
        out_shape=(
            jax.ShapeDtypeStruct((bsz, N_HEADS, stot, LANES), BF16),
            jax.ShapeDtypeStruct((bsz, N_HEADS, stot, LANES), BF16),
            jax.ShapeDtypeStruct((bsz, N_HEADS, stot, LANES), BF16),
            jax.ShapeDtypeStruct((bsz, stot, nf), BF16),
        ),
        grid=(bsz, stot // tm),
        in_specs=x_specs + [
            _mod_spec(n_lat // tm, mod_l.shape[-1]), _const_spec((1, d)),
            _const_spec(lw["w_in"].shape), _const_spec(lw["gq"].shape), _const_spec(lw["gkv"].shape),
            _const_spec(lw["w_q"].shape), _const_spec(lw["w_kv"].shape), _const_spec(tabs["bd"].shape),
        ] + [_const_spec((1, LANES))] * 6 + [_const_spec((2, LANES)), rope, rope],
        out_specs=(head_out, head_out, head_out, tok(nf)),
        compiler_params=_params(),
        name="input_projection",
    )(*x_args, mod_l, lw["g1"], lw["w_in"], lw["gq"], lw["gkv"], lw["w_q"], lw["w_kv"], tabs["bd"],
      lw["qg"], lw["qg_sw"], lw["kg"], lw["kg_sw"], tabs["qoff"], lw["koff"], tabs["voff"],
      tabs["cos"], tabs["sin"])


def _scores(q, k_ref, hh, start, tk):
    k = k_ref[0, hh, pl.ds(start, tk), :]
    return lax.dot_general(q, k, (((1,), (1,)), ((), ())), preferred_element_type=F32)


def _attn_finish(accs, o_ref):
    even, odd = accs
    left = lax.broadcasted_iota(jnp.int32, even.shape, 1) < V_HEAD_DIM
    num = jnp.where(left, even, odd)
    den = pltpu.roll(jnp.where(left, odd, even), V_HEAD_DIM, axis=1)
    o_ref[0] = (num / den).astype(BF16)


def _attn_bounded_kernel(q_ref, k_ref, v_ref, o_ref, *, n_chunks, tk):
    tq = q_ref.shape[2]
    qs = [q_ref[0, hh] for hh in range(2)]

    def body(j, accs):
        start = pl.multiple_of(j * tk, tk)
        new = []
        for hh in range(2):
            p = jnp.exp2(_scores(qs[hh], k_ref, hh, start, tk)).astype(BF16)
            new.append(accs[hh] + _dot(p, v_ref[0, hh, pl.ds(start, tk), :]))
        return tuple(new)

    zero = jnp.zeros((tq, LANES), F32)
    unroll = 2 if n_chunks % 2 == 0 else 1
    _attn_finish(lax.fori_loop(0, n_chunks, body, (zero, zero), unroll=unroll), o_ref)


def _attn_online_kernel(q_ref, k_ref, v_ref, o_ref, *, n_chunks, tk):
    tq = q_ref.shape[2]
    qs = [q_ref[0, hh] for hh in range(2)]

    def body(j, carry):
        start = pl.multiple_of(j * tk, tk)
        new = []
        for hh in range(2):
            m, acc = carry[hh]
            s = _scores(qs[hh], k_ref, hh, start, tk)
            m_new = jnp.maximum(m, jnp.max(s, axis=-1, keepdims=True))
            p = jnp.exp2(s - m_new).astype(BF16)
            acc = jnp.exp2(m - m_new) * acc + _dot(p, v_ref[0, hh, pl.ds(start, tk), :])
            new.append((m_new, acc))
        return tuple(new)

    init = (jnp.full((tq, 1), -jnp.inf, F32), jnp.zeros((tq, LANES), F32))
    out = lax.fori_loop(0, n_chunks, body, (init, init))
    _attn_finish([out[0][1], out[1][1]], o_ref)


def _attention_calls(body, tag, q, k, v, n_lat, with_ctx):
    bsz, _, stot, _ = q.shape
    n_ctx = stot - n_lat
    hp = N_HEADS // 2
    tk = next(c for c in KV_CHUNKS if stot % c == 0)
    tq = Q_TILE
    nv = N_HEADS * V_HEAD_DIM
    kv_all = pl.BlockSpec((1, 2, stot, LANES), lambda b, h, i: (b, h, 0, 0))
    attn_lat = pl.pallas_call(
        functools.partial(body, n_chunks=stot // tk, tk=tk),
        out_shape=jax.ShapeDtypeStruct((bsz, n_lat, nv), BF16),
        grid=(bsz, hp, n_lat // tq),
        in_specs=[pl.BlockSpec((1, 2, tq, LANES), lambda b, h, i: (b, h, i, 0)), kv_all, kv_all],
        out_specs=pl.BlockSpec((1, tq, LANES), lambda b, h, i: (b, i, h)),
        compiler_params=_params(),
        name="latent_attention" + tag,
    )(q, k, v)
    if not with_ctx:
        return attn_lat
    cblk = n_lat // n_ctx
    ctx_rows = pl.BlockSpec((1, 2, n_ctx, LANES), lambda b, h: (b, h, cblk, 0))
    attn_ctx = pl.pallas_call(
        functools.partial(body, n_chunks=1, tk=n_ctx),
        out_shape=jax.ShapeDtypeStruct((bsz, n_ctx, nv), BF16),
        grid=(bsz, hp),
        in_specs=[ctx_rows, ctx_rows, ctx_rows],
        out_specs=pl.BlockSpec((1, n_ctx, LANES), lambda b, h: (b, 0, h)),
        compiler_params=_params(),
        name="context_attention" + tag,
    )(q, k, v)
    return attn_lat, attn_ctx


def _attention(q, k, v, bound, n_lat, with_ctx):
    out = lax.cond(
        bound <= MAX_DIRECT_BOUND,
        lambda: _attention_calls(_attn_bounded_kernel, "", q, k, v, n_lat, with_ctx),
        lambda: _attention_calls(_attn_online_kernel, "_online", q, k, v, n_lat, with_ctx))
    return out if with_ctx else (out, None)


def _fft1_kernel(cs_ref, x_ref, a_ref):
    a_ref[0] = _dot(cs_ref[...], x_ref[0]).astype(BF16)


def _fft2_kernel(ar_ref, ai_ref, m2_ref, wch_ref, o_ref):
    nf = wch_ref.shape[-1]
    n2 = ar_ref.shape[2]
    for j in range(ar_ref.shape[1]):
        slab = jnp.concatenate([ar_ref[0, j], ai_ref[0, j]], axis=0)
        z = _dot(m2_ref[j], slab)
        zc = jnp.concatenate([z[:n2], z[n2:]], axis=1).astype(BF16)
        o_ref[0, :, j * nf:(j + 1) * nf] = _dot(zc, wch_ref[...]).astype(BF16)


def _fft_ctx_kernel(cs_ref, x_ref, wch_ref, o_ref):
    n = x_ref.shape[1]
    z = _dot(cs_ref[...], x_ref[0])
    zc = jnp.concatenate([z[:n], z[n:]], axis=1).astype(BF16)
    o_ref[0] = _dot(zc, wch_ref[...]).astype(BF16)


def _fourier_mix(f, tabs, n_lat, with_ctx):
    bsz, stot, nf = f.shape
    n_ctx = stot - n_lat
    n2 = FFT_N2
    n1 = n_lat // n2
    kg = min(FFT_K1_GROUP, n1)
    tc = min(n2 * nf, 8192)
    a = pl.pallas_call(
        _fft1_kernel,
        out_shape=jax.ShapeDtypeStruct((bsz, 2 * n1, n2 * nf), BF16),
        grid=(bsz, n2 * nf // tc),
        in_specs=[
            pl.BlockSpec((2 * n1, n1), lambda b, j: (0, 0)),
            pl.BlockSpec((1, n1, tc), lambda b, j: (b, 0, j)),
        ],
        out_specs=pl.BlockSpec((1, 2 * n1, tc), lambda b, j: (b, 0, j)),
        compiler_params=_params(),
        name="fourier_stage1",
    )(tabs["cs1"], f.reshape(bsz, stot // n2, n2 * nf))
    a4 = a.reshape(bsz, 2 * n1, n2, nf)
    four_lat = pl.pallas_call(
        _fft2_kernel,
        out_shape=jax.ShapeDtypeStruct((bsz, n2, n1 * nf), BF16),
        grid=(bsz, n1 // kg),
        in_specs=[
            pl.BlockSpec((1, kg, n2, nf), lambda b, g: (b, g, 0, 0)),
            pl.BlockSpec((1, kg, n2, nf), lambda b, g: (b, n1 // kg + g, 0, 0)),
            pl.BlockSpec((kg, 2 * n2, 2 * n2), lambda b, g: (g, 0, 0)),
            pl.BlockSpec((2 * nf, nf), lambda b, g: (0, 0)),
        ],
        out_specs=pl.BlockSpec((1, n2, kg * nf), lambda b, g: (b, 0, g)),
        compiler_params=_params(),
        name="fourier_stage2",
    )(a4, a4, tabs["m2"], tabs["wch_lat"]).reshape(bsz, n_lat, nf)
    if not with_ctx:
        return four_lat, None
    cblk = n_lat // n_ctx
    four_ctx = pl.pallas_call(
        _fft_ctx_kernel,
        out_shape=jax.ShapeDtypeStruct((bsz, n_ctx, nf), BF16),
        grid=(bsz,),
        in_specs=[
            pl.BlockSpec((2 * n_ctx, n_ctx), lambda b: (0, 0)),
            pl.BlockSpec((1, n_ctx, nf), lambda b: (b, cblk, 0)),
            pl.BlockSpec((2 * nf, nf), lambda b: (0, 0)),
        ],
        out_specs=pl.BlockSpec((1, n_ctx, nf), lambda b: (b, 0, 0)),
        compiler_params=_params(),
        name="fourier_context",
    )(tabs["cs_ctx"], f, tabs["wch_ctx"])
    return four_lat, four_ctx


def _merge_kernel(*refs, moe, n_lat_tiles, n_srcs):
    tiles = []
    for n in n_srcs:
        tiles.append(_token_tile(refs[:n], n_lat_tiles))
        refs = refs[n:]
    x, a, f = tiles
    mod_ref, ga_ref, gf_ref, wout_ref, g2_ref = refs[:5]
    rest = refs[5:]
    d = x.shape[-1]
    m = mod_ref[0]
    gate1, shift2, scale2 = m[:, 2 * d:3 * d], m[:, 3 * d:4 * d], m[:, 4 * d:5 * d]
    an = _rms(a.astype(F32)) * ga_ref[...]
    fn = _rms(f.astype(F32)) * gf_ref[...]
    y = _dot(jnp.concatenate([an, fn], axis=1).astype(BF16), wout_ref[...])
    xn = x + gate1 * y
    h2 = _rms(xn) * (g2_ref[...] * (1.0 + scale2)) + shift2
    if not moe:
        xo_ref, h_ref = rest
        xo_ref[0] = xn
        h_ref[0] = h2.astype(BF16)
        return
    wr_hi_ref, wr_lo_ref, xo_ref, h_ref, rt_ref = rest
    xo_ref[0] = xn
    _to_slabs(h_ref, h2)
    hi, lo = _split(h2)
    logits = _dot(hi, wr_hi_ref[...]) + _dot(lo, wr_hi_ref[...]) + _dot(hi, wr_lo_ref[...])
    lane = lax.broadcasted_iota(jnp.int32, logits.shape, 1).astype(F32)
    lg = jnp.where(lane < N_EXPERTS, logits, -jnp.inf)
    m1 = jnp.max(lg, axis=-1, keepdims=True)
    i1 = jnp.min(jnp.where(lg == m1, lane, float(LANES)), axis=-1, keepdims=True)
    lg2 = jnp.where(lane == i1, -jnp.inf, lg)
    m2 = jnp.max(lg2, axis=-1, keepdims=True)
    i2 = jnp.min(jnp.where(lg2 == m2, lane, float(LANES)), axis=-1, keepdims=True)
    e = jnp.exp(m2 - m1)
    w1 = 1.0 / (1.0 + e)
    w2 = e * w1
    rt_ref[0] = jnp.where(lane == 0, i1, jnp.where(lane == 1, i2, jnp.where(lane == 2, w1,
                                                                          jnp.where(lane == 3, w2, 0.0))))


def _merge(xt, attn, four, mod_l, lw, n_lat, moe):
    tm = ROW_TILE
    with_ctx = attn[1] is not None
    n_lat_tiles = n_lat // tm
    n_out = n_lat + (attn[1].shape[1] if with_ctx else 0)
    if not with_ctx:
        attn, four = attn[0], four[0]
        xt = xt[0] if isinstance(xt, tuple) else xt
    bsz, d = (xt[0] if isinstance(xt, tuple) else xt).shape[0::2]
    tok = lambda n: pl.BlockSpec((1, tm, n), lambda b, i: (b, i, 0))
    na, nf = lw["ga"].shape[-1], lw["gf"].shape[-1]
    in_specs, args, n_srcs = [], [], []
    for src in (xt, attn, four):
        specs, arrs = _token_specs(src, tm, n_lat_tiles)
        in_specs += specs
        args += arrs
        n_srcs.append(len(arrs))
    in_specs += [_mod_spec(n_lat_tiles, mod_l.shape[-1]),
                 _const_spec((1, na)), _const_spec((1, nf)), _const_spec((na + nf, d)), _const_spec((1, d))]
    args += [mod_l, lw["ga"], lw["gf"], lw["w_out"], lw["g2"]]
    out_shape = [jax.ShapeDtypeStruct((bsz, n_out, d), F32), jax.ShapeDtypeStruct((bsz, n_out, d), BF16)]
    out_specs = [tok(d), tok(d)]
    if moe:
        slab, tiles = d // LANES, n_out // tm
        out_shape[1] = jax.ShapeDtypeStruct((bsz * n_out * slab, LANES), F32)
        out_specs[1] = pl.BlockSpec((tm * slab, LANES), lambda b, i: (b * tiles + i, 0))
        in_specs += [_const_spec((d, LANES)), _const_spec((d, LANES))]
        args += [lw["wr_hi"], lw["wr_lo"]]
        out_shape.append(jax.ShapeDtypeStruct((bsz, n_out, LANES), F32))
        out_specs.append(tok(LANES))
    return pl.pallas_call(
        functools.partial(_merge_kernel, moe=moe, n_lat_tiles=n_lat_tiles, n_srcs=tuple(n_srcs)),
        out_shape=tuple(out_shape),
        grid=(bsz, n_out // tm),
        in_specs=in_specs,
        out_specs=tuple(out_specs),
        compiler_params=_params(),
        name="merge_router" if moe else "merge",
    )(*args)


def _swiglu_chunks(h, wg_ref, wu_ref, wd_ref, chunk, lead=(), after_chunk=None):
    f = wg_ref.shape[-1]
    acc = None
    for c in range(f // chunk):
        sl = slice(c * chunk, (c + 1) * chunk)
        g = _dot(h, wg_ref[lead + (slice(None), sl)])
        u = _dot(h, wu_ref[lead + (slice(None), sl)])
        part = _dot((_silu(g) * u).astype(BF16), wd_ref[lead + (sl, slice(None))])
        acc = part if acc is None else acc + part
        if after_chunk is not None:
            after_chunk(c)
    return acc


def _ffn_kernel(x_ref, h_ref, mod_ref, wg_ref, wu_ref, wd_ref, o_ref):
    d = x_ref.shape[-1]
    gate2 = mod_ref[0][:, 5 * d:6 * d]
    y = _swiglu_chunks(h_ref[0], wg_ref, wu_ref, wd_ref, FFN_CHUNK)
    o_ref[0] = x_ref[0] + gate2 * y


def _dense_ffn(xt, h2, mod_l, lw, n_lat, n_out):
    bsz, _, d = xt.shape
    tm = ROW_TILE
    tok = lambda: pl.BlockSpec((1, tm, d), lambda b, i: (b, i, 0))
    return pl.pallas_call(
        _ffn_kernel,
        out_shape=jax.ShapeDtypeStruct((bsz, n_out, d), F32),
        grid=(bsz, n_out // tm),
        in_specs=[tok(), tok(), _mod_spec(n_lat // tm, mod_l.shape[-1]),
                  _const_spec(lw["w_gate"].shape), _const_spec(lw["w_up"].shape),
                  _const_spec(lw["w_down"].shape)],
        out_specs=tok(),
        compiler_params=_params(),
        name="dense_swiglu",
    )(xt, h2, mod_l, lw["w_gate"], lw["w_up"], lw["w_down"])


def _to_slabs(ref, val):
    rows, n = val.shape[0], val.shape[1] // LANES
    for s in range(n):
        ref[pl.ds(s, rows, stride=n), :] = val[:, s * LANES:(s + 1) * LANES]


def _from_slabs(ref, n):
    rows = ref.shape[0] // n
    return jnp.concatenate([ref[pl.ds(s, rows, stride=n), :] for s in range(n)], axis=1)


def _start_rows(n_rows, make_copy):
    def start(r, c):
        for j, cp in enumerate(make_copy(r)):
            cp.start(priority=j % 2)
        return c

    lax.fori_loop(0, n_rows, start, 0, unroll=DMA_UNROLL)


def _wait_rows(n_rows, make_copy):
    def wait(r, c):
        for cp in make_copy(r):
            cp.wait()
        return c

    lax.fori_loop(0, n_rows, wait, 0, unroll=DMA_UNROLL)


def _row_copies(n_rows, make_copy):
    _start_rows(n_rows, make_copy)
    _wait_rows(n_rows, make_copy)


def _dispatch_kernel(zb_ref, dest_ref, h_ref, xs_ref, zbuf, zsems, sems, *, slab):
    tm = h_ref.shape[0] // slab

    @pl.when(pl.program_id(0) == 0)
    def _():
        zbuf[...] = jnp.zeros(zbuf.shape, zbuf.dtype)
        rows = zbuf.shape[0]

        def clear(i):
            start = pl.multiple_of(zb_ref[0, 0, i] * rows, rows)
            return pltpu.make_async_copy(zbuf, xs_ref.at[pl.ds(start, rows), :], zsems.at[i])

        for i in range(zsems.shape[0]):
            pl.when(zb_ref[0, 0, i] >= 0)(lambda i=i: clear(i).start())
        for i in range(zsems.shape[0]):
            pl.when(zb_ref[0, 0, i] >= 0)(lambda i=i: clear(i).wait())

    def copies(r):
        src = h_ref.at[pl.ds(pl.multiple_of(r * slab, slab), slab), :]
        return [pltpu.make_async_copy(
            src, xs_ref.at[pl.ds(pl.multiple_of(dest_ref[0, 0, TOP_K * r + j] * slab, slab), slab), :],
            sems.at[j, r]) for j in range(TOP_K)]

    _row_copies(tm, copies)


def _dispatch(h2, dest, zero_blocks, n_rows, slab):
    tm = ROW_TILE
    n_tiles = h2.shape[0] // (tm * slab)
    nz = zero_blocks.shape[0]
    return pl.pallas_call(
        functools.partial(_dispatch_kernel, slab=slab),
        out_shape=jax.ShapeDtypeStruct((n_rows * slab, LANES), h2.dtype),
        grid=(n_tiles,),
        in_specs=[
            pl.BlockSpec((1, 1, nz), lambda n: (0, 0, 0), memory_space=pltpu.SMEM),
            pl.BlockSpec((1, 1, TOP_K * tm), lambda n: (n, 0, 0), memory_space=pltpu.SMEM),
            pl.BlockSpec((tm * slab, LANES), lambda n: (n, 0)),
        ],
        out_specs=pl.BlockSpec(memory_space=pl.ANY),
        scratch_shapes=[pltpu.VMEM((EXPERT_ROWS * slab, LANES), h2.dtype), pltpu.SemaphoreType.DMA((nz,)),
                        pltpu.SemaphoreType.DMA((TOP_K, tm))],
        compiler_params=_params(),
        name="moe_dispatch",
    )(zero_blocks.reshape(1, 1, nz), dest.reshape(n_tiles, 1, TOP_K * tm), h2)


def _expert_kernel(be_ref, nu_ref, x_ref, wg_hbm, wu_hbm, wd_hbm, o_ref, wg_s, wu_s, wd_s, st_g, st_u, st_d,
                   sems, *, slab, layer):
    n = pl.program_id(0)
    n_used = nu_ref[0]
    used = n < n_used
    e = be_ref[n]
    nxt = be_ref[jnp.minimum(n + 1, pl.num_programs(0) - 1)]
    hand_over = jnp.logical_and(n + 1 < n_used, nxt != e)
    n_chunks = wg_s.shape[1] // MOE_CHUNK

    def cols(c):
        return pl.ds(c * MOE_CHUNK, MOE_CHUNK)

    def copies(ex, c):
        slot = c % 2
        return [pltpu.make_async_copy(wg_hbm.at[layer, ex, :, cols(c)], st_g.at[slot], sems.at[0, slot]),
                pltpu.make_async_copy(wu_hbm.at[layer, ex, :, cols(c)], st_u.at[slot], sems.at[1, slot]),
                pltpu.make_async_copy(wd_hbm.at[layer, ex, cols(c), :], st_d.at[slot], sems.at[2, slot])]

    def start(ex, c):
        for cp in copies(ex, c):
            cp.start()

    def finish(ex, c):
        for cp in copies(ex, c):
            cp.wait()
        wg_s[:, cols(c)] = st_g[c % 2].astype(BF16)
        wu_s[:, cols(c)] = st_u[c % 2].astype(BF16)
        wd_s[cols(c), :] = st_d[c % 2].astype(BF16)

    @pl.when(n == 0)
    def _():
        start(e, 0)
        for c in range(n_chunks):
            if c + 1 < n_chunks:
                start(e, c + 1)
            finish(e, c)

    @pl.when(jnp.logical_and(used, jnp.logical_not(hand_over)))
    def _():
        x = _from_slabs(x_ref, slab).astype(BF16)
        _to_slabs(o_ref, _swiglu_chunks(x, wg_s, wu_s, wd_s, MOE_CHUNK))

    @pl.when(hand_over)
    def _():
        def replace(c):
            if c + 1 < n_chunks:
                start(nxt, c + 1)
            finish(nxt, c)

        x = _from_slabs(x_ref, slab).astype(BF16)
        start(nxt, 0)
        _to_slabs(o_ref, _swiglu_chunks(x, wg_s, wu_s, wd_s, MOE_CHUNK, after_chunk=replace))

    @pl.when(jnp.logical_not(used))
    def _():
        o_ref[...] = jnp.zeros(o_ref.shape, o_ref.dtype)


def _expert_blocks(xs, block_exp, n_used, lw, slab):
    tm = EXPERT_ROWS
    d, f = lw["w_moe_gate"].shape[-2:]
    rows = pl.BlockSpec((tm * slab, LANES), lambda n, be, nu: (n, 0))
    hbm = pl.BlockSpec(memory_space=pl.ANY)
    return pl.pallas_call(
        functools.partial(_expert_kernel, slab=slab, layer=lw["moe_index"]),
        out_shape=jax.ShapeDtypeStruct(xs.shape, F32),
        grid_spec=pltpu.PrefetchScalarGridSpec(
            num_scalar_prefetch=2,
            grid=(xs.shape[0] // (tm * slab),),
            in_specs=[rows, hbm, hbm, hbm],
            out_specs=rows,
            scratch_shapes=[pltpu.VMEM((d, f), BF16), pltpu.VMEM((d, f), BF16), pltpu.VMEM((f, d), BF16),
                            pltpu.VMEM((2, d, MOE_CHUNK), F32), pltpu.VMEM((2, d, MOE_CHUNK), F32),
                            pltpu.VMEM((2, MOE_CHUNK, d), F32), pltpu.SemaphoreType.DMA((3, 2))],
        ),
        compiler_params=_params(),
        name="moe_expert_blocks",
    )(block_exp, n_used, xs, lw["w_moe_gate"], lw["w_moe_up"], lw["w_moe_down"])


def _combine_kernel(pos_ref, pos_next_ref, ys_ref, x_ref, rt_ref, mod_ref, o_ref, buf, sems, *, slab):
    tm, d = x_ref.shape[1], x_ref.shape[2]
    step = pl.program_id(0) * pl.num_programs(1) + pl.program_id(1)
    n_steps = pl.num_programs(0) * pl.num_programs(1)
    slot = step % 2

    def copies(idx_ref, slot_):
        def make(r):
            return [pltpu.make_async_copy(
                ys_ref.at[pl.ds(pl.multiple_of(idx_ref[0, 0, TOP_K * r + j] * slab, slab), slab), :],
                buf.at[slot_, j, pl.ds(pl.multiple_of(r * slab, slab), slab), :],
                sems.at[slot_, j, r]) for j in range(TOP_K)]
        return make

    pl.when(step == 0)(lambda: _start_rows(tm, copies(pos_ref, slot)))
    pl.when(step + 1 < n_steps)(lambda: _start_rows(tm, copies(pos_next_ref, 1 - slot)))
    _wait_rows(tm, copies(pos_ref, slot))
    gate2 = mod_ref[0][:, 5 * d:6 * d]
    rt = rt_ref[0]
    y = rt[:, 2:3] * _from_slabs(buf.at[slot, 0], slab) + rt[:, 3:4] * _from_slabs(buf.at[slot, 1], slab)
    o_ref[0] = x_ref[0] + gate2 * y


def _combine(ys, pos, xt, rt, mod_l, n_lat, n_out, slab):
    bsz, stot, d = xt.shape
    tm = ROW_TILE
    tiles = stot // tm
    tok = lambda n: pl.BlockSpec((1, tm, n), lambda b, i: (b, i, 0))
    last = bsz * tiles - 1
    pos = pos.reshape(bsz * tiles, 1, TOP_K * tm)
    return pl.pallas_call(
        functools.partial(_combine_kernel, slab=slab),
        out_shape=jax.ShapeDtypeStruct((bsz, n_out, d), F32),
        grid=(bsz, n_out // tm),
        in_specs=[
            pl.BlockSpec((1, 1, TOP_K * tm), lambda b, i: (b * tiles + i, 0, 0), memory_space=pltpu.SMEM),
            pl.BlockSpec((1, 1, TOP_K * tm), lambda b, i: (jnp.minimum(b * tiles + i + 1, last), 0, 0),
                         memory_space=pltpu.SMEM),
            pl.BlockSpec(memory_space=pl.ANY),
            tok(d), tok(LANES), _mod_spec(n_lat // tm, mod_l.shape[-1]),
        ],
        out_specs=tok(d),
        scratch_shapes=[pltpu.VMEM((2, TOP_K, tm * slab, LANES), F32), pltpu.SemaphoreType.DMA((2, TOP_K, tm))],
        compiler_params=_params(),
        name="moe_combine",
    )(pos, pos, ys, xt, rt, mod_l)


def _moe_ffn(xt, h2, rt, mod_l, lw, n_lat, n_out):
    bsz, stot, d = xt.shape
    slab = d // LANES
    n_assign = bsz * stot * TOP_K
    tm = EXPERT_ROWS
    exp_flat = rt[..., :TOP_K].astype(jnp.int32).reshape(n_assign)
    onehot = (exp_flat[:, None] == jnp.arange(N_EXPERTS, dtype=jnp.int32)[None, :]).astype(jnp.int32)
    csum = jnp.cumsum(onehot, axis=0)
    rank = jnp.sum(jnp.where(onehot > 0, csum, 0), axis=1) - 1
    counts = csum[-1]
    padded = (counts + tm - 1) // tm * tm
    pad_ends = jnp.cumsum(padded)
    pad_starts = pad_ends - padded
    dest = (jnp.sum(onehot * pad_starts[None, :], axis=1) + rank).astype(jnp.int32)
    n_blocks = (n_assign + N_EXPERTS * (tm - 1) + tm - 1) // tm
    block_start = jnp.arange(n_blocks, dtype=jnp.int32) * tm
    block_exp = jnp.minimum(jnp.sum((pad_ends[None, :] <= block_start[:, None]).astype(jnp.int32), axis=1),
                            N_EXPERTS - 1).astype(jnp.int32)
    n_used = (pad_ends[-1:] // tm).astype(jnp.int32)
    last_blocks = jnp.where(padded > 0, pad_ends // tm - 1, -1)
    tail = n_used + jnp.arange(N_EXPERTS, dtype=jnp.int32)
    zero_blocks = jnp.concatenate([last_blocks, jnp.where(tail < n_blocks, tail, -1)]).astype(jnp.int32)

    xs = _dispatch(h2, dest, zero_blocks, n_blocks * tm, slab)
    ys = _expert_blocks(xs, block_exp, n_used, lw, slab)
    return _combine(ys, dest, xt, rt, mod_l, n_lat, n_out, slab)


def _pair_swap_index():
    j = jnp.arange(QK_ROPE_DIM)
    return jnp.where((j % 16) < 8, j + 8, j - 8)


def _head_lanes(v):
    pad = [(0, 0)] * (v.ndim - 1) + [(0, LANES - QK_HEAD_DIM)]
    return jnp.pad(v, pad)


def _swap_rope(v):
    rope = v[..., QK_NOPE_DIM:][..., _pair_swap_index()]
    return _head_lanes(jnp.concatenate([jnp.zeros_like(v[..., :QK_NOPE_DIM]), rope], axis=-1))


def _dft_angles(n_rows, n_cols, period):
    idx = (jnp.arange(n_rows, dtype=jnp.int32)[:, None] * jnp.arange(n_cols, dtype=jnp.int32)[None, :]) % period
    return idx.astype(F32) * (2.0 * math.pi / period)


def _tables(n_lat, n_ctx):
    nf = FOURIER_GROUPS * FOURIER_GROUP_DIM
    rows = n_lat // GRID_W
    r = jnp.repeat(jnp.arange(rows, dtype=F32), GRID_W)
    col = jnp.tile(jnp.arange(GRID_W, dtype=F32), rows)
    half = QK_ROPE_DIM // 2
    inv_freq = ROPE_THETA ** (-jnp.arange(0, half, 2, dtype=F32) / half)
    ar, ac = r[:, None] * inv_freq, col[:, None] * inv_freq
    ones = jnp.ones((n_lat, QK_NOPE_DIM), F32)
    cos = jnp.concatenate([ones, jnp.cos(ar), jnp.cos(ar), jnp.cos(ac), jnp.cos(ac)], axis=1)
    sin = jnp.concatenate([0 * ones, -jnp.sin(ar), jnp.sin(ar), -jnp.sin(ac), jnp.sin(ac)], axis=1)
    cos = jnp.concatenate([cos, jnp.ones((n_ctx, QK_HEAD_DIM), F32)], axis=0)
    sin = jnp.concatenate([sin, jnp.zeros((n_ctx, QK_HEAD_DIM), F32)], axis=0)
    lane = jnp.arange(2 * LANES)
    bd = (lane[:, None] // LANES == lane[None, :] // LANES).astype(BF16)
    n2 = FFT_N2
    n1 = n_lat // n2
    a1 = _dft_angles(n1, n1, n1)
    cs1 = jnp.concatenate([jnp.cos(a1), -jnp.sin(a1)], axis=0).astype(BF16)
    k = (jnp.arange(n1, dtype=jnp.int32)[:, None, None] + n1 * jnp.arange(n2, dtype=jnp.int32)[None, :, None])
    ang = ((k * jnp.arange(n2, dtype=jnp.int32)[None, None, :]) % n_lat).astype(F32) * (2.0 * math.pi / n_lat)
    c2, s2 = jnp.cos(ang), jnp.sin(ang)
    m2 = jnp.concatenate([jnp.concatenate([c2, s2], axis=2), jnp.concatenate([-s2, c2], axis=2)], axis=1)
    ach = _dft_angles(FOURIER_GROUP_DIM, FOURIER_GROUP_DIM, FOURIER_GROUP_DIM)
    eye = jnp.eye(FOURIER_GROUPS, dtype=F32)
    wch = jnp.concatenate([jnp.kron(eye, jnp.cos(ach)), jnp.kron(eye, jnp.sin(ach))], axis=0)
    actx = _dft_angles(n_ctx, n_ctx, n_ctx)
    return {
        "cos": _head_lanes(cos), "sin": _head_lanes(sin), "bd": bd,
        "qoff": jnp.zeros((1, LANES), F32).at[0, QK_HEAD_DIM].set(1.0),
        "voff": jnp.stack([(jnp.arange(LANES) >= V_HEAD_DIM), (jnp.arange(LANES) < V_HEAD_DIM)]).astype(F32),
        "cs1": cs1, "m2": m2.astype(BF16),
        "wch_lat": (wch * (n_lat * FOURIER_GROUP_DIM) ** -0.5).astype(BF16),
        "wch_ctx": (wch * (n_ctx * FOURIER_GROUP_DIM) ** -0.5).astype(BF16),
        "cs_ctx": jnp.concatenate([jnp.cos(actx), -jnp.sin(actx)], axis=0).astype(BF16),
    }


def _layer_weights(layer, p):
    q_rank = p["q_lora_norm"].shape[-1]
    kv_rank = p["kv_lora_norm"].shape[-1]
    o_pe = q_rank + kv_rank
    o_f = o_pe + QK_ROPE_DIM
    w_in = p["w_in"][layer]
    d = w_in.shape[0]
    w_pe = w_in[:, o_pe:o_f]
    z64 = jnp.zeros((d, QK_NOPE_DIM), F32)
    z32 = jnp.zeros((d, LANES - QK_HEAD_DIM), F32)
    w_in_cat = jnp.concatenate(
        [w_in[:, :o_pe], w_in[:, o_f:], z64, w_pe, z32, z64, w_pe[:, _pair_swap_index()], z32], axis=1)
    w_uq = p["w_uq"][layer].reshape(q_rank, N_HEADS, QK_HEAD_DIM)
    w_q = jnp.concatenate([_head_lanes(w_uq).reshape(q_rank, -1), _swap_rope(w_uq).reshape(q_rank, -1)], axis=1)
    w_ukv = p["w_ukv"][layer].reshape(kv_rank, N_HEADS, QK_NOPE_DIM + V_HEAD_DIM)
    w_k = jnp.pad(w_ukv[..., :QK_NOPE_DIM], ((0, 0), (0, 0), (0, LANES - QK_NOPE_DIM))).reshape(kv_rank, -1)
    w_v = w_ukv[..., QK_NOPE_DIM:]
    zv = jnp.zeros_like(w_v)
    odd = (jnp.arange(N_HEADS) % 2 == 1)[None, :, None]
    w_v = jnp.where(odd, jnp.concatenate([zv, w_v], axis=-1), jnp.concatenate([w_v, zv], axis=-1))
    w_v = w_v.reshape(kv_rank, -1)
    row = lambda v: v.reshape(1, -1)
    bound = LOG2E * (1.01 * QK_HEAD_DIM ** 0.5 * jnp.max(jnp.abs(p["q_norm"][layer]))
                     * jnp.max(jnp.abs(p["k_norm"][layer])) + 0.1)
    lw = {
        "g1": row(p["norm1"][layer]), "g2": row(p["norm2"][layer]),
        "w_in": w_in_cat.astype(BF16),
        "gq": row(p["q_lora_norm"][layer]), "gkv": row(p["kv_lora_norm"][layer]),
        "w_q": w_q.astype(BF16), "w_kv": jnp.concatenate([w_k, w_v], axis=1).astype(BF16),
        "bound": bound, "koff": jnp.zeros((1, LANES), F32).at[0, QK_HEAD_DIM].set(-bound),
        "qg": row(_head_lanes(p["q_norm"][layer])), "qg_sw": row(_swap_rope(p["q_norm"][layer])),
        "kg": row(_head_lanes(p["k_norm"][layer])), "kg_sw": row(_swap_rope(p["k_norm"][layer])),
        "ga": row(p["out_norm_attn"][layer]), "gf": row(p["out_norm_fourier"][layer]),
        "w_out": p["w_out"][layer].astype(BF16),
    }
    if layer % 2 == 0:
        lw.update(w_gate=p["w_ffn_gate"][layer // 2].astype(BF16), w_up=p["w_ffn_up"][layer // 2].astype(BF16),
                  w_down=p["w_ffn_down"][layer // 2].astype(BF16))
    else:
        wr = jnp.pad(p["w_router"][layer // 2], ((0, 0), (0, LANES - N_EXPERTS)))
        wr_hi, wr_lo = _split(wr)
        lw.update(wr_hi=wr_hi, wr_lo=wr_lo,
                  moe_index=layer // 2, w_moe_gate=p["w_moe_gate"], w_moe_up=p["w_moe_up"],
                  w_moe_down=p["w_moe_down"])
    return lw


def kernel(x, c, ctx, c_ctx, w_ada, b_ada, norm1, w_in, q_lora_norm, kv_lora_norm, w_uq, w_ukv, q_norm, k_norm,
           out_norm_attn, out_norm_fourier, w_out, norm2, w_ffn_gate, w_ffn_up, w_ffn_down, w_router,
           w_moe_gate, w_moe_up, w_moe_down):
    params = dict(norm1=norm1, w_in=w_in, q_lora_norm=q_lora_norm, kv_lora_norm=kv_lora_norm, w_uq=w_uq,
                  w_ukv=w_ukv, q_norm=q_norm, k_norm=k_norm, out_norm_attn=out_norm_attn,
                  out_norm_fourier=out_norm_fourier, w_out=w_out, norm2=norm2, w_ffn_gate=w_ffn_gate,
                  w_ffn_up=w_ffn_up, w_ffn_down=w_ffn_down, w_router=w_router, w_moe_gate=w_moe_gate,
                  w_moe_up=w_moe_up, w_moe_down=w_moe_down)
    bsz, n_lat, d = x.shape
    n_ctx = ctx.shape[1]
    depth = w_ada.shape[0]
    assert n_lat % Q_TILE == 0 and n_lat % n_ctx == 0 and n_ctx % ROW_TILE == 0
    assert n_lat % FFT_N2 == 0 and n_lat % GRID_W == 0 and n_ctx % FFT_N2 == 0

    cond = jnp.concatenate([c, c_ctx[None, :], jnp.zeros((8 - bsz - 1, d), F32)], axis=0)
    mod = _modulation(cond, w_ada, b_ada)
    tabs = _tables(n_lat, n_ctx)
    xt = (x, ctx)

    for layer in range(depth):
        last = layer == depth - 1
        lw = _layer_weights(layer, params)
        mod_l = mod[layer].reshape(mod.shape[1], 1, mod.shape[2])
        q, k, v, f = _input_projection(xt, mod_l, lw, tabs, n_lat)
        attn = _attention(q, k, v, lw["bound"], n_lat, not last)
        four = _fourier_mix(f, tabs, n_lat, not last)
        moe = layer % 2 == 1
        n_out = n_lat if last else n_lat + n_ctx
        if moe:
            xt, h2, rt = _merge(xt, attn, four, mod_l, lw, n_lat, True)
            xt = _moe_ffn(xt, h2, rt, mod_l, lw, n_lat, n_out)
        else:
            xt, h2 = _merge(xt, attn, four, mod_l, lw, n_lat, False)
            xt = _dense_ffn(xt, h2, mod_l, lw, n_lat, n_out)
    return xt[:, :n_lat]
```

```python
import functools
import math

import jax
import jax.numpy as jnp
from jax import lax
from jax.experimental import pallas as pl
from jax.experimental.pallas import tpu as pltpu

F32 = jnp.float32
BF16 = jnp.bfloat16

N_HEADS = 8
QK_NOPE_DIM = 64
QK_ROPE_DIM = 32
QK_HEAD_DIM = QK_NOPE_DIM + QK_ROPE_DIM
V_HEAD_DIM = 64
GRID_W = 64
ROPE_THETA = 10000.0
FOURIER_GROUPS = 4
FOURIER_GROUP_DIM = 128
N_EXPERTS = 8
TOP_K = 2
EPS = 1e-6
LOG2E = 1.4426950408889634
MAX_DIRECT_BOUND = 50.0

LANES = 128
MXU_DIM = 256
VMEM_LIMIT_BYTES = 60 * 1024 * 1024

ROW_TILE = 256
INPROJ_ROWS = 256
Q_TILE = 1024
KV_CHUNKS = (1408, 640, 512, 256)
FFT_N2 = 128
FFT_K1_GROUP = 8
EXPERT_ROWS = 512
FFN_CHUNK = 1408
MOE_CHUNK = 512
DMA_UNROLL = 8


def _params(**kw):
    return pltpu.CompilerParams(vmem_limit_bytes=VMEM_LIMIT_BYTES, **kw)


def _dot(a, b):
    return jnp.dot(a, b, preferred_element_type=F32)


def _split(a):
    hi = a.astype(BF16)
    lo = (a - hi.astype(F32)).astype(BF16)
    return hi, lo


def _rms(x):
    return x * lax.rsqrt(jnp.mean(x * x, axis=-1, keepdims=True) + EPS)


def _silu(x):
    return x / (1.0 + jnp.exp(-x))


def _mod_kernel(c_ref, w_ref, b_ref, o_ref):
    chi, clo = _split(_silu(c_ref[...]))
    whi, wlo = _split(w_ref[0])
    o_ref[0] = _dot(chi, whi) + _dot(clo, whi) + _dot(chi, wlo) + b_ref[0]


def _modulation(cond, w_ada, b_ada):
    depth, d, n = w_ada.shape
    tn = n // 4
    return pl.pallas_call(
        _mod_kernel,
        out_shape=jax.ShapeDtypeStruct((depth, cond.shape[0], n), F32),
        grid=(depth, n // tn),
        in_specs=[
            pl.BlockSpec(cond.shape, lambda l, j: (0, 0)),
            pl.BlockSpec((1, d, tn), lambda l, j: (l, 0, j)),
            pl.BlockSpec((1, 1, tn), lambda l, j: (l, 0, j)),
        ],
        out_specs=pl.BlockSpec((1, cond.shape[0], tn), lambda l, j: (l, 0, j)),
        compiler_params=_params(),
        name="adaln_modulation",
    )(cond, w_ada, b_ada.reshape(depth, 1, n))


def _token_specs(src, tm, n_lat_tiles):
    if not isinstance(src, tuple):
        return [pl.BlockSpec((1, tm, src.shape[-1]), lambda b, i: (b, i, 0))], [src]
    n = src[0].shape[-1]
    lat = pl.BlockSpec((1, tm, n), lambda b, i: (b, jnp.minimum(i, n_lat_tiles - 1), 0))
    ctx = pl.BlockSpec((1, tm, n), lambda b, i: (b, jnp.maximum(i - n_lat_tiles, 0), 0))
    return [lat, ctx], list(src)


def _token_tile(refs, n_lat_tiles):
    if len(refs) == 1:
        return refs[0][0]
    return jnp.where(pl.program_id(1) >= n_lat_tiles, refs[1][0], refs[0][0])


def _inproj_kernel(*refs, n_src, n_lat_tiles):
    x = _token_tile(refs[:n_src], n_lat_tiles)
    (mod_ref, g1_ref, win_ref, gq_ref, gkv_ref, wq_ref, wkv_ref, bd_ref, qg_ref, qgs_ref, kg_ref, kgs_ref,
     qoff_ref, koff_ref, voff_ref, cos_ref, sin_ref, q_ref, k_ref, v_ref, f_ref) = refs[n_src:]
    d = x.shape[-1]
    m = mod_ref[0]
    shift, scale = m[:, 0:d], m[:, d:2 * d]
    g1 = g1_ref[...] * (1.0 + scale)
    nq = gq_ref.shape[-1]
    nkv = gkv_ref.shape[-1]
    nf = f_ref.shape[-1]
    o_f, o_pe = nq + nkv, nq + nkv + nf
    hw = N_HEADS * LANES
    voff = voff_ref[...]
    bd = bd_ref[...]

    def pair(t):
        return jnp.concatenate([t, t], axis=1)

    def head_rsqrt(raw):
        ss = _dot((raw * raw).astype(BF16), bd)
        return lax.rsqrt(ss * (1.0 / QK_HEAD_DIM) + EPS)

    q_scale = QK_HEAD_DIM ** -0.5 * LOG2E
    qoff, koff = pair(qoff_ref[...]), pair(koff_ref[...])

    for r0 in range(0, x.shape[0], INPROJ_ROWS):
        rows = slice(r0, r0 + INPROJ_ROWS)
        h = _rms(x[rows]) * g1 + shift
        p = _dot(h.astype(BF16), win_ref[...])
        f_ref[0, rows] = p[:, o_f:o_pe].astype(BF16)
        cq = (_rms(p[:, 0:nq]) * gq_ref[...]).astype(BF16)
        ckv = (_rms(p[:, nq:o_f]) * gkv_ref[...]).astype(BF16)
        qq = _dot(cq, wq_ref[...])
        kv = _dot(ckv, wkv_ref[...])
        for hd in range(N_HEADS):
            v_ref[0, hd, rows] = (kv[:, hw + hd * LANES:hw + (hd + 1) * LANES]
                                  + voff[hd % 2:hd % 2 + 1]).astype(BF16)
        kpe2 = pair(p[:, o_pe:o_pe + LANES])
        kpe_sw2 = pair(p[:, o_pe + LANES:o_pe + 2 * LANES])
        cos, sin = cos_ref[rows], sin_ref[rows]
        q_c, q_s = pair(qg_ref[...] * cos), pair(qgs_ref[...] * sin)
        k_c, k_s = pair(kg_ref[...] * cos), pair(kgs_ref[...] * sin)
        for hp in range(N_HEADS // 2):
            lo_, hi_ = hp * 2 * LANES, (hp + 1) * 2 * LANES
            q_raw, q_sw = qq[:, lo_:hi_], qq[:, hw + lo_:hw + hi_]
            qo = (head_rsqrt(q_raw) * q_scale) * (q_raw * q_c + q_sw * q_s) + qoff
            q_ref[0, 2 * hp, rows] = qo[:, :LANES].astype(BF16)
            q_ref[0, 2 * hp + 1, rows] = qo[:, LANES:].astype(BF16)
            k_raw = kv[:, lo_:hi_] + kpe2
            ko = head_rsqrt(k_raw) * (k_raw * k_c + kpe_sw2 * k_s) + koff
            k_ref[0, 2 * hp, rows] = ko[:, :LANES].astype(BF16)
            k_ref[0, 2 * hp + 1, rows] = ko[:, LANES:].astype(BF16)


def _mod_spec(n_lat_tiles, n_mod):
    def index(b, i):
        return (jnp.where(i < n_lat_tiles, b, pl.num_programs(0)), 0, 0)
    return pl.BlockSpec((1, 1, n_mod), index)


def _const_spec(shape):
    zeros = (0,) * len(shape)
    return pl.BlockSpec(shape, lambda b, i: zeros)


def _input_projection(xt, mod_l, lw, tabs, n_lat):
    pair = isinstance(xt, tuple)
    bsz, d = (xt[0] if pair else xt).shape[0::2]
    stot = xt[0].shape[1] + xt[1].shape[1] if pair else xt.shape[1]
    tm = ROW_TILE
    nf = FOURIER_GROUPS * FOURIER_GROUP_DIM
    tok = lambda n: pl.BlockSpec((1, tm, n), lambda b, i: (b, i, 0))
    rope = pl.BlockSpec((tm, LANES), lambda b, i: (i, 0))
    head_out = pl.BlockSpec((1, N_HEADS, tm, LANES), lambda b, i: (b, 0, i, 0))
    x_specs, x_args = _token_specs(xt, tm, n_lat // tm)
    return pl.pallas_call(
        functools.partial(_inproj_kernel, n_src=len(x_args), n_lat_tiles=n_lat // tm),
        out_shape=(
            jax.ShapeDtypeStruct((bsz, N_HEADS, stot, LANES), BF16),
            jax.ShapeDtypeStruct((bsz, N_HEADS, stot, LANES), BF16),
            jax.ShapeDtypeStruct((bsz, N_HEADS, stot, LANES), BF16),
            jax.ShapeDtypeStruct((bsz, stot, nf), BF16),
        ),
        grid=(bsz, stot // tm),
        in_specs=x_specs + [
            _mod_spec(n_lat // tm, mod_l.shape[-1]), _const_spec((1, d)),
            _const_spec(lw["w_in"].shape), _const_spec(lw["gq"].shape), _const_spec(lw["gkv"].shape),
            _const_spec(lw["w_q"].shape), _const_spec(lw["w_kv"].shape), _const_spec(tabs["bd"].shape),
        ] + [_const_spec((1, LANES))] * 6 + [_const_spec((2, LANES)), rope, rope],
        out_specs=(head_out, head_out, head_out, tok(nf)),
        compiler_params=_params(),
        name="input_projection",
    )(*x_args, mod_l, lw["g1"], lw["w_in"], lw["gq"], lw["gkv"], lw["w_q"], lw["w_kv"], tabs["bd"],
      lw["qg"], lw["qg_sw"], lw["kg"], lw["kg_sw"], tabs["qoff"], lw["koff"], tabs["voff"],
      tabs["cos"], tabs["sin"])


def _scores(q, k_ref, hh, start, tk):
    k = k_ref[0, hh, pl.ds(start, tk), :]
    return lax.dot_general(q, k, (((1,), (1,)), ((), ())), preferred_element_type=F32)


def _attn_finish(accs, o_ref):
    even, odd = accs
    left = lax.broadcasted_iota(jnp.int32, even.shape, 1) < V_HEAD_DIM
    num = jnp.where(left, even, odd)
    den = pltpu.roll(jnp.where(left, odd, even), V_HEAD_DIM, axis=1)
    o_ref[0] = (num / den).astype(BF16)


def _attn_bounded_kernel(q_ref, k_ref, v_ref, o_ref, *, n_chunks, tk):
    tq = q_ref.shape[2]
    qs = [q_ref[0, hh] for hh in range(2)]

    def body(j, accs):
        start = pl.multiple_of(j * tk, tk)
        new = []
        for hh in range(2):
            p = jnp.exp2(_scores(qs[hh], k_ref, hh, start, tk)).astype(BF16)
            new.append(accs[hh] + _dot(p, v_ref[0, hh, pl.ds(start, tk), :]))
        return tuple(new)

    zero = jnp.zeros((tq, LANES), F32)
    unroll = 2 if n_chunks % 2 == 0 else 1
    _attn_finish(lax.fori_loop(0, n_chunks, body, (zero, zero), unroll=unroll), o_ref)


def _attn_online_kernel(q_ref, k_ref, v_ref, o_ref, *, n_chunks, tk):
    tq = q_ref.shape[2]
    qs = [q_ref[0, hh] for hh in range(2)]

    def body(j, carry):
        start = pl.multiple_of(j * tk, tk)
        new = []
        for hh in range(2):
            m, acc = carry[hh]
            s = _scores(qs[hh], k_ref, hh, start, tk)
            m_new = jnp.maximum(m, jnp.max(s, axis=-1, keepdims=True))
            p = jnp.exp2(s - m_new).astype(BF16)
            acc = jnp.exp2(m - m_new) * acc + _dot(p, v_ref[0, hh, pl.ds(start, tk), :])
            new.append((m_new, acc))
        return tuple(new)

    init = (jnp.full((tq, 1), -jnp.inf, F32), jnp.zeros((tq, LANES), F32))
    out = lax.fori_loop(0, n_chunks, body, (init, init))
    _attn_finish([out[0][1], out[1][1]], o_ref)


def _attention_calls(body, tag, q, k, v, n_lat, with_ctx):
    bsz, _, stot, _ = q.shape
    n_ctx = stot - n_lat
    hp = N_HEADS // 2
    tk = next(c for c in KV_CHUNKS if stot % c == 0)
    tq = Q_TILE
    nv = N_HEADS * V_HEAD_DIM
    kv_all = pl.BlockSpec((1, 2, stot, LANES), lambda b, h, i: (b, h, 0, 0))
    attn_lat = pl.pallas_call(
        functools.partial(body, n_chunks=stot // tk, tk=tk),
        out_shape=jax.ShapeDtypeStruct((bsz, n_lat, nv), BF16),
        grid=(bsz, hp, n_lat // tq),
        in_specs=[pl.BlockSpec((1, 2, tq, LANES), lambda b, h, i: (b, h, i, 0)), kv_all, kv_all],
        out_specs=pl.BlockSpec((1, tq, LANES), lambda b, h, i: (b, i, h)),
        compiler_params=_params(),
        name="latent_attention" + tag,
    )(q, k, v)
    if not with_ctx:
        return attn_lat
    cblk = n_lat // n_ctx
    ctx_rows = pl.BlockSpec((1, 2, n_ctx, LANES), lambda b, h: (b, h, cblk, 0))
    attn_ctx = pl.pallas_call(
        functools.partial(body, n_chunks=1, tk=n_ctx),
        out_shape=jax.ShapeDtypeStruct((bsz, n_ctx, nv), BF16),
        grid=(bsz, hp),
        in_specs=[ctx_rows, ctx_rows, ctx_rows],
        out_specs=pl.BlockSpec((1, n_ctx, LANES), lambda b, h: (b, 0, h)),
        compiler_params=_params(),
        name="context_attention" + tag,
    )(q, k, v)
    return attn_lat, attn_ctx


def _attention(q, k, v, bound, n_lat, with_ctx):
    out = lax.cond(
        bound <= MAX_DIRECT_BOUND,
        lambda: _attention_calls(_attn_bounded_kernel, "", q, k, v, n_lat, with_ctx),
        lambda: _attention_calls(_attn_online_kernel, "_online", q, k, v, n_lat, with_ctx))
    return out if with_ctx else (out, None)


def _fft1_kernel(cs_ref, x_ref, a_ref):
    a_ref[0] = _dot(cs_ref[...], x_ref[0]).astype(BF16)


def _fft2_kernel(ar_ref, ai_ref, m2_ref, wch_ref, o_ref):
    nf = wch_ref.shape[-1]
    n2 = ar_ref.shape[2]
    for j in range(ar_ref.shape[1]):
        slab = jnp.concatenate([ar_ref[0, j], ai_ref[0, j]], axis=0)
        z = _dot(m2_ref[j], slab)
        zc = jnp.concatenate([z[:n2], z[n2:]], axis=1).astype(BF16)
        o_ref[0, :, j * nf:(j + 1) * nf] = _dot(zc, wch_ref[...]).astype(BF16)


def _fft_ctx_kernel(cs_ref, x_ref, wch_ref, o_ref):
    n = x_ref.shape[1]
    z = _dot(cs_ref[...], x_ref[0])
    zc = jnp.concatenate([z[:n], z[n:]], axis=1).astype(BF16)
    o_ref[0] = _dot(zc, wch_ref[...]).astype(BF16)


def _fourier_mix(f, tabs, n_lat, with_ctx):
    bsz, stot, nf = f.shape
    n_ctx = stot - n_lat
    n2 = FFT_N2
    n1 = n_lat // n2
    kg = min(FFT_K1_GROUP, n1)
    tc = min(n2 * nf, 8192)
    a = pl.pallas_call(
        _fft1_kernel,
        out_shape=jax.ShapeDtypeStruct((bsz, 2 * n1, n2 * nf), BF16),
        grid=(bsz, n2 * nf // tc),
        in_specs=[
            pl.BlockSpec((2 * n1, n1), lambda b, j: (0, 0)),
            pl.BlockSpec((1, n1, tc), lambda b, j: (b, 0, j)),
        ],
        out_specs=pl.BlockSpec((1, 2 * n1, tc), lambda b, j: (b, 0, j)),
        compiler_params=_params(),
        name="fourier_stage1",
    )(tabs["cs1"], f.reshape(bsz, stot // n2, n2 * nf))
    a4 = a.reshape(bsz, 2 * n1, n2, nf)
    four_lat = pl.pallas_call(
        _fft2_kernel,
        out_shape=jax.ShapeDtypeStruct((bsz, n2, n1 * nf), BF16),
        grid=(bsz, n1 // kg),
        in_specs=[
            pl.BlockSpec((1, kg, n2, nf), lambda b, g: (b, g, 0, 0)),
            pl.BlockSpec((1, kg, n2, nf), lambda b, g: (b, n1 // kg + g, 0, 0)),
            pl.BlockSpec((kg, 2 * n2, 2 * n2), lambda b, g: (g, 0, 0)),
            pl.BlockSpec((2 * nf, nf), lambda b, g: (0, 0)),
        ],
        out_specs=pl.BlockSpec((1, n2, kg * nf), lambda b, g: (b, 0, g)),
        compiler_params=_params(),
        name="fourier_stage2",
    )(a4, a4, tabs["m2"], tabs["wch_lat"]).reshape(bsz, n_lat, nf)
    if not with_ctx:
        return four_lat, None
    cblk = n_lat // n_ctx
    four_ctx = pl.pallas_call(
        _fft_ctx_kernel,
        out_shape=jax.ShapeDtypeStruct((bsz, n_ctx, nf), BF16),
        grid=(bsz,),
        in_specs=[
            pl.BlockSpec((2 * n_ctx, n_ctx), lambda b: (0, 0)),
            pl.BlockSpec((1, n_ctx, nf), lambda b: (b, cblk, 0)),
            pl.BlockSpec((2 * nf, nf), lambda b: (0, 0)),
        ],
        out_specs=pl.BlockSpec((1, n_ctx, nf), lambda b: (b, 0, 0)),
        compiler_params=_params(),
        name="fourier_context",
    )(tabs["cs_ctx"], f, tabs["wch_ctx"])
    return four_lat, four_ctx


def _merge_kernel(*refs, moe, n_lat_tiles, n_srcs):
    tiles = []
    for n in n_srcs:
        tiles.append(_token_tile(refs[:n], n_lat_tiles))
        refs = refs[n:]
    x, a, f = tiles
    mod_ref, ga_ref, gf_ref, wout_ref, g2_ref = refs[:5]
    rest = refs[5:]
    d = x.shape[-1]
    m = mod_ref[0]
    gate1, shift2, scale2 = m[:, 2 * d:3 * d], m[:, 3 * d:4 * d], m[:, 4 * d:5 * d]
    an = _rms(a.astype(F32)) * ga_ref[...]
    fn = _rms(f.astype(F32)) * gf_ref[...]
    y = _dot(jnp.concatenate([an, fn], axis=1).astype(BF16), wout_ref[...])
    xn = x + gate1 * y
    h2 = _rms(xn) * (g2_ref[...] * (1.0 + scale2)) + shift2
    if not moe:
        xo_ref, h_ref = rest
        xo_ref[0] = xn
        h_ref[0] = h2.astype(BF16)
        return
    wr_hi_ref, wr_lo_ref, xo_ref, h_ref, rt_ref = rest
    xo_ref[0] = xn
    _to_slabs(h_ref, h2)
    hi, lo = _split(h2)
    logits = _dot(hi, wr_hi_ref[...]) + _dot(lo, wr_hi_ref[...]) + _dot(hi, wr_lo_ref[...])
    lane = lax.broadcasted_iota(jnp.int32, logits.shape, 1).astype(F32)
    lg = jnp.where(lane < N_EXPERTS, logits, -jnp.inf)
    m1 = jnp.max(lg, axis=-1, keepdims=True)
    i1 = jnp.min(jnp.where(lg == m1, lane, float(LANES)), axis=-1, keepdims=True)
    lg2 = jnp.where(lane == i1, -jnp.inf, lg)
    m2 = jnp.max(lg2, axis=-1, keepdims=True)
    i2 = jnp.min(jnp.where(lg2 == m2, lane, float(LANES)), axis=-1, keepdims=True)
    e = jnp.exp(m2 - m1)
    w1 = 1.0 / (1.0 + e)
    w2 = e * w1
    rt_ref[0] = jnp.where(lane == 0, i1, jnp.where(lane == 1, i2, jnp.where(lane == 2, w1,
                                                                          jnp.where(lane == 3, w2, 0.0))))


def _merge(xt, attn, four, mod_l, lw, n_lat, moe):
    tm = ROW_TILE
    with_ctx = attn[1] is not None
    n_lat_tiles = n_lat // tm
    n_out = n_lat + (attn[1].shape[1] if with_ctx else 0)
    if not with_ctx:
        attn, four = attn[0], four[0]
        xt = xt[0] if isinstance(xt, tuple) else xt
    bsz, d = (xt[0] if isinstance(xt, tuple) else xt).shape[0::2]
    tok = lambda n: pl.BlockSpec((1, tm, n), lambda b, i: (b, i, 0))
    na, nf = lw["ga"].shape[-1], lw["gf"].shape[-1]
    in_specs, args, n_srcs = [], [], []
    for src in (xt, attn, four):
        specs, arrs = _token_specs(src, tm, n_lat_tiles)
        in_specs += specs
        args += arrs
        n_srcs.append(len(arrs))
    in_specs += [_mod_spec(n_lat_tiles, mod_l.shape[-1]),
                 _const_spec((1, na)), _const_spec((1, nf)), _const_spec((na + nf, d)), _const_spec((1, d))]
    args += [mod_l, lw["ga"], lw["gf"], lw["w_out"], lw["g2"]]
    out_shape = [jax.ShapeDtypeStruct((bsz, n_out, d), F32), jax.ShapeDtypeStruct((bsz, n_out, d), BF16)]
    out_specs = [tok(d), tok(d)]
    if moe:
        slab, tiles = d // LANES, n_out // tm
        out_shape[1] = jax.ShapeDtypeStruct((bsz * n_out * slab, LANES), F32)
        out_specs[1] = pl.BlockSpec((tm * slab, LANES), lambda b, i: (b * tiles + i, 0))
        in_specs += [_const_spec((d, LANES)), _const_spec((d, LANES))]
        args += [lw["wr_hi"], lw["wr_lo"]]
        out_shape.append(jax.ShapeDtypeStruct((bsz, n_out, LANES), F32))
        out_specs.append(tok(LANES))
    return pl.pallas_call(
        functools.partial(_merge_kernel, moe=moe, n_lat_tiles=n_lat_tiles, n_srcs=tuple(n_srcs)),
        out_shape=tuple(out_shape),
        grid=(bsz, n_out // tm),
        in_specs=in_specs,
        out_specs=tuple(out_specs),
        compiler_params=_params(),
        name="merge_router" if moe else "merge",
    )(*args)


def _swiglu_chunks(h, wg_ref, wu_ref, wd_ref, chunk, lead=(), after_chunk=None):
    f = wg_ref.shape[-1]
    acc = None
    for c in range(f // chunk):
        sl = slice(c * chunk, (c + 1) * chunk)
        g = _dot(h, wg_ref[lead + (slice(None), sl)])
        u = _dot(h, wu_ref[lead + (slice(None), sl)])
        part = _dot((_silu(g) * u).astype(BF16), wd_ref[lead + (sl, slice(None))])
        acc = part if acc is None else acc + part
        if after_chunk is not None:
            after_chunk(c)
    return acc


def _ffn_kernel(x_ref, h_ref, mod_ref, wg_ref, wu_ref, wd_ref, o_ref):
    d = x_ref.shape[-1]
    gate2 = mod_ref[0][:, 5 * d:6 * d]
    y = _swiglu_chunks(h_ref[0], wg_ref, wu_ref, wd_ref, FFN_CHUNK)
    o_ref[0] = x_ref[0] + gate2 * y


def _dense_ffn(xt, h2, mod_l, lw, n_lat, n_out):
    bsz, _, d = xt.shape
    tm = ROW_TILE
    tok = lambda: pl.BlockSpec((1, tm, d), lambda b, i: (b, i, 0))
    return pl.pallas_call(
        _ffn_kernel,
        out_shape=jax.ShapeDtypeStruct((bsz, n_out, d), F32),
        grid=(bsz, n_out // tm),
        in_specs=[tok(), tok(), _mod_spec(n_lat // tm, mod_l.shape[-1]),
                  _const_spec(lw["w_gate"].shape), _const_spec(lw["w_up"].shape),
                  _const_spec(lw["w_down"].shape)],
        out_specs=tok(),
        compiler_params=_params(),
        name="dense_swiglu",
    )(xt, h2, mod_l, lw["w_gate"], lw["w_up"], lw["w_down"])


def _to_slabs(ref, val):
    rows, n = val.shape[0], val.shape[1] // LANES
    for s in range(n):
        ref[pl.ds(s, rows, stride=n), :] = val[:, s * LANES:(s + 1) * LANES]


def _from_slabs(ref, n):
    rows = ref.shape[0] // n
    return jnp.concatenate([ref[pl.ds(s, rows, stride=n), :] for s in range(n)], axis=1)


def _start_rows(n_rows, make_copy):
    def start(r, c):
        for j, cp in enumerate(make_copy(r)):
            cp.start(priority=j % 2)
        return c

    lax.fori_loop(0, n_rows, start, 0, unroll=DMA_UNROLL)


def _dispatch_kernel(zb_ref, dest_ref, h_ref, xs_ref, zbuf, zsems, sems, *, slab):
    tm = h_ref.shape[0] // slab

    @pl.when(pl.program_id(0) == 0)
    def _():
        zbuf[...] = jnp.zeros(zbuf.shape, zbuf.dtype)
        rows = zbuf.shape[0]

        def clear(i):
            start = pl.multiple_of(zb_ref[0, 0, i] * rows, rows)
            return pltpu.make_async_copy(zbuf, xs_ref.at[pl.ds(start, rows), :], zsems.at[i])

        for i in range(zsems.shape[0]):
            pl.when(zb_ref[0, 0, i] >= 0)(lambda i=i: clear(i).start())
        for i in range(zsems.shape[0]):
            pl.when(zb_ref[0, 0, i] >= 0)(lambda i=i: clear(i).wait())

    def copies(r):
        src = h_ref.at[pl.ds(pl.multiple_of(r * slab, slab), slab), :]
        return [pltpu.make_async_copy(
            src, xs_ref.at[pl.ds(pl.multiple_of(dest_ref[0, 0, TOP_K * r + j] * slab, slab), slab), :],
            sems.at[j]) for j in range(TOP_K)]

    _start_rows(tm, copies)
    for j in range(TOP_K):
        pltpu.make_async_copy(h_ref, xs_ref.at[pl.ds(0, tm * slab), :], sems.at[j]).wait()


def _dispatch(h2, dest, zero_blocks, n_rows, slab):
    tm = ROW_TILE
    n_tiles = h2.shape[0] // (tm * slab)
    nz = zero_blocks.shape[0]
    return pl.pallas_call(
        functools.partial(_dispatch_kernel, slab=slab),
        out_shape=jax.ShapeDtypeStruct((n_rows * slab, LANES), h2.dtype),
        grid=(n_tiles,),
        in_specs=[
            pl.BlockSpec((1, 1, nz), lambda n: (0, 0, 0), memory_space=pltpu.SMEM),
            pl.BlockSpec((1, 1, TOP_K * tm), lambda n: (n, 0, 0), memory_space=pltpu.SMEM),
            pl.BlockSpec((tm * slab, LANES), lambda n: (n, 0)),
        ],
        out_specs=pl.BlockSpec(memory_space=pl.ANY),
        scratch_shapes=[pltpu.VMEM((EXPERT_ROWS * slab, LANES), h2.dtype), pltpu.SemaphoreType.DMA((nz,)),
                        pltpu.SemaphoreType.DMA((TOP_K,))],
        compiler_params=_params(),
        name="moe_dispatch",
    )(zero_blocks.reshape(1, 1, nz), dest.reshape(n_tiles, 1, TOP_K * tm), h2)


def _expert_kernel(be_ref, nu_ref, x_ref, wg_hbm, wu_hbm, wd_hbm, o_ref, wg_s, wu_s, wd_s, st_g, st_u, st_d,
                   sems, *, slab, layer):
    n = pl.program_id(0)
    n_used = nu_ref[0]
    used = n < n_used
    e = be_ref[n]
    nxt = be_ref[jnp.minimum(n + 1, pl.num_programs(0) - 1)]
    hand_over = jnp.logical_and(n + 1 < n_used, nxt != e)
    n_chunks = wg_s.shape[1] // MOE_CHUNK

    def cols(c):
        return pl.ds(c * MOE_CHUNK, MOE_CHUNK)

    def copies(ex, c):
        slot = c % 2
        return [pltpu.make_async_copy(wg_hbm.at[layer, ex, :, cols(c)], st_g.at[slot], sems.at[0, slot]),
                pltpu.make_async_copy(wu_hbm.at[layer, ex, :, cols(c)], st_u.at[slot], sems.at[1, slot]),
                pltpu.make_async_copy(wd_hbm.at[layer, ex, cols(c), :], st_d.at[slot], sems.at[2, slot])]

    def start(ex, c):
        for cp in copies(ex, c):
            cp.start()

    def finish(ex, c):
        for cp in copies(ex, c):
            cp.wait()
        wg_s[:, cols(c)] = st_g[c % 2].astype(BF16)
        wu_s[:, cols(c)] = st_u[c % 2].astype(BF16)
        wd_s[cols(c), :] = st_d[c % 2].astype(BF16)

    @pl.when(n == 0)
    def _():
        start(e, 0)
        for c in range(n_chunks):
            if c + 1 < n_chunks:
                start(e, c + 1)
            finish(e, c)

    @pl.when(jnp.logical_and(used, jnp.logical_not(hand_over)))
    def _():
        x = _from_slabs(x_ref, slab).astype(BF16)
        _to_slabs(o_ref, _swiglu_chunks(x, wg_s, wu_s, wd_s, MOE_CHUNK))

    @pl.when(hand_over)
    def _():
        def replace(c):
            if c + 1 < n_chunks:
                start(nxt, c + 1)
            finish(nxt, c)

        x = _from_slabs(x_ref, slab).astype(BF16)
        start(nxt, 0)
        _to_slabs(o_ref, _swiglu_chunks(x, wg_s, wu_s, wd_s, MOE_CHUNK, after_chunk=replace))

    @pl.when(jnp.logical_not(used))
    def _():
        o_ref[...] = jnp.zeros(o_ref.shape, o_ref.dtype)


def _expert_blocks(xs, block_exp, n_used, lw, slab):
    tm = EXPERT_ROWS
    d, f = lw["w_moe_gate"].shape[-2:]
    rows = pl.BlockSpec((tm * slab, LANES), lambda n, be, nu: (n, 0))
    hbm = pl.BlockSpec(memory_space=pl.ANY)
    return pl.pallas_call(
        functools.partial(_expert_kernel, slab=slab, layer=lw["moe_index"]),
        out_shape=jax.ShapeDtypeStruct(xs.shape, F32),
        grid_spec=pltpu.PrefetchScalarGridSpec(
            num_scalar_prefetch=2,
            grid=(xs.shape[0] // (tm * slab),),
            in_specs=[rows, hbm, hbm, hbm],
            out_specs=rows,
            scratch_shapes=[pltpu.VMEM((d, f), BF16), pltpu.VMEM((d, f), BF16), pltpu.VMEM((f, d), BF16),
                            pltpu.VMEM((2, d, MOE_CHUNK), F32), pltpu.VMEM((2, d, MOE_CHUNK), F32),
                            pltpu.VMEM((2, MOE_CHUNK, d), F32), pltpu.SemaphoreType.DMA((3, 2))],
        ),
        compiler_params=_params(),
        name="moe_expert_blocks",
    )(block_exp, n_used, xs, lw["w_moe_gate"], lw["w_moe_up"], lw["w_moe_down"])


def _combine_kernel(pos_ref, pos_next_ref, ys_ref, x_ref, rt_ref, mod_ref, o_ref, buf, sems, *, slab):
    tm, d = x_ref.shape[1], x_ref.shape[2]
    step = pl.program_id(0) * pl.num_programs(1) + pl.program_id(1)
    n_steps = pl.num_programs(0) * pl.num_programs(1)
    slot = step % 2

    def copies(idx_ref, slot_):
        def make(r):
            return [pltpu.make_async_copy(
                ys_ref.at[pl.ds(pl.multiple_of(idx_ref[0, 0, TOP_K * r + j] * slab, slab), slab), :],
                buf.at[slot_, j, pl.ds(pl.multiple_of(r * slab, slab), slab), :],
                sems.at[slot_, j]) for j in range(TOP_K)]
        return make

    pl.when(step == 0)(lambda: _start_rows(tm, copies(pos_ref, slot)))
    pl.when(step + 1 < n_steps)(lambda: _start_rows(tm, copies(pos_next_ref, 1 - slot)))
    for j in range(TOP_K):
        pltpu.make_async_copy(ys_ref.at[pl.ds(0, tm * slab), :], buf.at[slot, j], sems.at[slot, j]).wait()
    gate2 = mod_ref[0][:, 5 * d:6 * d]
    rt = rt_ref[0]
    y = rt[:, 2:3] * _from_slabs(buf.at[slot, 0], slab) + rt[:, 3:4] * _from_slabs(buf.at[slot, 1], slab)
    o_ref[0] = x_ref[0] + gate2 * y


def _combine(ys, pos, xt, rt, mod_l, n_lat, n_out, slab):
    bsz, stot, d = xt.shape
    tm = ROW_TILE
    tiles = stot // tm
    tok = lambda n: pl.BlockSpec((1, tm, n), lambda b, i: (b, i, 0))
    last = bsz * tiles - 1
    pos = pos.reshape(bsz * tiles, 1, TOP_K * tm)
    return pl.pallas_call(
        functools.partial(_combine_kernel, slab=slab),
        out_shape=jax.ShapeDtypeStruct((bsz, n_out, d), F32),
        grid=(bsz, n_out // tm),
        in_specs=[
            pl.BlockSpec((1, 1, TOP_K * tm), lambda b, i: (b * tiles + i, 0, 0), memory_space=pltpu.SMEM),
            pl.BlockSpec((1, 1, TOP_K * tm), lambda b, i: (jnp.minimum(b * tiles + i + 1, last), 0, 0),
                         memory_space=pltpu.SMEM),
            pl.BlockSpec(memory_space=pl.ANY),
            tok(d), tok(LANES), _mod_spec(n_lat // tm, mod_l.shape[-1]),
        ],
        out_specs=tok(d),
        scratch_shapes=[pltpu.VMEM((2, TOP_K, tm * slab, LANES), F32), pltpu.SemaphoreType.DMA((2, TOP_K))],
        compiler_params=_params(),
        name="moe_combine",
    )(pos, pos, ys, xt, rt, mod_l)


def _moe_ffn(xt, h2, rt, mod_l, lw, n_lat, n_out):
    bsz, stot, d = xt.shape
    slab = d // LANES
    n_assign = bsz * stot * TOP_K
    tm = EXPERT_ROWS
    exp_flat = rt[..., :TOP_K].astype(jnp.int32).reshape(n_assign)
    onehot = (exp_flat[:, None] == jnp.arange(N_EXPERTS, dtype=jnp.int32)[None, :]).astype(jnp.int32)
    csum = jnp.cumsum(onehot, axis=0)
    rank = jnp.sum(jnp.where(onehot > 0, csum, 0), axis=1) - 1
    counts = csum[-1]
    padded = (counts + tm - 1) // tm * tm
    pad_ends = jnp.cumsum(padded)
    pad_starts = pad_ends - padded
    dest = (jnp.sum(onehot * pad_starts[None, :], axis=1) + rank).astype(jnp.int32)
    n_blocks = (n_assign + N_EXPERTS * (tm - 1) + tm - 1) // tm
    block_start = jnp.arange(n_blocks, dtype=jnp.int32) * tm
    block_exp = jnp.minimum(jnp.sum((pad_ends[None, :] <= block_start[:, None]).astype(jnp.int32), axis=1),
                            N_EXPERTS - 1).astype(jnp.int32)
    n_used = (pad_ends[-1:] // tm).astype(jnp.int32)
    last_blocks = jnp.where(padded > 0, pad_ends // tm - 1, -1)
    tail = n_used + jnp.arange(N_EXPERTS, dtype=jnp.int32)
    zero_blocks = jnp.concatenate([last_blocks, jnp.where(tail < n_blocks, tail, -1)]).astype(jnp.int32)

    xs = _dispatch(h2, dest, zero_blocks, n_blocks * tm, slab)
    ys = _expert_blocks(xs, block_exp, n_used, lw, slab)
    return _combine(ys, dest, xt, rt, mod_l, n_lat, n_out, slab)


def _pair_swap_index():
    j = jnp.arange(QK_ROPE_DIM)
    return jnp.where((j % 16) < 8, j + 8, j - 8)


def _head_lanes(v):
    pad = [(0, 0)] * (v.ndim - 1) + [(0, LANES - QK_HEAD_DIM)]
    return jnp.pad(v, pad)


def _swap_rope(v):
    rope = v[..., QK_NOPE_DIM:][..., _pair_swap_index()]
    return _head_lanes(jnp.concatenate([jnp.zeros_like(v[..., :QK_NOPE_DIM]), rope], axis=-1))


def _dft_angles(n_rows, n_cols, period):
    idx = (jnp.arange(n_rows, dtype=jnp.int32)[:, None] * jnp.arange(n_cols, dtype=jnp.int32)[None, :]) % period
    return idx.astype(F32) * (2.0 * math.pi / period)


def _tables(n_lat, n_ctx):
    nf = FOURIER_GROUPS * FOURIER_GROUP_DIM
    rows = n_lat // GRID_W
    r = jnp.repeat(jnp.arange(rows, dtype=F32), GRID_W)
    col = jnp.tile(jnp.arange(GRID_W, dtype=F32), rows)
    half = QK_ROPE_DIM // 2
    inv_freq = ROPE_THETA ** (-jnp.arange(0, half, 2, dtype=F32) / half)
    ar, ac = r[:, None] * inv_freq, col[:, None] * inv_freq
    ones = jnp.ones((n_lat, QK_NOPE_DIM), F32)
    cos = jnp.concatenate([ones, jnp.cos(ar), jnp.cos(ar), jnp.cos(ac), jnp.cos(ac)], axis=1)
    sin = jnp.concatenate([0 * ones, -jnp.sin(ar), jnp.sin(ar), -jnp.sin(ac), jnp.sin(ac)], axis=1)
    cos = jnp.concatenate([cos, jnp.ones((n_ctx, QK_HEAD_DIM), F32)], axis=0)
    sin = jnp.concatenate([sin, jnp.zeros((n_ctx, QK_HEAD_DIM), F32)], axis=0)
    lane = jnp.arange(2 * LANES)
    bd = (lane[:, None] // LANES == lane[None, :] // LANES).astype(BF16)
    n2 = FFT_N2
    n1 = n_lat // n2
    a1 = _dft_angles(n1, n1, n1)
    cs1 = jnp.concatenate([jnp.cos(a1), -jnp.sin(a1)], axis=0).astype(BF16)
    k = (jnp.arange(n1, dtype=jnp.int32)[:, None, None] + n1 * jnp.arange(n2, dtype=jnp.int32)[None, :, None])
    ang = ((k * jnp.arange(n2, dtype=jnp.int32)[None, None, :]) % n_lat).astype(F32) * (2.0 * math.pi / n_lat)
    c2, s2 = jnp.cos(ang), jnp.sin(ang)
    m2 = jnp.concatenate([jnp.concatenate([c2, s2], axis=2), jnp.concatenate([-s2, c2], axis=2)], axis=1)
    ach = _dft_angles(FOURIER_GROUP_DIM, FOURIER_GROUP_DIM, FOURIER_GROUP_DIM)
    eye = jnp.eye(FOURIER_GROUPS, dtype=F32)
    wch = jnp.concatenate([jnp.kron(eye, jnp.cos(ach)), jnp.kron(eye, jnp.sin(ach))], axis=0)
    actx = _dft_angles(n_ctx, n_ctx, n_ctx)
    return {
        "cos": _head_lanes(cos), "sin": _head_lanes(sin), "bd": bd,
        "qoff": jnp.zeros((1, LANES), F32).at[0, QK_HEAD_DIM].set(1.0),
        "voff": jnp.stack([(jnp.arange(LANES) >= V_HEAD_DIM), (jnp.arange(LANES) < V_HEAD_DIM)]).astype(F32),
        "cs1": cs1, "m2": m2.astype(BF16),
        "wch_lat": (wch * (n_lat * FOURIER_GROUP_DIM) ** -0.5).astype(BF16),
        "wch_ctx": (wch * (n_ctx * FOURIER_GROUP_DIM) ** -0.5).astype(BF16),
        "cs_ctx": jnp.concatenate([jnp.cos(actx), -jnp.sin(actx)], axis=0).astype(BF16),
    }


def _layer_weights(layer, p):
    q_rank = p["q_lora_norm"].shape[-1]
    kv_rank = p["kv_lora_norm"].shape[-1]
    o_pe = q_rank + kv_rank
    o_f = o_pe + QK_ROPE_DIM
    w_in = p["w_in"][layer]
    d = w_in.shape[0]
    w_pe = w_in[:, o_pe:o_f]
    z64 = jnp.zeros((d, QK_NOPE_DIM), F32)
    z32 = jnp.zeros((d, LANES - QK_HEAD_DIM), F32)
    w_in_cat = jnp.concatenate(
        [w_in[:, :o_pe], w_in[:, o_f:], z64, w_pe, z32, z64, w_pe[:, _pair_swap_index()], z32], axis=1)
    w_uq = p["w_uq"][layer].reshape(q_rank, N_HEADS, QK_HEAD_DIM)
    w_q = jnp.concatenate([_head_lanes(w_uq).reshape(q_rank, -1), _swap_rope(w_uq).reshape(q_rank, -1)], axis=1)
    w_ukv = p["w_ukv"][layer].reshape(kv_rank, N_HEADS, QK_NOPE_DIM + V_HEAD_DIM)
    w_k = jnp.pad(w_ukv[..., :QK_NOPE_DIM], ((0, 0), (0, 0), (0, LANES - QK_NOPE_DIM))).reshape(kv_rank, -1)
    w_v = w_ukv[..., QK_NOPE_DIM:]
    zv = jnp.zeros_like(w_v)
    odd = (jnp.arange(N_HEADS) % 2 == 1)[None, :, None]
    w_v = jnp.where(odd, jnp.concatenate([zv, w_v], axis=-1), jnp.concatenate([w_v, zv], axis=-1))
    w_v = w_v.reshape(kv_rank, -1)
    row = lambda v: v.reshape(1, -1)
    bound = LOG2E * (1.01 * QK_HEAD_DIM ** 0.5 * jnp.max(jnp.abs(p["q_norm"][layer]))
                     * jnp.max(jnp.abs(p["k_norm"][layer])) + 0.1)
    lw = {
        "g1": row(p["norm1"][layer]), "g2": row(p["norm2"][layer]),
        "w_in": w_in_cat.astype(BF16),
        "gq": row(p["q_lora_norm"][layer]), "gkv": row(p["kv_lora_norm"][layer]),
        "w_q": w_q.astype(BF16), "w_kv": jnp.concatenate([w_k, w_v], axis=1).astype(BF16),
        "bound": bound, "koff": jnp.zeros((1, LANES), F32).at[0, QK_HEAD_DIM].set(-bound),
        "qg": row(_head_lanes(p["q_norm"][layer])), "qg_sw": row(_swap_rope(p["q_norm"][layer])),
        "kg": row(_head_lanes(p["k_norm"][layer])), "kg_sw": row(_swap_rope(p["k_norm"][layer])),
        "ga": row(p["out_norm_attn"][layer]), "gf": row(p["out_norm_fourier"][layer]),
        "w_out": p["w_out"][layer].astype(BF16),
    }
    if layer % 2 == 0:
        lw.update(w_gate=p["w_ffn_gate"][layer // 2].astype(BF16), w_up=p["w_ffn_up"][layer // 2].astype(BF16),
                  w_down=p["w_ffn_down"][layer // 2].astype(BF16))
    else:
        wr = jnp.pad(p["w_router"][layer // 2], ((0, 0), (0, LANES - N_EXPERTS)))
        wr_hi, wr_lo = _split(wr)
        lw.update(wr_hi=wr_hi, wr_lo=wr_lo,
                  moe_index=layer // 2, w_moe_gate=p["w_moe_gate"], w_moe_up=p["w_moe_up"],
                  w_moe_down=p["w_moe_down"])
    return lw


def kernel(x, c, ctx, c_ctx, w_ada, b_ada, norm1, w_in, q_lora_norm, kv_lora_norm, w_uq, w_ukv, q_norm, k_norm,
           out_norm_attn, out_norm_fourier, w_out, norm2, w_ffn_gate, w_ffn_up, w_ffn_down, w_router,
           w_moe_gate, w_moe_up, w_moe_down):
    params = dict(norm1=norm1, w_in=w_in, q_lora_norm=q_lora_norm, kv_lora_norm=kv_lora_norm, w_uq=w_uq,
                  w_ukv=w_ukv, q_norm=q_norm, k_norm=k_norm, out_norm_attn=out_norm_attn,
                  out_norm_fourier=out_norm_fourier, w_out=w_out, norm2=norm2, w_ffn_gate=w_ffn_gate,
                  w_ffn_up=w_ffn_up, w_ffn_down=w_ffn_down, w_router=w_router, w_moe_gate=w_moe_gate,
                  w_moe_up=w_moe_up, w_moe_down=w_moe_down)
    bsz, n_lat, d = x.shape
    n_ctx = ctx.shape[1]
    depth = w_ada.shape[0]
    assert n_lat % Q_TILE == 0 and n_lat % n_ctx == 0 and n_ctx % ROW_TILE == 0
    assert n_lat % FFT_N2 == 0 and n_lat % GRID_W == 0 and n_ctx % FFT_N2 == 0

    cond = jnp.concatenate([c, c_ctx[None, :], jnp.zeros((8 - bsz - 1, d), F32)], axis=0)
    mod = _modulation(cond, w_ada, b_ada)
    tabs = _tables(n_lat, n_ctx)
    xt = (x, ctx)

    for layer in range(depth):
        last = layer == depth - 1
        lw = _layer_weights(layer, params)
        mod_l = mod[layer].reshape(mod.shape[1], 1, mod.shape[2])
        q, k, v, f = _input_projection(xt, mod_l, lw, tabs, n_lat)
        attn = _attention(q, k, v, lw["bound"], n_lat, not last)
        four = _fourier_mix(f, tabs, n_lat, not last)
        moe = layer % 2 == 1
        n_out = n_lat if last else n_lat + n_ctx
        if moe:
            xt, h2, rt = _merge(xt, attn, four, mod_l, lw, n_lat, True)
            xt = _moe_ffn(xt, h2, rt, mod_l, lw, n_lat, n_out)
        else:
            xt, h2 = _merge(xt, attn, four, mod_l, lw, n_lat, False)
            xt = _dense_ffn(xt, h2, mod_l, lw, n_lat, n_out)
    return xt[:, :n_lat]
```

```python
import functools
import math

import jax
import jax.numpy as jnp
from jax import lax
from jax.experimental import pallas as pl
from jax.experimental.pallas import tpu as pltpu

F32 = jnp.float32
BF16 = jnp.bfloat16

N_HEADS = 8
QK_NOPE_DIM = 64
QK_ROPE_DIM = 32
QK_HEAD_DIM = QK_NOPE_DIM + QK_ROPE_DIM
V_HEAD_DIM = 64
GRID_W = 64
ROPE_THETA = 10000.0
FOURIER_GROUPS = 4
FOURIER_GROUP_DIM = 128
N_EXPERTS = 8
TOP_K = 2
EPS = 1e-6
LOG2E = 1.4426950408889634
MAX_DIRECT_BOUND = 50.0

LANES = 128
MXU_DIM = 256
VMEM_LIMIT_BYTES = 60 * 1024 * 1024

ROW_TILE = 256
INPROJ_ROWS = 256
Q_TILE = 1024
KV_CHUNKS = (1408, 640, 512, 256)
FFT_N2 = 128
FFT_K1_GROUP = 8
EXPERT_ROWS = 512
FFN_CHUNK = 1408
MOE_CHUNK = 512
DMA_UNROLL = 8


def _params(**kw):
    return pltpu.CompilerParams(vmem_limit_bytes=VMEM_LIMIT_BYTES, **kw)


def _dot(a, b):
    return jnp.dot(a, b, preferred_element_type=F32)


def _split(a):
    hi = a.astype(BF16)
    lo = (a - hi.astype(F32)).astype(BF16)
    return hi, lo


def _rms(x):
    return x * lax.rsqrt(jnp.mean(x * x, axis=-1, keepdims=True) + EPS)


def _silu(x):
    return x / (1.0 + jnp.exp(-x))


def _mod_kernel(c_ref, w_ref, b_ref, o_ref):
    chi, clo = _split(_silu(c_ref[...]))
    whi, wlo = _split(w_ref[0])
    o_ref[0] = _dot(chi, whi) + _dot(clo, whi) + _dot(chi, wlo) + b_ref[0]


def _modulation(cond, w_ada, b_ada):
    depth, d, n = w_ada.shape
    tn = n // 4
    return pl.pallas_call(
        _mod_kernel,
        out_shape=jax.ShapeDtypeStruct((depth, cond.shape[0], n), F32),
        grid=(depth, n // tn),
        in_specs=[
            pl.BlockSpec(cond.shape, lambda l, j: (0, 0)),
            pl.BlockSpec((1, d, tn), lambda l, j: (l, 0, j)),
            pl.BlockSpec((1, 1, tn), lambda l, j: (l, 0, j)),
        ],
        out_specs=pl.BlockSpec((1, cond.shape[0], tn), lambda l, j: (l, 0, j)),
        compiler_params=_params(),
        name="adaln_modulation",
    )(cond, w_ada, b_ada.reshape(depth, 1, n))


def _token_specs(src, tm, n_lat_tiles):
    if not isinstance(src, tuple):
        return [pl.BlockSpec((1, tm, src.shape[-1]), lambda b, i: (b, i, 0))], [src]
    n = src[0].shape[-1]
    lat = pl.BlockSpec((1, tm, n), lambda b, i: (b, jnp.minimum(i, n_lat_tiles - 1), 0))
    ctx = pl.BlockSpec((1, tm, n), lambda b, i: (b, jnp.maximum(i - n_lat_tiles, 0), 0))
    return [lat, ctx], list(src)


def _token_tile(refs, n_lat_tiles):
    if len(refs) == 1:
        return refs[0][0]
    return jnp.where(pl.program_id(1) >= n_lat_tiles, refs[1][0], refs[0][0])


def _inproj_kernel(*refs, n_src, n_lat_tiles):
    x = _token_tile(refs[:n_src], n_lat_tiles)
    (mod_ref, g1_ref, win_ref, gq_ref, gkv_ref, wq_ref, wkv_ref, bd_ref, qg_ref, qgs_ref, kg_ref, kgs_ref,
     qoff_ref, koff_ref, voff_ref, cos_ref, sin_ref, q_ref, k_ref, v_ref, f_ref) = refs[n_src:]
    d = x.shape[-1]
    m = mod_ref[0]
    shift, scale = m[:, 0:d], m[:, d:2 * d]
    g1 = g1_ref[...] * (1.0 + scale)
    nq = gq_ref.shape[-1]
    nkv = gkv_ref.shape[-1]
    nf = f_ref.shape[-1]
    o_f, o_pe = nq + nkv, nq + nkv + nf
    hw = N_HEADS * LANES
    voff = voff_ref[...]
    bd = bd_ref[...]

    def pair(t):
        return jnp.concatenate([t, t], axis=1)

    def head_rsqrt(raw):
        ss = _dot((raw * raw).astype(BF16), bd)
        return lax.rsqrt(ss * (1.0 / QK_HEAD_DIM) + EPS)

    q_scale = QK_HEAD_DIM ** -0.5 * LOG2E
    qoff, koff = pair(qoff_ref[...]), pair(koff_ref[...])

    for r0 in range(0, x.shape[0], INPROJ_ROWS):
        rows = slice(r0, r0 + INPROJ_ROWS)
        h = _rms(x[rows]) * g1 + shift
        p = _dot(h.astype(BF16), win_ref[...])
        f_ref[0, rows] = p[:, o_f:o_pe].astype(BF16)
        cq = (_rms(p[:, 0:nq]) * gq_ref[...]).astype(BF16)
        ckv = (_rms(p[:, nq:o_f]) * gkv_ref[...]).astype(BF16)
        qq = _dot(cq, wq_ref[...])
        kv = _dot(ckv, wkv_ref[...])
        for hd in range(N_HEADS):
            v_ref[0, hd, rows] = (kv[:, hw + hd * LANES:hw + (hd + 1) * LANES]
                                  + voff[hd % 2:hd % 2 + 1]).astype(BF16)
        kpe2 = pair(p[:, o_pe:o_pe + LANES])
        kpe_sw2 = pair(p[:, o_pe + LANES:o_pe + 2 * LANES])
        cos, sin = cos_ref[rows], sin_ref[rows]
        q_c, q_s = pair(qg_ref[...] * cos), pair(qgs_ref[...] * sin)
        k_c, k_s = pair(kg_ref[...] * cos), pair(kgs_ref[...] * sin)
        for hp in range(N_HEADS // 2):
            lo_, hi_ = hp * 2 * LANES, (hp + 1) * 2 * LANES
            q_raw, q_sw = qq[:, lo_:hi_], qq[:, hw + lo_:hw + hi_]
            qo = (head_rsqrt(q_raw) * q_scale) * (q_raw * q_c + q_sw * q_s) + qoff
            q_ref[0, 2 * hp, rows] = qo[:, :LANES].astype(BF16)
            q_ref[0, 2 * hp + 1, rows] = qo[:, LANES:].astype(BF16)
            k_raw = kv[:, lo_:hi_] + kpe2
            ko = head_rsqrt(k_raw) * (k_raw * k_c + kpe_sw2 * k_s) + koff
            k_ref[0, 2 * hp, rows] = ko[:, :LANES].astype(BF16)
            k_ref[0, 2 * hp + 1, rows] = ko[:, LANES:].astype(BF16)


def _mod_spec(n_lat_tiles, n_mod):
    def index(b, i):
        return (jnp.where(i < n_lat_tiles, b, pl.num_programs(0)), 0, 0)
    return pl.BlockSpec((1, 1, n_mod), index)


def _const_spec(shape):
    zeros = (0,) * len(shape)
    return pl.BlockSpec(shape, lambda b, i: zeros)


def _input_projection(xt, mod_l, lw, tabs, n_lat):
    pair = isinstance(xt, tuple)
    bsz, d = (xt[0] if pair else xt).shape[0::2]
    stot = xt[0].shape[1] + xt[1].shape[1] if pair else xt.shape[1]
    tm = ROW_TILE
    nf = FOURIER_GROUPS * FOURIER_GROUP_DIM
    tok = lambda n: pl.BlockSpec((1, tm, n), lambda b, i: (b, i, 0))
    rope = pl.BlockSpec((tm, LANES), lambda b, i: (i, 0))
    head_out = pl.BlockSpec((1, N_HEADS, tm, LANES), lambda b, i: (b, 0, i, 0))
    x_specs, x_args = _token_specs(xt, tm, n_lat // tm)
    return pl.pallas_call(
        functools.partial(_inproj_kernel, n_src=len(x_args), n_lat_tiles=n_lat // tm),
        out_shape=(
            jax.ShapeDtypeStruct((bsz, N_HEADS, stot, LANES), BF16),
            jax.ShapeDtypeStruct((bsz, N_HEADS, stot, LANES), BF16),
            jax.ShapeDtypeStruct((bsz, N_HEADS, stot, LANES), BF16),
            jax.ShapeDtypeStruct((bsz, stot, nf), BF16),
        ),
        grid=(bsz, stot // tm),
        in_specs=x_specs + [
            _mod_spec(n_lat // tm, mod_l.shape[-1]), _const_spec((1, d)),
            _const_spec(lw["w_in"].shape), _const_spec(lw["gq"].shape), _const_spec(lw["gkv"].shape),
            _const_spec(lw["w_q"].shape), _const_spec(lw["w_kv"].shape), _const_spec(tabs["bd"].shape),
        ] + [_const_spec((1, LANES))] * 6 + [_const_spec((2, LANES)), rope, rope],
        out_specs=(head_out, head_out, head_out, tok(nf)),
        compiler_params=_params(),
        name="input_projection",
    )(*x_args, mod_l, lw["g1"], lw["w_in"], lw["gq"], lw["gkv"], lw["w_q"], lw["w_kv"], tabs["bd"],
      lw["qg"], lw["qg_sw"], lw["kg"], lw["kg_sw"], tabs["qoff"], lw["koff"], tabs["voff"],
      tabs["cos"], tabs["sin"])


def _scores(q, k_ref, hh, start, tk):
    k = k_ref[0, hh, pl.ds(start, tk), :]
    return lax.dot_general(q, k, (((1,), (1,)), ((), ())), preferred_element_type=F32)


def _attn_finish(accs, o_ref):
    even, odd = accs
    left = lax.broadcasted_iota(jnp.int32, even.shape, 1) < V_HEAD_DIM
    num = jnp.where(left, even, odd)
    den = pltpu.roll(jnp.where(left, odd, even), V_HEAD_DIM, axis=1)
    o_ref[0] = (num / den).astype(BF16)


def _attn_bounded_kernel(q_ref, k_ref, v_ref, o_ref, *, n_chunks, tk):
    tq = q_ref.shape[2]
    qs = [q_ref[0, hh] for hh in range(2)]

    def body(j, accs):
        start = pl.multiple_of(j * tk, tk)
        new = []
        for hh in range(2):
            p = jnp.exp2(_scores(qs[hh], k_ref, hh, start, tk)).astype(BF16)
            new.append(accs[hh] + _dot(p, v_ref[0, hh, pl.ds(start, tk), :]))
        return tuple(new)

    zero = jnp.zeros((tq, LANES), F32)
    unroll = 2 if n_chunks % 2 == 0 else 1
    _attn_finish(lax.fori_loop(0, n_chunks, body, (zero, zero), unroll=unroll), o_ref)


def _attn_online_kernel(q_ref, k_ref, v_ref, o_ref, *, n_chunks, tk):
    tq = q_ref.shape[2]
    qs = [q_ref[0, hh] for hh in range(2)]

    def body(j, carry):
        start = pl.multiple_of(j * tk, tk)
        new = []
        for hh in range(2):
            m, acc = carry[hh]
            s = _scores(qs[hh], k_ref, hh, start, tk)
            m_new = jnp.maximum(m, jnp.max(s, axis=-1, keepdims=True))
            p = jnp.exp2(s - m_new).astype(BF16)
            acc = jnp.exp2(m - m_new) * acc + _dot(p, v_ref[0, hh, pl.ds(start, tk), :])
            new.append((m_new, acc))
        return tuple(new)

    init = (jnp.full((tq, 1), -jnp.inf, F32), jnp.zeros((tq, LANES), F32))
    out = lax.fori_loop(0, n_chunks, body, (init, init))
    _attn_finish([out[0][1], out[1][1]], o_ref)


def _attention_calls(body, tag, q, k, v, n_lat, with_ctx):
    bsz, _, stot, _ = q.shape
    n_ctx = stot - n_lat
    hp = N_HEADS // 2
    tk = next(c for c in KV_CHUNKS if stot % c == 0)
    tq = Q_TILE
    nv = N_HEADS * V_HEAD_DIM
    kv_all = pl.BlockSpec((1, 2, stot, LANES), lambda b, h, i: (b, h, 0, 0))
    attn_lat = pl.pallas_call(
        functools.partial(body, n_chunks=stot // tk, tk=tk),
        out_shape=jax.ShapeDtypeStruct((bsz, n_lat, nv), BF16),
        grid=(bsz, hp, n_lat // tq),
        in_specs=[pl.BlockSpec((1, 2, tq, LANES), lambda b, h, i: (b, h, i, 0)), kv_all, kv_all],
        out_specs=pl.BlockSpec((1, tq, LANES), lambda b, h, i: (b, i, h)),
        compiler_params=_params(),
        name="latent_attention" + tag,
    )(q, k, v)
    if not with_ctx:
        return attn_lat
    cblk = n_lat // n_ctx
    ctx_rows = pl.BlockSpec((1, 2, n_ctx, LANES), lambda b, h: (b, h, cblk, 0))
    attn_ctx = pl.pallas_call(
        functools.partial(body, n_chunks=1, tk=n_ctx),
        out_shape=jax.ShapeDtypeStruct((bsz, n_ctx, nv), BF16),
        grid=(bsz, hp),
        in_specs=[ctx_rows, ctx_rows, ctx_rows],
        out_specs=pl.BlockSpec((1, n_ctx, LANES), lambda b, h: (b, 0, h)),
        compiler_params=_params(),
        name="context_attention" + tag,
    )(q, k, v)
    return attn_lat, attn_ctx


def _attention(q, k, v, bound, n_lat, with_ctx):
    out = lax.cond(
        bound <= MAX_DIRECT_BOUND,
        lambda: _attention_calls(_attn_bounded_kernel, "", q, k, v, n_lat, with_ctx),
        lambda: _attention_calls(_attn_online_kernel, "_online", q, k, v, n_lat, with_ctx))
    return out if with_ctx else (out, None)


def _fft1_kernel(cs_ref, x_ref, a_ref):
    a_ref[0] = _dot(cs_ref[...], x_ref[0]).astype(BF16)


def _fft2_kernel(ar_ref, ai_ref, m2_ref, wch_ref, o_ref):
    nf = wch_ref.shape[-1]
    n2 = ar_ref.shape[2]
    for j in range(ar_ref.shape[1]):
        slab = jnp.concatenate([ar_ref[0, j], ai_ref[0, j]], axis=0)
        z = _dot(m2_ref[j], slab)
        zc = jnp.concatenate([z[:n2], z[n2:]], axis=1).astype(BF16)
        o_ref[0, :, j * nf:(j + 1) * nf] = _dot(zc, wch_ref[...]).astype(BF16)


def _fft_ctx_kernel(cs_ref, x_ref, wch_ref, o_ref):
    n = x_ref.shape[1]
    z = _dot(cs_ref[...], x_ref[0])
    zc = jnp.concatenate([z[:n], z[n:]], axis=1).astype(BF16)
    o_ref[0] = _dot(zc, wch_ref[...]).astype(BF16)


def _fourier_mix(f, tabs, n_lat, with_ctx):
    bsz, stot, nf = f.shape
    n_ctx = stot - n_lat
    n2 = FFT_N2
    n1 = n_lat // n2
    kg = min(FFT_K1_GROUP, n1)
    tc = min(n2 * nf, 8192)
    a = pl.pallas_call(
        _fft1_kernel,
        out_shape=jax.ShapeDtypeStruct((bsz, 2 * n1, n2 * nf), BF16),
        grid=(bsz, n2 * nf // tc),
        in_specs=[
            pl.BlockSpec((2 * n1, n1), lambda b, j: (0, 0)),
            pl.BlockSpec((1, n1, tc), lambda b, j: (b, 0, j)),
        ],
        out_specs=pl.BlockSpec((1, 2 * n1, tc), lambda b, j: (b, 0, j)),
        compiler_params=_params(),
        name="fourier_stage1",
    )(tabs["cs1"], f.reshape(bsz, stot // n2, n2 * nf))
    a4 = a.reshape(bsz, 2 * n1, n2, nf)
    four_lat = pl.pallas_call(
        _fft2_kernel,
        out_shape=jax.ShapeDtypeStruct((bsz, n2, n1 * nf), BF16),
        grid=(bsz, n1 // kg),
        in_specs=[
            pl.BlockSpec((1, kg, n2, nf), lambda b, g: (b, g, 0, 0)),
            pl.BlockSpec((1, kg, n2, nf), lambda b, g: (b, n1 // kg + g, 0, 0)),
            pl.BlockSpec((kg, 2 * n2, 2 * n2), lambda b, g: (g, 0, 0)),
            pl.BlockSpec((2 * nf, nf), lambda b, g: (0, 0)),
        ],
        out_specs=pl.BlockSpec((1, n2, kg * nf), lambda b, g: (b, 0, g)),
        compiler_params=_params(),
        name="fourier_stage2",
    )(a4, a4, tabs["m2"], tabs["wch_lat"]).reshape(bsz, n_lat, nf)
    if not with_ctx:
        return four_lat, None
    cblk = n_lat // n_ctx
    four_ctx = pl.pallas_call(
        _fft_ctx_kernel,
        out_shape=jax.ShapeDtypeStruct((bsz, n_ctx, nf), BF16),
        grid=(bsz,),
        in_specs=[
            pl.BlockSpec((2 * n_ctx, n_ctx), lambda b: (0, 0)),
            pl.BlockSpec((1, n_ctx, nf), lambda b: (b, cblk, 0)),
            pl.BlockSpec((2 * nf, nf), lambda b: (0, 0)),
        ],
        out_specs=pl.BlockSpec((1, n_ctx, nf), lambda b: (b, 0, 0)),
        compiler_params=_params(),
        name="fourier_context",
    )(tabs["cs_ctx"], f, tabs["wch_ctx"])
    return four_lat, four_ctx


def _merge_kernel(*refs, moe, n_lat_tiles, n_srcs):
    tiles = []
    for n in n_srcs:
        tiles.append(_token_tile(refs[:n], n_lat_tiles))
        refs = refs[n:]
    x, a, f = tiles
    mod_ref, ga_ref, gf_ref, wout_ref, g2_ref = refs[:5]
    rest = refs[5:]
    d = x.shape[-1]
    m = mod_ref[0]
    gate1, shift2, scale2 = m[:, 2 * d:3 * d], m[:, 3 * d:4 * d], m[:, 4 * d:5 * d]
    an = _rms(a.astype(F32)) * ga_ref[...]
    fn = _rms(f.astype(F32)) * gf_ref[...]
    y = _dot(jnp.concatenate([an, fn], axis=1).astype(BF16), wout_ref[...])
    xn = x + gate1 * y
    h2 = _rms(xn) * (g2_ref[...] * (1.0 + scale2)) + shift2
    if not moe:
        wg_ref, wu_ref, wd_ref, xo_ref = rest
        gate2 = m[:, 5 * d:6 * d]
        xo_ref[0] = xn + gate2 * _swiglu_chunks(h2.astype(BF16), wg_ref, wu_ref, wd_ref, FFN_CHUNK)
        return
    wr_hi_ref, wr_lo_ref, xo_ref, h_ref, rt_ref = rest
    xo_ref[0] = xn
    _to_slabs(h_ref, h2)
    hi, lo = _split(h2)
    logits = _dot(hi, wr_hi_ref[...]) + _dot(lo, wr_hi_ref[...]) + _dot(hi, wr_lo_ref[...])
    lane = lax.broadcasted_iota(jnp.int32, logits.shape, 1).astype(F32)
    lg = jnp.where(lane < N_EXPERTS, logits, -jnp.inf)
    m1 = jnp.max(lg, axis=-1, keepdims=True)
    i1 = jnp.min(jnp.where(lg == m1, lane, float(LANES)), axis=-1, keepdims=True)
    lg2 = jnp.where(lane == i1, -jnp.inf, lg)
    m2 = jnp.max(lg2, axis=-1, keepdims=True)
    i2 = jnp.min(jnp.where(lg2 == m2, lane, float(LANES)), axis=-1, keepdims=True)
    e = jnp.exp(m2 - m1)
    w1 = 1.0 / (1.0 + e)
    w2 = e * w1
    rt_ref[0] = jnp.where(lane == 0, i1, jnp.where(lane == 1, i2, jnp.where(lane == 2, w1,
                                                                          jnp.where(lane == 3, w2, 0.0))))


def _merge(xt, attn, four, mod_l, lw, n_lat, moe):
    tm = ROW_TILE
    with_ctx = attn[1] is not None
    n_lat_tiles = n_lat // tm
    n_out = n_lat + (attn[1].shape[1] if with_ctx else 0)
    if not with_ctx:
        attn, four = attn[0], four[0]
        xt = xt[0] if isinstance(xt, tuple) else xt
    bsz, d = (xt[0] if isinstance(xt, tuple) else xt).shape[0::2]
    tok = lambda n: pl.BlockSpec((1, tm, n), lambda b, i: (b, i, 0))
    na, nf = lw["ga"].shape[-1], lw["gf"].shape[-1]
    in_specs, args, n_srcs = [], [], []
    for src in (xt, attn, four):
        specs, arrs = _token_specs(src, tm, n_lat_tiles)
        in_specs += specs
        args += arrs
        n_srcs.append(len(arrs))
    in_specs += [_mod_spec(n_lat_tiles, mod_l.shape[-1]),
                 _const_spec((1, na)), _const_spec((1, nf)), _const_spec((na + nf, d)), _const_spec((1, d))]
    args += [mod_l, lw["ga"], lw["gf"], lw["w_out"], lw["g2"]]
    out_shape = [jax.ShapeDtypeStruct((bsz, n_out, d), F32)]
    out_specs = [tok(d)]
    if moe:
        slab, tiles = d // LANES, n_out // tm
        in_specs += [_const_spec((d, LANES)), _const_spec((d, LANES))]
        args += [lw["wr_hi"], lw["wr_lo"]]
        out_shape += [jax.ShapeDtypeStruct((bsz * n_out * slab, LANES), F32),
                      jax.ShapeDtypeStruct((bsz, n_out, LANES), F32)]
        out_specs += [pl.BlockSpec((tm * slab, LANES), lambda b, i: (b * tiles + i, 0)), tok(LANES)]
    else:
        in_specs += [_const_spec(lw[k].shape) for k in ("w_gate", "w_up", "w_down")]
        args += [lw["w_gate"], lw["w_up"], lw["w_down"]]
    return pl.pallas_call(
        functools.partial(_merge_kernel, moe=moe, n_lat_tiles=n_lat_tiles, n_srcs=tuple(n_srcs)),
        out_shape=tuple(out_shape),
        grid=(bsz, n_out // tm),
        in_specs=in_specs,
        out_specs=tuple(out_specs),
        compiler_params=_params(),
        name="merge_router" if moe else "merge_swiglu",
    )(*args)


def _swiglu_chunks(h, wg_ref, wu_ref, wd_ref, chunk, lead=(), after_chunk=None):
    f = wg_ref.shape[-1]
    acc = None
    for c in range(f // chunk):
        sl = slice(c * chunk, (c + 1) * chunk)
        g = _dot(h, wg_ref[lead + (slice(None), sl)])
        u = _dot(h, wu_ref[lead + (slice(None), sl)])
        part = _dot((_silu(g) * u).astype(BF16), wd_ref[lead + (sl, slice(None))])
        acc = part if acc is None else acc + part
        if after_chunk is not None:
            after_chunk(c)
    return acc


def _to_slabs(ref, val):
    rows, n = val.shape[0], val.shape[1] // LANES
    for s in range(n):
        ref[pl.ds(s, rows, stride=n), :] = val[:, s * LANES:(s + 1) * LANES]


def _from_slabs(ref, n):
    rows = ref.shape[0] // n
    return jnp.concatenate([ref[pl.ds(s, rows, stride=n), :] for s in range(n)], axis=1)


def _start_rows(n_rows, make_copy):
    def start(r, c):
        for j, cp in enumerate(make_copy(r)):
            cp.start(priority=j % 2)
        return c

    lax.fori_loop(0, n_rows, start, 0, unroll=DMA_UNROLL)


def _dispatch_kernel(zb_ref, dest_ref, h_ref, xs_ref, zbuf, zsems, sems, *, slab):
    tm = h_ref.shape[0] // slab

    @pl.when(pl.program_id(0) == 0)
    def _():
        zbuf[...] = jnp.zeros(zbuf.shape, zbuf.dtype)
        rows = zbuf.shape[0]

        def clear(i):
            start = pl.multiple_of(zb_ref[0, 0, i] * rows, rows)
            return pltpu.make_async_copy(zbuf, xs_ref.at[pl.ds(start, rows), :], zsems.at[i])

        for i in range(zsems.shape[0]):
            pl.when(zb_ref[0, 0, i] >= 0)(lambda i=i: clear(i).start())
        for i in range(zsems.shape[0]):
            pl.when(zb_ref[0, 0, i] >= 0)(lambda i=i: clear(i).wait())

    def copies(r):
        src = h_ref.at[pl.ds(pl.multiple_of(r * slab, slab), slab), :]
        return [pltpu.make_async_copy(
            src, xs_ref.at[pl.ds(pl.multiple_of(dest_ref[0, 0, TOP_K * r + j] * slab, slab), slab), :],
            sems.at[j]) for j in range(TOP_K)]

    _start_rows(tm, copies)
    for j in range(TOP_K):
        pltpu.make_async_copy(h_ref, xs_ref.at[pl.ds(0, tm * slab), :], sems.at[j]).wait()


def _dispatch(h2, dest, zero_blocks, n_rows, slab):
    tm = ROW_TILE
    n_tiles = h2.shape[0] // (tm * slab)
    nz = zero_blocks.shape[0]
    return pl.pallas_call(
        functools.partial(_dispatch_kernel, slab=slab),
        out_shape=jax.ShapeDtypeStruct((n_rows * slab, LANES), h2.dtype),
        grid=(n_tiles,),
        in_specs=[
            pl.BlockSpec((1, 1, nz), lambda n: (0, 0, 0), memory_space=pltpu.SMEM),
            pl.BlockSpec((1, 1, TOP_K * tm), lambda n: (n, 0, 0), memory_space=pltpu.SMEM),
            pl.BlockSpec((tm * slab, LANES), lambda n: (n, 0)),
        ],
        out_specs=pl.BlockSpec(memory_space=pl.ANY),
        scratch_shapes=[pltpu.VMEM((EXPERT_ROWS * slab, LANES), h2.dtype), pltpu.SemaphoreType.DMA((nz,)),
                        pltpu.SemaphoreType.DMA((TOP_K,))],
        compiler_params=_params(),
        name="moe_dispatch",
    )(zero_blocks.reshape(1, 1, nz), dest.reshape(n_tiles, 1, TOP_K * tm), h2)


def _expert_kernel(be_ref, nu_ref, x_ref, wg_hbm, wu_hbm, wd_hbm, o_ref, wg_s, wu_s, wd_s, st_g, st_u, st_d,
                   sems, *, slab, layer):
    n = pl.program_id(0)
    n_used = nu_ref[0]
    used = n < n_used
    e = be_ref[n]
    nxt = be_ref[jnp.minimum(n + 1, pl.num_programs(0) - 1)]
    hand_over = jnp.logical_and(n + 1 < n_used, nxt != e)
    n_chunks = wg_s.shape[1] // MOE_CHUNK

    def cols(c):
        return pl.ds(c * MOE_CHUNK, MOE_CHUNK)

    def copies(ex, c):
        slot = c % 2
        return [pltpu.make_async_copy(wg_hbm.at[layer, ex, :, cols(c)], st_g.at[slot], sems.at[0, slot]),
                pltpu.make_async_copy(wu_hbm.at[layer, ex, :, cols(c)], st_u.at[slot], sems.at[1, slot]),
                pltpu.make_async_copy(wd_hbm.at[layer, ex, cols(c), :], st_d.at[slot], sems.at[2, slot])]

    def start(ex, c):
        for cp in copies(ex, c):
            cp.start()

    def finish(ex, c):
        for cp in copies(ex, c):
            cp.wait()
        wg_s[:, cols(c)] = st_g[c % 2].astype(BF16)
        wu_s[:, cols(c)] = st_u[c % 2].astype(BF16)
        wd_s[cols(c), :] = st_d[c % 2].astype(BF16)

    @pl.when(n == 0)
    def _():
        start(e, 0)
        for c in range(n_chunks):
            if c + 1 < n_chunks:
                start(e, c + 1)
            finish(e, c)

    @pl.when(jnp.logical_and(used, jnp.logical_not(hand_over)))
    def _():
        x = _from_slabs(x_ref, slab).astype(BF16)
        _to_slabs(o_ref, _swiglu_chunks(x, wg_s, wu_s, wd_s, MOE_CHUNK))

    @pl.when(hand_over)
    def _():
        def replace(c):
            if c + 1 < n_chunks:
                start(nxt, c + 1)
            finish(nxt, c)

        x = _from_slabs(x_ref, slab).astype(BF16)
        start(nxt, 0)
        _to_slabs(o_ref, _swiglu_chunks(x, wg_s, wu_s, wd_s, MOE_CHUNK, after_chunk=replace))

    @pl.when(jnp.logical_not(used))
    def _():
        o_ref[...] = jnp.zeros(o_ref.shape, o_ref.dtype)


def _expert_blocks(xs, block_exp, n_used, lw, slab):
    tm = EXPERT_ROWS
    d, f = lw["w_moe_gate"].shape[-2:]
    rows = pl.BlockSpec((tm * slab, LANES), lambda n, be, nu: (n, 0))
    hbm = pl.BlockSpec(memory_space=pl.ANY)
    return pl.pallas_call(
        functools.partial(_expert_kernel, slab=slab, layer=lw["moe_index"]),
        out_shape=jax.ShapeDtypeStruct(xs.shape, F32),
        grid_spec=pltpu.PrefetchScalarGridSpec(
            num_scalar_prefetch=2,
            grid=(xs.shape[0] // (tm * slab),),
            in_specs=[rows, hbm, hbm, hbm],
            out_specs=rows,
            scratch_shapes=[pltpu.VMEM((d, f), BF16), pltpu.VMEM((d, f), BF16), pltpu.VMEM((f, d), BF16),
                            pltpu.VMEM((2, d, MOE_CHUNK), F32), pltpu.VMEM((2, d, MOE_CHUNK), F32),
                            pltpu.VMEM((2, MOE_CHUNK, d), F32), pltpu.SemaphoreType.DMA((3, 2))],
        ),
        compiler_params=_params(),
        name="moe_expert_blocks",
    )(block_exp, n_used, xs, lw["w_moe_gate"], lw["w_moe_up"], lw["w_moe_down"])


def _combine_kernel(pos_ref, pos_next_ref, ys_ref, x_ref, rt_ref, mod_ref, o_ref, buf, sems, *, slab):
    tm, d = x_ref.shape[1], x_ref.shape[2]
    step = pl.program_id(0) * pl.num_programs(1) + pl.program_id(1)
    n_steps = pl.num_programs(0) * pl.num_programs(1)
    slot = step % 2

    def copies(idx_ref, slot_):
        def make(r):
            return [pltpu.make_async_copy(
                ys_ref.at[pl.ds(pl.multiple_of(idx_ref[0, 0, TOP_K * r + j] * slab, slab), slab), :],
                buf.at[slot_, j, pl.ds(pl.multiple_of(r * slab, slab), slab), :],
                sems.at[slot_, j]) for j in range(TOP_K)]
        return make

    pl.when(step == 0)(lambda: _start_rows(tm, copies(pos_ref, slot)))
    pl.when(step + 1 < n_steps)(lambda: _start_rows(tm, copies(pos_next_ref, 1 - slot)))
    for j in range(TOP_K):
        pltpu.make_async_copy(ys_ref.at[pl.ds(0, tm * slab), :], buf.at[slot, j], sems.at[slot, j]).wait()
    gate2 = mod_ref[0][:, 5 * d:6 * d]
    rt = rt_ref[0]
    y = rt[:, 2:3] * _from_slabs(buf.at[slot, 0], slab) + rt[:, 3:4] * _from_slabs(buf.at[slot, 1], slab)
    o_ref[0] = x_ref[0] + gate2 * y


def _combine(ys, pos, xt, rt, mod_l, n_lat, n_out, slab):
    bsz, stot, d = xt.shape
    tm = ROW_TILE
    tiles = stot // tm
    tok = lambda n: pl.BlockSpec((1, tm, n), lambda b, i: (b, i, 0))
    last = bsz * tiles - 1
    pos = pos.reshape(bsz * tiles, 1, TOP_K * tm)
    return pl.pallas_call(
        functools.partial(_combine_kernel, slab=slab),
        out_shape=jax.ShapeDtypeStruct((bsz, n_out, d), F32),
        grid=(bsz, n_out // tm),
        in_specs=[
            pl.BlockSpec((1, 1, TOP_K * tm), lambda b, i: (b * tiles + i, 0, 0), memory_space=pltpu.SMEM),
            pl.BlockSpec((1, 1, TOP_K * tm), lambda b, i: (jnp.minimum(b * tiles + i + 1, last), 0, 0),
                         memory_space=pltpu.SMEM),
            pl.BlockSpec(memory_space=pl.ANY),
            tok(d), tok(LANES), _mod_spec(n_lat // tm, mod_l.shape[-1]),
        ],
        out_specs=tok(d),
        scratch_shapes=[pltpu.VMEM((2, TOP_K, tm * slab, LANES), F32), pltpu.SemaphoreType.DMA((2, TOP_K))],
        compiler_params=_params(),
        name="moe_combine",
    )(pos, pos, ys, xt, rt, mod_l)


def _moe_ffn(xt, h2, rt, mod_l, lw, n_lat, n_out):
    bsz, stot, d = xt.shape
    slab = d // LANES
    n_assign = bsz * stot * TOP_K
    tm = EXPERT_ROWS
    exp_flat = rt[..., :TOP_K].astype(jnp.int32).reshape(n_assign)
    onehot = (exp_flat[:, None] == jnp.arange(N_EXPERTS, dtype=jnp.int32)[None, :]).astype(jnp.int32)
    csum = jnp.cumsum(onehot, axis=0)
    rank = jnp.sum(jnp.where(onehot > 0, csum, 0), axis=1) - 1
    counts = csum[-1]
    padded = (counts + tm - 1) // tm * tm
    pad_ends = jnp.cumsum(padded)
    pad_starts = pad_ends - padded
    dest = (jnp.sum(onehot * pad_starts[None, :], axis=1) + rank).astype(jnp.int32)
    n_blocks = (n_assign + N_EXPERTS * (tm - 1) + tm - 1) // tm
    block_start = jnp.arange(n_blocks, dtype=jnp.int32) * tm
    block_exp = jnp.minimum(jnp.sum((pad_ends[None, :] <= block_start[:, None]).astype(jnp.int32), axis=1),
                            N_EXPERTS - 1).astype(jnp.int32)
    n_used = (pad_ends[-1:] // tm).astype(jnp.int32)
    last_blocks = jnp.where(padded > 0, pad_ends // tm - 1, -1)
    tail = n_used + jnp.arange(N_EXPERTS, dtype=jnp.int32)
    zero_blocks = jnp.concatenate([last_blocks, jnp.where(tail < n_blocks, tail, -1)]).astype(jnp.int32)

    xs = _dispatch(h2, dest, zero_blocks, n_blocks * tm, slab)
    ys = _expert_blocks(xs, block_exp, n_used, lw, slab)
    return _combine(ys, dest, xt, rt, mod_l, n_lat, n_out, slab)


def _pair_swap_index():
    j = jnp.arange(QK_ROPE_DIM)
    return jnp.where((j % 16) < 8, j + 8, j - 8)


def _head_lanes(v):
    pad = [(0, 0)] * (v.ndim - 1) + [(0, LANES - QK_HEAD_DIM)]
    return jnp.pad(v, pad)


def _swap_rope(v):
    rope = v[..., QK_NOPE_DIM:][..., _pair_swap_index()]
    return _head_lanes(jnp.concatenate([jnp.zeros_like(v[..., :QK_NOPE_DIM]), rope], axis=-1))


def _dft_angles(n_rows, n_cols, period):
    idx = (jnp.arange(n_rows, dtype=jnp.int32)[:, None] * jnp.arange(n_cols, dtype=jnp.int32)[None, :]) % period
    return idx.astype(F32) * (2.0 * math.pi / period)


def _tables(n_lat, n_ctx):
    nf = FOURIER_GROUPS * FOURIER_GROUP_DIM
    rows = n_lat // GRID_W
    r = jnp.repeat(jnp.arange(rows, dtype=F32), GRID_W)
    col = jnp.tile(jnp.arange(GRID_W, dtype=F32), rows)
    half = QK_ROPE_DIM // 2
    inv_freq = ROPE_THETA ** (-jnp.arange(0, half, 2, dtype=F32) / half)
    ar, ac = r[:, None] * inv_freq, col[:, None] * inv_freq
    ones = jnp.ones((n_lat, QK_NOPE_DIM), F32)
    cos = jnp.concatenate([ones, jnp.cos(ar), jnp.cos(ar), jnp.cos(ac), jnp.cos(ac)], axis=1)
    sin = jnp.concatenate([0 * ones, -jnp.sin(ar), jnp.sin(ar), -jnp.sin(ac), jnp.sin(ac)], axis=1)
    cos = jnp.concatenate([cos, jnp.ones((n_ctx, QK_HEAD_DIM), F32)], axis=0)
    sin = jnp.concatenate([sin, jnp.zeros((n_ctx, QK_HEAD_DIM), F32)], axis=0)
    lane = jnp.arange(2 * LANES)
    bd = (lane[:, None] // LANES == lane[None, :] // LANES).astype(BF16)
    n2 = FFT_N2
    n1 = n_lat // n2
    a1 = _dft_angles(n1, n1, n1)
    cs1 = jnp.concatenate([jnp.cos(a1), -jnp.sin(a1)], axis=0).astype(BF16)
    k = (jnp.arange(n1, dtype=jnp.int32)[:, None, None] + n1 * jnp.arange(n2, dtype=jnp.int32)[None, :, None])
    ang = ((k * jnp.arange(n2, dtype=jnp.int32)[None, None, :]) % n_lat).astype(F32) * (2.0 * math.pi / n_lat)
    c2, s2 = jnp.cos(ang), jnp.sin(ang)
    m2 = jnp.concatenate([jnp.concatenate([c2, s2], axis=2), jnp.concatenate([-s2, c2], axis=2)], axis=1)
    ach = _dft_angles(FOURIER_GROUP_DIM, FOURIER_GROUP_DIM, FOURIER_GROUP_DIM)
    eye = jnp.eye(FOURIER_GROUPS, dtype=F32)
    wch = jnp.concatenate([jnp.kron(eye, jnp.cos(ach)), jnp.kron(eye, jnp.sin(ach))], axis=0)
    actx = _dft_angles(n_ctx, n_ctx, n_ctx)
    return {
        "cos": _head_lanes(cos), "sin": _head_lanes(sin), "bd": bd,
        "qoff": jnp.zeros((1, LANES), F32).at[0, QK_HEAD_DIM].set(1.0),
        "voff": jnp.stack([(jnp.arange(LANES) >= V_HEAD_DIM), (jnp.arange(LANES) < V_HEAD_DIM)]).astype(F32),
        "cs1": cs1, "m2": m2.astype(BF16),
        "wch_lat": (wch * (n_lat * FOURIER_GROUP_DIM) ** -0.5).astype(BF16),
        "wch_ctx": (wch * (n_ctx * FOURIER_GROUP_DIM) ** -0.5).astype(BF16),
        "cs_ctx": jnp.concatenate([jnp.cos(actx), -jnp.sin(actx)], axis=0).astype(BF16),
    }


def _layer_weights(layer, p):
    q_rank = p["q_lora_norm"].shape[-1]
    kv_rank = p["kv_lora_norm"].shape[-1]
    o_pe = q_rank + kv_rank
    o_f = o_pe + QK_ROPE_DIM
    w_in = p["w_in"][layer]
    d = w_in.shape[0]
    w_pe = w_in[:, o_pe:o_f]
    z64 = jnp.zeros((d, QK_NOPE_DIM), F32)
    z32 = jnp.zeros((d, LANES - QK_HEAD_DIM), F32)
    w_in_cat = jnp.concatenate(
        [w_in[:, :o_pe], w_in[:, o_f:], z64, w_pe, z32, z64, w_pe[:, _pair_swap_index()], z32], axis=1)
    w_uq = p["w_uq"][layer].reshape(q_rank, N_HEADS, QK_HEAD_DIM)
    w_q = jnp.concatenate([_head_lanes(w_uq).reshape(q_rank, -1), _swap_rope(w_uq).reshape(q_rank, -1)], axis=1)
    w_ukv = p["w_ukv"][layer].reshape(kv_rank, N_HEADS, QK_NOPE_DIM + V_HEAD_DIM)
    w_k = jnp.pad(w_ukv[..., :QK_NOPE_DIM], ((0, 0), (0, 0), (0, LANES - QK_NOPE_DIM))).reshape(kv_rank, -1)
    w_v = w_ukv[..., QK_NOPE_DIM:]
    zv = jnp.zeros_like(w_v)
    odd = (jnp.arange(N_HEADS) % 2 == 1)[None, :, None]
    w_v = jnp.where(odd, jnp.concatenate([zv, w_v], axis=-1), jnp.concatenate([w_v, zv], axis=-1))
    w_v = w_v.reshape(kv_rank, -1)
    row = lambda v: v.reshape(1, -1)
    bound = LOG2E * (1.01 * QK_HEAD_DIM ** 0.5 * jnp.max(jnp.abs(p["q_norm"][layer]))
                     * jnp.max(jnp.abs(p["k_norm"][layer])) + 0.1)
    lw = {
        "g1": row(p["norm1"][layer]), "g2": row(p["norm2"][layer]),
        "w_in": w_in_cat.astype(BF16),
        "gq": row(p["q_lora_norm"][layer]), "gkv": row(p["kv_lora_norm"][layer]),
        "w_q": w_q.astype(BF16), "w_kv": jnp.concatenate([w_k, w_v], axis=1).astype(BF16),
        "bound": bound, "koff": jnp.zeros((1, LANES), F32).at[0, QK_HEAD_DIM].set(-bound),
        "qg": row(_head_lanes(p["q_norm"][layer])), "qg_sw": row(_swap_rope(p["q_norm"][layer])),
        "kg": row(_head_lanes(p["k_norm"][layer])), "kg_sw": row(_swap_rope(p["k_norm"][layer])),
        "ga": row(p["out_norm_attn"][layer]), "gf": row(p["out_norm_fourier"][layer]),
        "w_out": p["w_out"][layer].astype(BF16),
    }
    if layer % 2 == 0:
        lw.update(w_gate=p["w_ffn_gate"][layer // 2].astype(BF16), w_up=p["w_ffn_up"][layer // 2].astype(BF16),
                  w_down=p["w_ffn_down"][layer // 2].astype(BF16))
    else:
        wr = jnp.pad(p["w_router"][layer // 2], ((0, 0), (0, LANES - N_EXPERTS)))
        wr_hi, wr_lo = _split(wr)
        lw.update(wr_hi=wr_hi, wr_lo=wr_lo,
                  moe_index=layer // 2, w_moe_gate=p["w_moe_gate"], w_moe_up=p["w_moe_up"],
                  w_moe_down=p["w_moe_down"])
    return lw


def kernel(x, c, ctx, c_ctx, w_ada, b_ada, norm1, w_in, q_lora_norm, kv_lora_norm, w_uq, w_ukv, q_norm, k_norm,
           out_norm_attn, out_norm_fourier, w_out, norm2, w_ffn_gate, w_ffn_up, w_ffn_down, w_router,
           w_moe_gate, w_moe_up, w_moe_down):
    params = dict(norm1=norm1, w_in=w_in, q_lora_norm=q_lora_norm, kv_lora_norm=kv_lora_norm, w_uq=w_uq,
                  w_ukv=w_ukv, q_norm=q_norm, k_norm=k_norm, out_norm_attn=out_norm_attn,
                  out_norm_fourier=out_norm_fourier, w_out=w_out, norm2=norm2, w_ffn_gate=w_ffn_gate,
                  w_ffn_up=w_ffn_up, w_ffn_down=w_ffn_down, w_router=w_router, w_moe_gate=w_moe_gate,
                  w_moe_up=w_moe_up, w_moe_down=w_moe_down)
    bsz, n_lat, d = x.shape
    n_ctx = ctx.shape[1]
    depth = w_ada.shape[0]
    assert n_lat % Q_TILE == 0 and n_lat % n_ctx == 0 and n_ctx % ROW_TILE == 0
    assert n_lat % FFT_N2 == 0 and n_lat % GRID_W == 0 and n_ctx % FFT_N2 == 0

    cond = jnp.concatenate([c, c_ctx[None, :], jnp.zeros((8 - bsz - 1, d), F32)], axis=0)
    mod = _modulation(cond, w_ada, b_ada)
    tabs = _tables(n_lat, n_ctx)
    xt = (x, ctx)

    for layer in range(depth):
        last = layer == depth - 1
        lw = _layer_weights(layer, params)
        mod_l = mod[layer].reshape(mod.shape[1], 1, mod.shape[2])
        q, k, v, f = _input_projection(xt, mod_l, lw, tabs, n_lat)
        attn = _attention(q, k, v, lw["bound"], n_lat, not last)
        four = _fourier_mix(f, tabs, n_lat, not last)
        moe = layer % 2 == 1
        n_out = n_lat if last else n_lat + n_ctx
        if moe:
            xt, h2, rt = _merge(xt, attn, four, mod_l, lw, n_lat, True)
            xt = _moe_ffn(xt, h2, rt, mod_l, lw, n_lat, n_out)
        else:
            (xt,) = _merge(xt, attn, four, mod_l, lw, n_lat, False)
    return xt[:, :n_lat]
```

```python
import functools
import math

import jax
import jax.numpy as jnp
from jax import lax
from jax.experimental import pallas as pl
from jax.experimental.pallas import tpu as pltpu

F32 = jnp.float32
BF16 = jnp.bfloat16

N_HEADS = 8
QK_NOPE_DIM = 64
QK_ROPE_DIM = 32
QK_HEAD_DIM = QK_NOPE_DIM + QK_ROPE_DIM
V_HEAD_DIM = 64
GRID_W = 64
ROPE_THETA = 10000.0
FOURIER_GROUPS = 4
FOURIER_GROUP_DIM = 128
N_EXPERTS = 8
TOP_K = 2
EPS = 1e-6
LOG2E = 1.4426950408889634
MAX_DIRECT_BOUND = 50.0

LANES = 128
MXU_DIM = 256
VMEM_LIMIT_BYTES = 60 * 1024 * 1024

ROW_TILE = 256
INPROJ_ROWS = 256
Q_TILE = 2048
KV_CHUNKS = (1408, 640, 512, 256)
FFT_N2 = 128
FFT_K1_GROUP = 8
EXPERT_ROWS = 512
FFN_CHUNK = 1408
MOE_CHUNK = 512
DMA_UNROLL = 8


def _params(**kw):
    return pltpu.CompilerParams(vmem_limit_bytes=VMEM_LIMIT_BYTES, **kw)


def _dot(a, b):
    return jnp.dot(a, b, preferred_element_type=F32)


def _split(a):
    hi = a.astype(BF16)
    lo = (a - hi.astype(F32)).astype(BF16)
    return hi, lo


def _rms(x):
    return x * lax.rsqrt(jnp.mean(x * x, axis=-1, keepdims=True) + EPS)


def _silu(x):
    return x / (1.0 + jnp.exp(-x))


def _mod_kernel(c_ref, w_ref, b_ref, o_ref):
    chi, clo = _split(_silu(c_ref[...]))
    whi, wlo = _split(w_ref[0])
    o_ref[0] = _dot(chi, whi) + _dot(clo, whi) + _dot(chi, wlo) + b_ref[0]


def _modulation(cond, w_ada, b_ada):
    depth, d, n = w_ada.shape
    tn = n // 4
    return pl.pallas_call(
        _mod_kernel,
        out_shape=jax.ShapeDtypeStruct((depth, cond.shape[0], n), F32),
        grid=(depth, n // tn),
        in_specs=[
            pl.BlockSpec(cond.shape, lambda l, j: (0, 0)),
            pl.BlockSpec((1, d, tn), lambda l, j: (l, 0, j)),
            pl.BlockSpec((1, 1, tn), lambda l, j: (l, 0, j)),
        ],
        out_specs=pl.BlockSpec((1, cond.shape[0], tn), lambda l, j: (l, 0, j)),
        compiler_params=_params(),
        name="adaln_modulation",
    )(cond, w_ada, b_ada.reshape(depth, 1, n))


def _token_specs(src, tm, n_lat_tiles):
    if not isinstance(src, tuple):
        return [pl.BlockSpec((1, tm, src.shape[-1]), lambda b, i: (b, i, 0))], [src]
    n = src[0].shape[-1]
    lat = pl.BlockSpec((1, tm, n), lambda b, i: (b, jnp.minimum(i, n_lat_tiles - 1), 0))
    ctx = pl.BlockSpec((1, tm, n), lambda b, i: (b, jnp.maximum(i - n_lat_tiles, 0), 0))
    return [lat, ctx], list(src)


def _token_tile(refs, n_lat_tiles):
    if len(refs) == 1:
        return refs[0][0]
    return jnp.where(pl.program_id(1) >= n_lat_tiles, refs[1][0], refs[0][0])


def _inproj_kernel(*refs, n_src, n_lat_tiles):
    x = _token_tile(refs[:n_src], n_lat_tiles)
    (mod_ref, g1_ref, win_ref, gq_ref, gkv_ref, wq_ref, wkv_ref, bd_ref, qg_ref, qgs_ref, kg_ref, kgs_ref,
     qoff_ref, koff_ref, voff_ref, cos_ref, sin_ref, q_ref, k_ref, v_ref, f_ref) = refs[n_src:]
    d = x.shape[-1]
    m = mod_ref[0]
    shift, scale = m[:, 0:d], m[:, d:2 * d]
    g1 = g1_ref[...] * (1.0 + scale)
    nq = gq_ref.shape[-1]
    nkv = gkv_ref.shape[-1]
    nf = f_ref.shape[-1]
    o_f, o_pe = nq + nkv, nq + nkv + nf
    hw = N_HEADS * LANES
    voff = voff_ref[...]
    bd = bd_ref[...]

    def pair(t):
        return jnp.concatenate([t, t], axis=1)

    def head_rsqrt(raw):
        ss = _dot((raw * raw).astype(BF16), bd)
        return lax.rsqrt(ss * (1.0 / QK_HEAD_DIM) + EPS)

    q_scale = QK_HEAD_DIM ** -0.5 * LOG2E
    qoff, koff = pair(qoff_ref[...]), pair(koff_ref[...])

    for r0 in range(0, x.shape[0], INPROJ_ROWS):
        rows = slice(r0, r0 + INPROJ_ROWS)
        h = _rms(x[rows]) * g1 + shift
        p = _dot(h.astype(BF16), win_ref[...])
        f_ref[0, rows] = p[:, o_f:o_pe].astype(BF16)
        cq = (_rms(p[:, 0:nq]) * gq_ref[...]).astype(BF16)
        ckv = (_rms(p[:, nq:o_f]) * gkv_ref[...]).astype(BF16)
        qq = _dot(cq, wq_ref[...])
        kv = _dot(ckv, wkv_ref[...])
        for hd in range(N_HEADS):
            v_ref[0, hd, rows] = (kv[:, hw + hd * LANES:hw + (hd + 1) * LANES]
                                  + voff[hd % 2:hd % 2 + 1]).astype(BF16)
        kpe2 = pair(p[:, o_pe:o_pe + LANES])
        kpe_sw2 = pair(p[:, o_pe + LANES:o_pe + 2 * LANES])
        cos, sin = cos_ref[rows], sin_ref[rows]
        q_c, q_s = pair(qg_ref[...] * cos), pair(qgs_ref[...] * sin)
        k_c, k_s = pair(kg_ref[...] * cos), pair(kgs_ref[...] * sin)
        for hp in range(N_HEADS // 2):
            lo_, hi_ = hp * 2 * LANES, (hp + 1) * 2 * LANES
            q_raw, q_sw = qq[:, lo_:hi_], qq[:, hw + lo_:hw + hi_]
            qo = (head_rsqrt(q_raw) * q_scale) * (q_raw * q_c + q_sw * q_s) + qoff
            q_ref[0, 2 * hp, rows] = qo[:, :LANES].astype(BF16)
            q_ref[0, 2 * hp + 1, rows] = qo[:, LANES:].astype(BF16)
            k_raw = kv[:, lo_:hi_] + kpe2
            ko = head_rsqrt(k_raw) * (k_raw * k_c + kpe_sw2 * k_s) + koff
            k_ref[0, 2 * hp, rows] = ko[:, :LANES].astype(BF16)
            k_ref[0, 2 * hp + 1, rows] = ko[:, LANES:].astype(BF16)


def _mod_spec(n_lat_tiles, n_mod):
    def index(b, i):
        return (jnp.where(i < n_lat_tiles, b, pl.num_programs(0)), 0, 0)
    return pl.BlockSpec((1, 1, n_mod), index)


def _const_spec(shape):
    zeros = (0,) * len(shape)
    return pl.BlockSpec(shape, lambda b, i: zeros)


def _input_projection(xt, mod_l, lw, tabs, n_lat):
    pair = isinstance(xt, tuple)
    bsz, d = (xt[0] if pair else xt).shape[0::2]
    stot = xt[0].shape[1] + xt[1].shape[1] if pair else xt.shape[1]
    tm = ROW_TILE
    nf = FOURIER_GROUPS * FOURIER_GROUP_DIM
    tok = lambda n: pl.BlockSpec((1, tm, n), lambda b, i: (b, i, 0))
    rope = pl.BlockSpec((tm, LANES), lambda b, i: (i, 0))
    head_out = pl.BlockSpec((1, N_HEADS, tm, LANES), lambda b, i: (b, 0, i, 0))
    x_specs, x_args = _token_specs(xt, tm, n_lat // tm)
    return pl.pallas_call(
        functools.partial(_inproj_kernel, n_src=len(x_args), n_lat_tiles=n_lat // tm),
        out_shape=(
            jax.ShapeDtypeStruct((bsz, N_HEADS, stot, LANES), BF16),
            jax.ShapeDtypeStruct((bsz, N_HEADS, stot, LANES), BF16),
            jax.ShapeDtypeStruct((bsz, N_HEADS, stot, LANES), BF16),
            jax.ShapeDtypeStruct((bsz, stot, nf), BF16),
        ),
        grid=(bsz, stot // tm),
        in_specs=x_specs + [
            _mod_spec(n_lat // tm, mod_l.shape[-1]), _const_spec((1, d)),
            _const_spec(lw["w_in"].shape), _const_spec(lw["gq"].shape), _const_spec(lw["gkv"].shape),
            _const_spec(lw["w_q"].shape), _const_spec(lw["w_kv"].shape), _const_spec(tabs["bd"].shape),
        ] + [_const_spec((1, LANES))] * 6 + [_const_spec((2, LANES)), rope, rope],
        out_specs=(head_out, head_out, head_out, tok(nf)),
        compiler_params=_params(),
        name="input_projection",
    )(*x_args, mod_l, lw["g1"], lw["w_in"], lw["gq"], lw["gkv"], lw["w_q"], lw["w_kv"], tabs["bd"],
      lw["qg"], lw["qg_sw"], lw["kg"], lw["kg_sw"], tabs["qoff"], lw["koff"], tabs["voff"],
      tabs["cos"], tabs["sin"])


def _scores(q, k_ref, hh, start, tk):
    k = k_ref[0, hh, pl.ds(start, tk), :]
    return lax.dot_general(q, k, (((1,), (1,)), ((), ())), preferred_element_type=F32)


def _attn_finish(accs, o_ref):
    even, odd = accs
    left = lax.broadcasted_iota(jnp.int32, even.shape, 1) < V_HEAD_DIM
    num = jnp.where(left, even, odd)
    den = pltpu.roll(jnp.where(left, odd, even), V_HEAD_DIM, axis=1)
    o_ref[0] = (num / den).astype(BF16)


def _attn_bounded_kernel(q_ref, k_ref, v_ref, o_ref, *, n_chunks, tk):
    tq = q_ref.shape[2]
    qs = [q_ref[0, hh] for hh in range(2)]

    def body(j, accs):
        start = pl.multiple_of(j * tk, tk)
        new = []
        for hh in range(2):
            p = jnp.exp2(_scores(qs[hh], k_ref, hh, start, tk)).astype(BF16)
            new.append(accs[hh] + _dot(p, v_ref[0, hh, pl.ds(start, tk), :]))
        return tuple(new)

    zero = jnp.zeros((tq, LANES), F32)
    unroll = 2 if n_chunks % 2 == 0 else 1
    _attn_finish(lax.fori_loop(0, n_chunks, body, (zero, zero), unroll=unroll), o_ref)


def _attn_online_kernel(q_ref, k_ref, v_ref, o_ref, *, n_chunks, tk):
    tq = q_ref.shape[2]
    qs = [q_ref[0, hh] for hh in range(2)]

    def body(j, carry):
        start = pl.multiple_of(j * tk, tk)
        new = []
        for hh in range(2):
            m, acc = carry[hh]
            s = _scores(qs[hh], k_ref, hh, start, tk)
            m_new = jnp.maximum(m, jnp.max(s, axis=-1, keepdims=True))
            p = jnp.exp2(s - m_new).astype(BF16)
            acc = jnp.exp2(m - m_new) * acc + _dot(p, v_ref[0, hh, pl.ds(start, tk), :])
            new.append((m_new, acc))
        return tuple(new)

    init = (jnp.full((tq, 1), -jnp.inf, F32), jnp.zeros((tq, LANES), F32))
    out = lax.fori_loop(0, n_chunks, body, (init, init))
    _attn_finish([out[0][1], out[1][1]], o_ref)


def _attention_calls(body, tag, q, k, v, n_lat, with_ctx):
    bsz, _, stot, _ = q.shape
    n_ctx = stot - n_lat
    hp = N_HEADS // 2
    tk = next(c for c in KV_CHUNKS if stot % c == 0)
    tq = Q_TILE
    nv = N_HEADS * V_HEAD_DIM
    kv_all = pl.BlockSpec((1, 2, stot, LANES), lambda b, h, i: (b, h, 0, 0))
    attn_lat = pl.pallas_call(
        functools.partial(body, n_chunks=stot // tk, tk=tk),
        out_shape=jax.ShapeDtypeStruct((bsz, n_lat, nv), BF16),
        grid=(bsz, hp, n_lat // tq),
        in_specs=[pl.BlockSpec((1, 2, tq, LANES), lambda b, h, i: (b, h, i, 0)), kv_all, kv_all],
        out_specs=pl.BlockSpec((1, tq, LANES), lambda b, h, i: (b, i, h)),
        compiler_params=_params(),
        name="latent_attention" + tag,
    )(q, k, v)
    if not with_ctx:
        return attn_lat
    cblk = n_lat // n_ctx
    ctx_rows = pl.BlockSpec((1, 2, n_ctx, LANES), lambda b, h: (b, h, cblk, 0))
    attn_ctx = pl.pallas_call(
        functools.partial(body, n_chunks=1, tk=n_ctx),
        out_shape=jax.ShapeDtypeStruct((bsz, n_ctx, nv), BF16),
        grid=(bsz, hp),
        in_specs=[ctx_rows, ctx_rows, ctx_rows],
        out_specs=pl.BlockSpec((1, n_ctx, LANES), lambda b, h: (b, 0, h)),
        compiler_params=_params(),
        name="context_attention" + tag,
    )(q, k, v)
    return attn_lat, attn_ctx


def _attention(q, k, v, bound, n_lat, with_ctx):
    out = lax.cond(
        bound <= MAX_DIRECT_BOUND,
        lambda: _attention_calls(_attn_bounded_kernel, "", q, k, v, n_lat, with_ctx),
        lambda: _attention_calls(_attn_online_kernel, "_online", q, k, v, n_lat, with_ctx))
    return out if with_ctx else (out, None)


def _fft1_kernel(cs_ref, x_ref, a_ref):
    a_ref[0] = _dot(cs_ref[...], x_ref[0]).astype(BF16)


def _fft2_kernel(ar_ref, ai_ref, m2_ref, wch_ref, o_ref):
    nf = wch_ref.shape[-1]
    n2 = ar_ref.shape[2]
    for j in range(ar_ref.shape[1]):
        slab = jnp.concatenate([ar_ref[0, j], ai_ref[0, j]], axis=0)
        z = _dot(m2_ref[j], slab)
        zc = jnp.concatenate([z[:n2], z[n2:]], axis=1).astype(BF16)
        o_ref[0, :, j * nf:(j + 1) * nf] = _dot(zc, wch_ref[...]).astype(BF16)


def _fft_ctx_kernel(cs_ref, x_ref, wch_ref, o_ref):
    n = x_ref.shape[1]
    z = _dot(cs_ref[...], x_ref[0])
    zc = jnp.concatenate([z[:n], z[n:]], axis=1).astype(BF16)
    o_ref[0] = _dot(zc, wch_ref[...]).astype(BF16)


def _fourier_mix(f, tabs, n_lat, with_ctx):
    bsz, stot, nf = f.shape
    n_ctx = stot - n_lat
    n2 = FFT_N2
    n1 = n_lat // n2
    kg = min(FFT_K1_GROUP, n1)
    tc = min(n2 * nf, 8192)
    a = pl.pallas_call(
        _fft1_kernel,
        out_shape=jax.ShapeDtypeStruct((bsz, 2 * n1, n2 * nf), BF16),
        grid=(bsz, n2 * nf // tc),
        in_specs=[
            pl.BlockSpec((2 * n1, n1), lambda b, j: (0, 0)),
            pl.BlockSpec((1, n1, tc), lambda b, j: (b, 0, j)),
        ],
        out_specs=pl.BlockSpec((1, 2 * n1, tc), lambda b, j: (b, 0, j)),
        compiler_params=_params(),
        name="fourier_stage1",
    )(tabs["cs1"], f.reshape(bsz, stot // n2, n2 * nf))
    a4 = a.reshape(bsz, 2 * n1, n2, nf)
    four_lat = pl.pallas_call(
        _fft2_kernel,
        out_shape=jax.ShapeDtypeStruct((bsz, n2, n1 * nf), BF16),
        grid=(bsz, n1 // kg),
        in_specs=[
            pl.BlockSpec((1, kg, n2, nf), lambda b, g: (b, g, 0, 0)),
            pl.BlockSpec((1, kg, n2, nf), lambda b, g: (b, n1 // kg + g, 0, 0)),
            pl.BlockSpec((kg, 2 * n2, 2 * n2), lambda b, g: (g, 0, 0)),
            pl.BlockSpec((2 * nf, nf), lambda b, g: (0, 0)),
        ],
        out_specs=pl.BlockSpec((1, n2, kg * nf), lambda b, g: (b, 0, g)),
        compiler_params=_params(),
        name="fourier_stage2",
    )(a4, a4, tabs["m2"], tabs["wch_lat"]).reshape(bsz, n_lat, nf)
    if not with_ctx:
        return four_lat, None
    cblk = n_lat // n_ctx
    four_ctx = pl.pallas_call(
        _fft_ctx_kernel,
        out_shape=jax.ShapeDtypeStruct((bsz, n_ctx, nf), BF16),
        grid=(bsz,),
        in_specs=[
            pl.BlockSpec((2 * n_ctx, n_ctx), lambda b: (0, 0)),
            pl.BlockSpec((1, n_ctx, nf), lambda b: (b, cblk, 0)),
            pl.BlockSpec((2 * nf, nf), lambda b: (0, 0)),
        ],
        out_specs=pl.BlockSpec((1, n_ctx, nf), lambda b: (b, 0, 0)),
        compiler_params=_params(),
        name="fourier_context",
    )(tabs["cs_ctx"], f, tabs["wch_ctx"])
    return four_lat, four_ctx


def _merge_kernel(*refs, moe, n_lat_tiles, n_srcs):
    tiles = []
    for n in n_srcs:
        tiles.append(_token_tile(refs[:n], n_lat_tiles))
        refs = refs[n:]
    x, a, f = tiles
    mod_ref, ga_ref, gf_ref, wout_ref, g2_ref = refs[:5]
    rest = refs[5:]
    d = x.shape[-1]
    m = mod_ref[0]
    gate1, shift2, scale2 = m[:, 2 * d:3 * d], m[:, 3 * d:4 * d], m[:, 4 * d:5 * d]
    an = _rms(a.astype(F32)) * ga_ref[...]
    fn = _rms(f.astype(F32)) * gf_ref[...]
    y = _dot(jnp.concatenate([an, fn], axis=1).astype(BF16), wout_ref[...])
    xn = x + gate1 * y
    h2 = _rms(xn) * (g2_ref[...] * (1.0 + scale2)) + shift2
    if not moe:
        wg_ref, wu_ref, wd_ref, xo_ref = rest
        gate2 = m[:, 5 * d:6 * d]
        xo_ref[0] = xn + gate2 * _swiglu_chunks(h2.astype(BF16), wg_ref, wu_ref, wd_ref, FFN_CHUNK)
        return
    wr_hi_ref, wr_lo_ref, xo_ref, h_ref, rt_ref = rest
    xo_ref[0] = xn
    _to_slabs(h_ref, h2)
    hi, lo = _split(h2)
    logits = _dot(hi, wr_hi_ref[...]) + _dot(lo, wr_hi_ref[...]) + _dot(hi, wr_lo_ref[...])
    lane = lax.broadcasted_iota(jnp.int32, logits.shape, 1).astype(F32)
    lg = jnp.where(lane < N_EXPERTS, logits, -jnp.inf)
    m1 = jnp.max(lg, axis=-1, keepdims=True)
    i1 = jnp.min(jnp.where(lg == m1, lane, float(LANES)), axis=-1, keepdims=True)
    lg2 = jnp.where(lane == i1, -jnp.inf, lg)
    m2 = jnp.max(lg2, axis=-1, keepdims=True)
    i2 = jnp.min(jnp.where(lg2 == m2, lane, float(LANES)), axis=-1, keepdims=True)
    e = jnp.exp(m2 - m1)
    w1 = 1.0 / (1.0 + e)
    w2 = e * w1
    rt_ref[0] = jnp.where(lane == 0, i1, jnp.where(lane == 1, i2, jnp.where(lane == 2, w1,
                                                                          jnp.where(lane == 3, w2, 0.0))))


def _merge(xt, attn, four, mod_l, lw, n_lat, moe):
    tm = ROW_TILE
    with_ctx = attn[1] is not None
    n_lat_tiles = n_lat // tm
    n_out = n_lat + (attn[1].shape[1] if with_ctx else 0)
    if not with_ctx:
        attn, four = attn[0], four[0]
        xt = xt[0] if isinstance(xt, tuple) else xt
    bsz, d = (xt[0] if isinstance(xt, tuple) else xt).shape[0::2]
    tok = lambda n: pl.BlockSpec((1, tm, n), lambda b, i: (b, i, 0))
    na, nf = lw["ga"].shape[-1], lw["gf"].shape[-1]
    in_specs, args, n_srcs = [], [], []
    for src in (xt, attn, four):
        specs, arrs = _token_specs(src, tm, n_lat_tiles)
        in_specs += specs
        args += arrs
        n_srcs.append(len(arrs))
    in_specs += [_mod_spec(n_lat_tiles, mod_l.shape[-1]),
                 _const_spec((1, na)), _const_spec((1, nf)), _const_spec((na + nf, d)), _const_spec((1, d))]
    args += [mod_l, lw["ga"], lw["gf"], lw["w_out"], lw["g2"]]
    out_shape = [jax.ShapeDtypeStruct((bsz, n_out, d), F32)]
    out_specs = [tok(d)]
    if moe:
        slab, tiles = d // LANES, n_out // tm
        in_specs += [_const_spec((d, LANES)), _const_spec((d, LANES))]
        args += [lw["wr_hi"], lw["wr_lo"]]
        out_shape += [jax.ShapeDtypeStruct((bsz * n_out * slab, LANES), F32),
                      jax.ShapeDtypeStruct((bsz, n_out, LANES), F32)]
        out_specs += [pl.BlockSpec((tm * slab, LANES), lambda b, i: (b * tiles + i, 0)), tok(LANES)]
    else:
        in_specs += [_const_spec(lw[k].shape) for k in ("w_gate", "w_up", "w_down")]
        args += [lw["w_gate"], lw["w_up"], lw["w_down"]]
    return pl.pallas_call(
        functools.partial(_merge_kernel, moe=moe, n_lat_tiles=n_lat_tiles, n_srcs=tuple(n_srcs)),
        out_shape=tuple(out_shape),
        grid=(bsz, n_out // tm),
        in_specs=in_specs,
        out_specs=tuple(out_specs),
        compiler_params=_params(),
        name="merge_router" if moe else "merge_swiglu",
    )(*args)


def _swiglu_chunks(h, wg_ref, wu_ref, wd_ref, chunk, lead=(), after_chunk=None):
    f = wg_ref.shape[-1]
    acc = None
    for c in range(f // chunk):
        sl = slice(c * chunk, (c + 1) * chunk)
        g = _dot(h, wg_ref[lead + (slice(None), sl)])
        u = _dot(h, wu_ref[lead + (slice(None), sl)])
        part = _dot((_silu(g) * u).astype(BF16), wd_ref[lead + (sl, slice(None))])
        acc = part if acc is None else acc + part
        if after_chunk is not None:
            after_chunk(c)
    return acc


def _to_slabs(ref, val):
    rows, n = val.shape[0], val.shape[1] // LANES
    for s in range(n):
        ref[pl.ds(s, rows, stride=n), :] = val[:, s * LANES:(s + 1) * LANES]


def _from_slabs(ref, n):
    rows = ref.shape[0] // n
    return jnp.concatenate([ref[pl.ds(s, rows, stride=n), :] for s in range(n)], axis=1)


def _start_rows(n_rows, make_copy):
    def start(r, c):
        for j, cp in enumerate(make_copy(r)):
            cp.start(priority=j % 2)
        return c

    lax.fori_loop(0, n_rows, start, 0, unroll=DMA_UNROLL)


def _dispatch_kernel(zb_ref, dest_ref, h_ref, xs_ref, zbuf, zsems, sems, *, slab):
    tm = h_ref.shape[0] // slab

    @pl.when(pl.program_id(0) == 0)
    def _():
        zbuf[...] = jnp.zeros(zbuf.shape, zbuf.dtype)
        rows = zbuf.shape[0]

        def clear(i):
            start = pl.multiple_of(zb_ref[0, 0, i] * rows, rows)
            return pltpu.make_async_copy(zbuf, xs_ref.at[pl.ds(start, rows), :], zsems.at[i])

        for i in range(zsems.shape[0]):
            pl.when(zb_ref[0, 0, i] >= 0)(lambda i=i: clear(i).start())
        for i in range(zsems.shape[0]):
            pl.when(zb_ref[0, 0, i] >= 0)(lambda i=i: clear(i).wait())

    def copies(r):
        src = h_ref.at[pl.ds(pl.multiple_of(r * slab, slab), slab), :]
        return [pltpu.make_async_copy(
            src, xs_ref.at[pl.ds(pl.multiple_of(dest_ref[0, 0, TOP_K * r + j] * slab, slab), slab), :],
            sems.at[j]) for j in range(TOP_K)]

    _start_rows(tm, copies)
    for j in range(TOP_K):
        pltpu.make_async_copy(h_ref, xs_ref.at[pl.ds(0, tm * slab), :], sems.at[j]).wait()


def _dispatch(h2, dest, zero_blocks, n_rows, slab):
    tm = ROW_TILE
    n_tiles = h2.shape[0] // (tm * slab)
    nz = zero_blocks.shape[0]
    return pl.pallas_call(
        functools.partial(_dispatch_kernel, slab=slab),
        out_shape=jax.ShapeDtypeStruct((n_rows * slab, LANES), h2.dtype),
        grid=(n_tiles,),
        in_specs=[
            pl.BlockSpec((1, 1, nz), lambda n: (0, 0, 0), memory_space=pltpu.SMEM),
            pl.BlockSpec((1, 1, TOP_K * tm), lambda n: (n, 0, 0), memory_space=pltpu.SMEM),
            pl.BlockSpec((tm * slab, LANES), lambda n: (n, 0)),
        ],
        out_specs=pl.BlockSpec(memory_space=pl.ANY),
        scratch_shapes=[pltpu.VMEM((EXPERT_ROWS * slab, LANES), h2.dtype), pltpu.SemaphoreType.DMA((nz,)),
                        pltpu.SemaphoreType.DMA((TOP_K,))],
        compiler_params=_params(),
        name="moe_dispatch",
    )(zero_blocks.reshape(1, 1, nz), dest.reshape(n_tiles, 1, TOP_K * tm), h2)


def _expert_kernel(be_ref, nu_ref, x_ref, wg_hbm, wu_hbm, wd_hbm, o_ref, wg_s, wu_s, wd_s, st_g, st_u, st_d,
                   sems, *, slab, layer):
    n = pl.program_id(0)
    n_used = nu_ref[0]
    used = n < n_used
    e = be_ref[n]
    nxt = be_ref[jnp.minimum(n + 1, pl.num_programs(0) - 1)]
    hand_over = jnp.logical_and(n + 1 < n_used, nxt != e)
    n_chunks = wg_s.shape[1] // MOE_CHUNK

    def cols(c):
        return pl.ds(c * MOE_CHUNK, MOE_CHUNK)

    def copies(ex, c):
        slot = c % 2
        return [pltpu.make_async_copy(wg_hbm.at[layer, ex, :, cols(c)], st_g.at[slot], sems.at[0, slot]),
                pltpu.make_async_copy(wu_hbm.at[layer, ex, :, cols(c)], st_u.at[slot], sems.at[1, slot]),
                pltpu.make_async_copy(wd_hbm.at[layer, ex, cols(c), :], st_d.at[slot], sems.at[2, slot])]

    def start(ex, c):
        for cp in copies(ex, c):
            cp.start()

    def finish(ex, c):
        for cp in copies(ex, c):
            cp.wait()
        wg_s[:, cols(c)] = st_g[c % 2].astype(BF16)
        wu_s[:, cols(c)] = st_u[c % 2].astype(BF16)
        wd_s[cols(c), :] = st_d[c % 2].astype(BF16)

    @pl.when(n == 0)
    def _():
        start(e, 0)
        for c in range(n_chunks):
            if c + 1 < n_chunks:
                start(e, c + 1)
            finish(e, c)

    @pl.when(jnp.logical_and(used, jnp.logical_not(hand_over)))
    def _():
        x = _from_slabs(x_ref, slab).astype(BF16)
        _to_slabs(o_ref, _swiglu_chunks(x, wg_s, wu_s, wd_s, MOE_CHUNK))

    @pl.when(hand_over)
    def _():
        def replace(c):
            if c + 1 < n_chunks:
                start(nxt, c + 1)
            finish(nxt, c)

        x = _from_slabs(x_ref, slab).astype(BF16)
        start(nxt, 0)
        _to_slabs(o_ref, _swiglu_chunks(x, wg_s, wu_s, wd_s, MOE_CHUNK, after_chunk=replace))

    @pl.when(jnp.logical_not(used))
    def _():
        o_ref[...] = jnp.zeros(o_ref.shape, o_ref.dtype)


def _expert_blocks(xs, block_exp, n_used, lw, slab):
    tm = EXPERT_ROWS
    d, f = lw["w_moe_gate"].shape[-2:]
    rows = pl.BlockSpec((tm * slab, LANES), lambda n, be, nu: (n, 0))
    hbm = pl.BlockSpec(memory_space=pl.ANY)
    return pl.pallas_call(
        functools.partial(_expert_kernel, slab=slab, layer=lw["moe_index"]),
        out_shape=jax.ShapeDtypeStruct(xs.shape, F32),
        grid_spec=pltpu.PrefetchScalarGridSpec(
            num_scalar_prefetch=2,
            grid=(xs.shape[0] // (tm * slab),),
            in_specs=[rows, hbm, hbm, hbm],
            out_specs=rows,
            scratch_shapes=[pltpu.VMEM((d, f), BF16), pltpu.VMEM((d, f), BF16), pltpu.VMEM((f, d), BF16),
                            pltpu.VMEM((2, d, MOE_CHUNK), F32), pltpu.VMEM((2, d, MOE_CHUNK), F32),
                            pltpu.VMEM((2, MOE_CHUNK, d), F32), pltpu.SemaphoreType.DMA((3, 2))],
        ),
        compiler_params=_params(),
        name="moe_expert_blocks",
    )(block_exp, n_used, xs, lw["w_moe_gate"], lw["w_moe_up"], lw["w_moe_down"])


def _combine_kernel(pos_ref, pos_next_ref, ys_ref, x_ref, rt_ref, mod_ref, o_ref, buf, sems, *, slab):
    tm, d = x_ref.shape[1], x_ref.shape[2]
    step = pl.program_id(0) * pl.num_programs(1) + pl.program_id(1)
    n_steps = pl.num_programs(0) * pl.num_programs(1)
    slot = step % 2

    def copies(idx_ref, slot_):
        def make(r):
            return [pltpu.make_async_copy(
                ys_ref.at[pl.ds(pl.multiple_of(idx_ref[0, 0, TOP_K * r + j] * slab, slab), slab), :],
                buf.at[slot_, j, pl.ds(pl.multiple_of(r * slab, slab), slab), :],
                sems.at[slot_, j]) for j in range(TOP_K)]
        return make

    pl.when(step == 0)(lambda: _start_rows(tm, copies(pos_ref, slot)))
    pl.when(step + 1 < n_steps)(lambda: _start_rows(tm, copies(pos_next_ref, 1 - slot)))
    for j in range(TOP_K):
        pltpu.make_async_copy(ys_ref.at[pl.ds(0, tm * slab), :], buf.at[slot, j], sems.at[slot, j]).wait()
    gate2 = mod_ref[0][:, 5 * d:6 * d]
    rt = rt_ref[0]
    y = rt[:, 2:3] * _from_slabs(buf.at[slot, 0], slab) + rt[:, 3:4] * _from_slabs(buf.at[slot, 1], slab)
    o_ref[0] = x_ref[0] + gate2 * y


def _combine(ys, pos, xt, rt, mod_l, n_lat, n_out, slab):
    bsz, stot, d = xt.shape
    tm = ROW_TILE
    tiles = stot // tm
    tok = lambda n: pl.BlockSpec((1, tm, n), lambda b, i: (b, i, 0))
    last = bsz * tiles - 1
    pos = pos.reshape(bsz * tiles, 1, TOP_K * tm)
    return pl.pallas_call(
        functools.partial(_combine_kernel, slab=slab),
        out_shape=jax.ShapeDtypeStruct((bsz, n_out, d), F32),
        grid=(bsz, n_out // tm),
        in_specs=[
            pl.BlockSpec((1, 1, TOP_K * tm), lambda b, i: (b * tiles + i, 0, 0), memory_space=pltpu.SMEM),
            pl.BlockSpec((1, 1, TOP_K * tm), lambda b, i: (jnp.minimum(b * tiles + i + 1, last), 0, 0),
                         memory_space=pltpu.SMEM),
            pl.BlockSpec(memory_space=pl.ANY),
            tok(d), tok(LANES), _mod_spec(n_lat // tm, mod_l.shape[-1]),
        ],
        out_specs=tok(d),
        scratch_shapes=[pltpu.VMEM((2, TOP_K, tm * slab, LANES), F32), pltpu.SemaphoreType.DMA((2, TOP_K))],
        compiler_params=_params(),
        name="moe_combine",
    )(pos, pos, ys, xt, rt, mod_l)


def _moe_ffn(xt, h2, rt, mod_l, lw, n_lat, n_out):
    bsz, stot, d = xt.shape
    slab = d // LANES
    n_assign = bsz * stot * TOP_K
    tm = EXPERT_ROWS
    exp_flat = rt[..., :TOP_K].astype(jnp.int32).reshape(n_assign)
    onehot = (exp_flat[:, None] == jnp.arange(N_EXPERTS, dtype=jnp.int32)[None, :]).astype(jnp.int32)
    csum = jnp.cumsum(onehot, axis=0)
    rank = jnp.sum(jnp.where(onehot > 0, csum, 0), axis=1) - 1
    counts = csum[-1]
    padded = (counts + tm - 1) // tm * tm
    pad_ends = jnp.cumsum(padded)
    pad_starts = pad_ends - padded
    dest = (jnp.sum(onehot * pad_starts[None, :], axis=1) + rank).astype(jnp.int32)
    n_blocks = (n_assign + N_EXPERTS * (tm - 1) + tm - 1) // tm
    block_start = jnp.arange(n_blocks, dtype=jnp.int32) * tm
    block_exp = jnp.minimum(jnp.sum((pad_ends[None, :] <= block_start[:, None]).astype(jnp.int32), axis=1),
                            N_EXPERTS - 1).astype(jnp.int32)
    n_used = (pad_ends[-1:] // tm).astype(jnp.int32)
    last_blocks = jnp.where(padded > 0, pad_ends // tm - 1, -1)
    tail = n_used + jnp.arange(N_EXPERTS, dtype=jnp.int32)
    zero_blocks = jnp.concatenate([last_blocks, jnp.where(tail < n_blocks, tail, -1)]).astype(jnp.int32)

    xs = _dispatch(h2, dest, zero_blocks, n_blocks * tm, slab)
    ys = _expert_blocks(xs, block_exp, n_used, lw, slab)
    return _combine(ys, dest, xt, rt, mod_l, n_lat, n_out, slab)


def _pair_swap_index():
    j = jnp.arange(QK_ROPE_DIM)
    return jnp.where((j % 16) < 8, j + 8, j - 8)


def _head_lanes(v):
    pad = [(0, 0)] * (v.ndim - 1) + [(0, LANES - QK_HEAD_DIM)]
    return jnp.pad(v, pad)


def _swap_rope(v):
    rope = v[..., QK_NOPE_DIM:][..., _pair_swap_index()]
    return _head_lanes(jnp.concatenate([jnp.zeros_like(v[..., :QK_NOPE_DIM]), rope], axis=-1))


def _dft_angles(n_rows, n_cols, period):
    idx = (jnp.arange(n_rows, dtype=jnp.int32)[:, None] * jnp.arange(n_cols, dtype=jnp.int32)[None, :]) % period
    return idx.astype(F32) * (2.0 * math.pi / period)


def _tables(n_lat, n_ctx):
    nf = FOURIER_GROUPS * FOURIER_GROUP_DIM
    rows = n_lat // GRID_W
    r = jnp.repeat(jnp.arange(rows, dtype=F32), GRID_W)
    col = jnp.tile(jnp.arange(GRID_W, dtype=F32), rows)
    half = QK_ROPE_DIM // 2
    inv_freq = ROPE_THETA ** (-jnp.arange(0, half, 2, dtype=F32) / half)
    ar, ac = r[:, None] * inv_freq, col[:, None] * inv_freq
    ones = jnp.ones((n_lat, QK_NOPE_DIM), F32)
    cos = jnp.concatenate([ones, jnp.cos(ar), jnp.cos(ar), jnp.cos(ac), jnp.cos(ac)], axis=1)
    sin = jnp.concatenate([0 * ones, -jnp.sin(ar), jnp.sin(ar), -jnp.sin(ac), jnp.sin(ac)], axis=1)
    cos = jnp.concatenate([cos, jnp.ones((n_ctx, QK_HEAD_DIM), F32)], axis=0)
    sin = jnp.concatenate([sin, jnp.zeros((n_ctx, QK_HEAD_DIM), F32)], axis=0)
    lane = jnp.arange(2 * LANES)
    bd = (lane[:, None] // LANES == lane[None, :] // LANES).astype(BF16)
    n2 = FFT_N2
    n1 = n_lat // n2
    a1 = _dft_angles(n1, n1, n1)
    cs1 = jnp.concatenate([jnp.cos(a1), -jnp.sin(a1)], axis=0).astype(BF16)
    k = (jnp.arange(n1, dtype=jnp.int32)[:, None, None] + n1 * jnp.arange(n2, dtype=jnp.int32)[None, :, None])
    ang = ((k * jnp.arange(n2, dtype=jnp.int32)[None, None, :]) % n_lat).astype(F32) * (2.0 * math.pi / n_lat)
    c2, s2 = jnp.cos(ang), jnp.sin(ang)
    m2 = jnp.concatenate([jnp.concatenate([c2, s2], axis=2), jnp.concatenate([-s2, c2], axis=2)], axis=1)
    ach = _dft_angles(FOURIER_GROUP_DIM, FOURIER_GROUP_DIM, FOURIER_GROUP_DIM)
    eye = jnp.eye(FOURIER_GROUPS, dtype=F32)
    wch = jnp.concatenate([jnp.kron(eye, jnp.cos(ach)), jnp.kron(eye, jnp.sin(ach))], axis=0)
    actx = _dft_angles(n_ctx, n_ctx, n_ctx)
    return {
        "cos": _head_lanes(cos), "sin": _head_lanes(sin), "bd": bd,
        "qoff": jnp.zeros((1, LANES), F32).at[0, QK_HEAD_DIM].set(1.0),
        "voff": jnp.stack([(jnp.arange(LANES) >= V_HEAD_DIM), (jnp.arange(LANES) < V_HEAD_DIM)]).astype(F32),
        "cs1": cs1, "m2": m2.astype(BF16),
        "wch_lat": (wch * (n_lat * FOURIER_GROUP_DIM) ** -0.5).astype(BF16),
        "wch_ctx": (wch * (n_ctx * FOURIER_GROUP_DIM) ** -0.5).astype(BF16),
        "cs_ctx": jnp.concatenate([jnp.cos(actx), -jnp.sin(actx)], axis=0).astype(BF16),
    }


def _layer_weights(layer, p):
    q_rank = p["q_lora_norm"].shape[-1]
    kv_rank = p["kv_lora_norm"].shape[-1]
    o_pe = q_rank + kv_rank
    o_f = o_pe + QK_ROPE_DIM
    w_in = p["w_in"][layer]
    d = w_in.shape[0]
    w_pe = w_in[:, o_pe:o_f]
    z64 = jnp.zeros((d, QK_NOPE_DIM), F32)
    z32 = jnp.zeros((d, LANES - QK_HEAD_DIM), F32)
    w_in_cat = jnp.concatenate(
        [w_in[:, :o_pe], w_in[:, o_f:], z64, w_pe, z32, z64, w_pe[:, _pair_swap_index()], z32], axis=1)
    w_uq = p["w_uq"][layer].reshape(q_rank, N_HEADS, QK_HEAD_DIM)
    w_q = jnp.concatenate([_head_lanes(w_uq).reshape(q_rank, -1), _swap_rope(w_uq).reshape(q_rank, -1)], axis=1)
    w_ukv = p["w_ukv"][layer].reshape(kv_rank, N_HEADS, QK_NOPE_DIM + V_HEAD_DIM)
    w_k = jnp.pad(w_ukv[..., :QK_NOPE_DIM], ((0, 0), (0, 0), (0, LANES - QK_NOPE_DIM))).reshape(kv_rank, -1)
    w_v = w_ukv[..., QK_NOPE_DIM:]
    zv = jnp.zeros_like(w_v)
    odd = (jnp.arange(N_HEADS) % 2 == 1)[None, :, None]
    w_v = jnp.where(odd, jnp.concatenate([zv, w_v], axis=-1), jnp.concatenate([w_v, zv], axis=-1))
    w_v = w_v.reshape(kv_rank, -1)
    row = lambda v: v.reshape(1, -1)
    bound = LOG2E * (1.01 * QK_HEAD_DIM ** 0.5 * jnp.max(jnp.abs(p["q_norm"][layer]))
                     * jnp.max(jnp.abs(p["k_norm"][layer])) + 0.1)
    lw = {
        "g1": row(p["norm1"][layer]), "g2": row(p["norm2"][layer]),
        "w_in": w_in_cat.astype(BF16),
        "gq": row(p["q_lora_norm"][layer]), "gkv": row(p["kv_lora_norm"][layer]),
        "w_q": w_q.astype(BF16), "w_kv": jnp.concatenate([w_k, w_v], axis=1).astype(BF16),
        "bound": bound, "koff": jnp.zeros((1, LANES), F32).at[0, QK_HEAD_DIM].set(-bound),
        "qg": row(_head_lanes(p["q_norm"][layer])), "qg_sw": row(_swap_rope(p["q_norm"][layer])),
        "kg": row(_head_lanes(p["k_norm"][layer])), "kg_sw": row(_swap_rope(p["k_norm"][layer])),
        "ga": row(p["out_norm_attn"][layer]), "gf": row(p["out_norm_fourier"][layer]),
        "w_out": p["w_out"][layer].astype(BF16),
    }
    if layer % 2 == 0:
        lw.update(w_gate=p["w_ffn_gate"][layer // 2].astype(BF16), w_up=p["w_ffn_up"][layer // 2].astype(BF16),
                  w_down=p["w_ffn_down"][layer // 2].astype(BF16))
    else:
        wr = jnp.pad(p["w_router"][layer // 2], ((0, 0), (0, LANES - N_EXPERTS)))
        wr_hi, wr_lo = _split(wr)
        lw.update(wr_hi=wr_hi, wr_lo=wr_lo,
                  moe_index=layer // 2, w_moe_gate=p["w_moe_gate"], w_moe_up=p["w_moe_up"],
                  w_moe_down=p["w_moe_down"])
    return lw


def kernel(x, c, ctx, c_ctx, w_ada, b_ada, norm1, w_in, q_lora_norm, kv_lora_norm, w_uq, w_ukv, q_norm, k_norm,
           out_norm_attn, out_norm_fourier, w_out, norm2, w_ffn_gate, w_ffn_up, w_ffn_down, w_router,
           w_moe_gate, w_moe_up, w_moe_down):
    params = dict(norm1=norm1, w_in=w_in, q_lora_norm=q_lora_norm, kv_lora_norm=kv_lora_norm, w_uq=w_uq,
                  w_ukv=w_ukv, q_norm=q_norm, k_norm=k_norm, out_norm_attn=out_norm_attn,
                  out_norm_fourier=out_norm_fourier, w_out=w_out, norm2=norm2, w_ffn_gate=w_ffn_gate,
                  w_ffn_up=w_ffn_up, w_ffn_down=w_ffn_down, w_router=w_router, w_moe_gate=w_moe_gate,
                  w_moe_up=w_moe_up, w_moe_down=w_moe_down)
    bsz, n_lat, d = x.shape
    n_ctx = ctx.shape[1]
    depth = w_ada.shape[0]
    assert n_lat % Q_TILE == 0 and n_lat % n_ctx == 0 and n_ctx % ROW_TILE == 0
    assert n_lat % FFT_N2 == 0 and n_lat % GRID_W == 0 and n_ctx % FFT_N2 == 0

    cond = jnp.concatenate([c, c_ctx[None, :], jnp.zeros((8 - bsz - 1, d), F32)], axis=0)
    mod = _modulation(cond, w_ada, b_ada)
    tabs = _tables(n_lat, n_ctx)
    xt = (x, ctx)

    for layer in range(depth):
        last = layer == depth - 1
        lw = _layer_weights(layer, params)
        mod_l = mod[layer].reshape(mod.shape[1], 1, mod.shape[2])
        q, k, v, f = _input_projection(xt, mod_l, lw, tabs, n_lat)
        attn = _attention(q, k, v, lw["bound"], n_lat, not last)
        four = _fourier_mix(f, tabs, n_lat, not last)
        moe = layer % 2 == 1
        n_out = n_lat if last else n_lat + n_ctx
        if moe:
            xt, h2, rt = _merge(xt, attn, four, mod_l, lw, n_lat, True)
            xt = _moe_ffn(xt, h2, rt, mod_l, lw, n_lat, n_out)
        else:
            (xt,) = _merge(xt, attn, four, mod_l, lw, n_lat, False)
    return xt[:, :n_lat]
```
